```python
import math
import jax
import jax.numpy as jnp
from jax import lax
import numpy as np

D_MODEL = 1024
BATCH = 16
SEQ = 2048
DEPTH = 4

N_MEM = 256
N_GROUPS = 4
GW = D_MODEL // N_GROUPS
D_MIX = N_GROUPS * GW
HEAD_DIM = 64
S5_GROUP = 16
S5_GROUPS = GW // S5_GROUP
S5_STATE = 64
ML_HEADS = GW // HEAD_DIM
ML_CHUNK = 64
CONV_K = 4
RW_HEADS = GW // HEAD_DIM
RW_DECAY_RANK = 64
RW_A_RANK = 64
RW_GATE_RANK = 128
RW_GN_EPS = 64e-5
MB_HEADS = GW // HEAD_DIM
MOBA_BLOCK = 256
MOBA_TOPK = 3
MOBA_QCHUNK = 16
REL_BUCKETS = 32
REL_MAX_DIST = 128
XA_HEADS = 4
XA_HEAD_DIM = D_MODEL // XA_HEADS
D_FF = ((8 * D_MODEL + 3 * 256 - 1) // (3 * 256)) * 256
ML_COLS = 4 * GW + 2 * ML_HEADS
RW_COLS = 3 * GW + RW_DECAY_RANK + RW_A_RANK + RW_GATE_RANK
MB_COLS = 3 * GW
D_IN = GW + ML_COLS + RW_COLS + MB_COLS
RMS_EPS = 1e-6

kernel_name = 'hybrid_s5_mlstm_rwkv7_moba_trunk'


def rms_norm(x, g):
    xf = x.astype(jnp.float32)
    y = xf * lax.rsqrt(jnp.mean(xf * xf, axis=-1, keepdims=True) + RMS_EPS)
    return (y * g.astype(jnp.float32)).astype(x.dtype)


def split_cols(t, sizes):
    out, start = [], 0
    for s in sizes:
        out.append(t[..., start:start + s])
        start += s
    return out


def causal_dwconv(x, w):
    k, c = w.shape
    return lax.conv_general_dilated(x, w[:, None, :].astype(x.dtype), window_strides=(1,),
                                    padding=[(k - 1, 0)], dimension_numbers=('NWC', 'WIO', 'NWC'),
                                    feature_group_count=c)


def t5_bucket(rel):
    n = jnp.maximum(rel, 0)
    max_exact = REL_BUCKETS // 2
    nf = jnp.maximum(n, 1).astype(jnp.float32)
    large = max_exact + (jnp.log(nf / max_exact) / math.log(REL_MAX_DIST / max_exact)
                         * (REL_BUCKETS - max_exact)).astype(jnp.int32)
    return jnp.where(n < max_exact, n, jnp.minimum(large, REL_BUCKETS - 1))


def _complex_affine_combine(e1, e2):
    a1r, a1i, b1r, b1i = e1
    a2r, a2i, b2r, b2i = e2
    return (a2r * a1r - a2i * a1i, a2r * a1i + a2i * a1r,
            a2r * b1r - a2i * b1i + b2r, a2r * b1i + a2i * b1r + b2i)


def s5_mixer(u, lam_re, lam_im, log_step, b_re, b_im, c_re, c_im, d_skip, w_glu, b_glu):
    f32 = jnp.float32
    bsz, seq, _ = u.shape
    ug = u.astype(f32).reshape(bsz, seq, S5_GROUPS, S5_GROUP)
    lr, li = lam_re.astype(f32), lam_im.astype(f32)
    step = jnp.exp(log_step.astype(f32))[:, None]
    mag = jnp.exp(lr * step)
    ang = li * step
    ab_re, ab_im = mag * jnp.cos(ang), mag * jnp.sin(ang)
    den = lr * lr + li * li
    co_re = ((ab_re - 1.0) * lr + ab_im * li) / den
    co_im = (ab_im * lr - (ab_re - 1.0) * li) / den
    br, bi = b_re.astype(f32), b_im.astype(f32)
    bb_re = co_re[..., None] * br - co_im[..., None] * bi
    bb_im = co_re[..., None] * bi + co_im[..., None] * br
    bu_re = jnp.einsum('blgh,gph->blgp', ug, bb_re)
    bu_im = jnp.einsum('blgh,gph->blgp', ug, bb_im)
    a_re = jnp.broadcast_to(ab_re, (1, seq) + ab_re.shape)
    a_im = jnp.broadcast_to(ab_im, (1, seq) + ab_im.shape)
    _, _, xs_re, xs_im = lax.associative_scan(_complex_affine_combine,
                                              (a_re, a_im, bu_re, bu_im), axis=1)
    y = (jnp.einsum('blgp,ghp->blgh', xs_re, c_re.astype(f32))
         - jnp.einsum('blgp,ghp->blgh', xs_im, c_im.astype(f32)))
    y = y.reshape(bsz, seq, GW) + d_skip.astype(f32) * u.astype(f32)
    z = jax.nn.gelu(y)
    return z * jax.nn.sigmoid(z @ w_glu.astype(f32) + b_glu.astype(f32))


def mlstm_mixer(p, conv_w, i_bias, f_bias):
    f32 = jnp.float32
    bsz, seq, _ = p.shape
    nc, cl, nh, dh = seq // ML_CHUNK, ML_CHUNK, ML_HEADS, HEAD_DIM
    q, k, v, o, i_pre, f_pre = split_cols(p, (GW, GW, GW, GW, nh, nh))
    qk = jax.nn.silu(causal_dwconv(jnp.concatenate([q, k], axis=-1), conv_w))
    q, k = qk[..., :GW], qk[..., GW:]

    def heads(t):
        return t.astype(f32).reshape(bsz, nc, cl, nh, dh).transpose(0, 3, 1, 2, 4)

    def gate(t, bias):
        return (t.astype(f32) + bias.astype(f32)).reshape(bsz, nc, cl, nh).transpose(0, 3, 1, 2)

    qh, kh, vh = heads(q), heads(k) * dh ** -0.5, heads(v)
    log_i = gate(i_pre, i_bias)
    bcum = jnp.cumsum(jax.nn.log_sigmoid(gate(f_pre, f_bias)), axis=-1)
    causal = jnp.tril(jnp.ones((cl, cl), dtype=bool))
    dmat = jnp.where(causal, bcum[..., :, None] - bcum[..., None, :] + log_i[..., None, :], -jnp.inf)
    b_last = bcum[..., -1]
    w_st = b_last[..., None] - bcum + log_i
    m_in = jnp.max(w_st, axis=-1)
    e_st = jnp.exp(w_st - m_in[..., None])
    kv_ch = jnp.einsum('bhcs,bhcsk,bhcsv->bhckv', e_st, kh, vh)
    k_ch = jnp.einsum('bhcs,bhcsk->bhck', e_st, kh)

    def chunk_step(carry, xs):
        c_st, n_st, m_st = carry
        bl, mi, kv, ks = xs
        m_new = jnp.maximum(bl + m_st, mi)
        a = jnp.exp(bl + m_st - m_new)
        e = jnp.exp(mi - m_new)
        c_new = a[..., None, None] * c_st + e[..., None, None] * kv
        n_new = a[..., None] * n_st + e[..., None] * ks
        return (c_new, n_new, m_new), (c_st, n_st, m_st)

    init = (jnp.zeros((bsz, nh, dh, dh), f32), jnp.zeros((bsz, nh, dh), f32), jnp.zeros((bsz, nh), f32))
    xs = tuple(jnp.moveaxis(t, 2, 0) for t in (b_last, m_in, kv_ch, k_ch))
    _, (c_prev, n_prev, m_prev) = lax.scan(chunk_step, init, xs)
    c_prev = jnp.moveaxis(c_prev, 0, 2)
    n_prev = jnp.moveaxis(n_prev, 0, 2)
    m_prev = jnp.moveaxis(m_prev, 0, 2)
    b_inter = bcum + m_prev[..., None]
    m_comb = jnp.maximum(b_inter, jnp.max(dmat, axis=-1))
    w_inter = jnp.exp(b_inter - m_comb)
    s = jnp.exp(dmat - m_comb[..., None]) * jnp.einsum('bhctd,bhcsd->bhcts', qh, kh)
    num = (jnp.einsum('bhcts,bhcsv->bhctv', s, vh)
           + w_inter[..., None] * jnp.einsum('bhctk,bhckv->bhctv', qh, c_prev))
    den = jnp.sum(s, axis=-1) + w_inter * jnp.einsum('bhctk,bhck->bhct', qh, n_prev)
    h = num / jnp.maximum(jnp.abs(den), jnp.exp(-m_comb))[..., None]
    h = h.transpose(0, 2, 3, 1, 4).reshape(bsz, seq, GW)
    return jax.nn.sigmoid(o.astype(f32)) * h


def rwkv7_mixer(p, mu, w0, w2, a0, a2, g2, k_k, k_a, r_k, ln_w, ln_b):
    f32 = jnp.float32
    bsz, seq, _ = p.shape
    nh, dh = RW_HEADS, HEAD_DIM
    p = p.astype(f32)
    prev = jnp.pad(p, ((0, 0), (1, 0), (0, 0)))[:, :-1]
    p = p + (prev - p) * mu.astype(f32)
    r, k, v, w_lo, a_lo, g_lo = split_cols(p, (GW, GW, GW, RW_DECAY_RANK, RW_A_RANK, RW_GATE_RANK))
    w = -jax.nn.softplus(-(w0.astype(f32) + jnp.tanh(w_lo) @ w2.astype(f32))) - 0.5
    decay = jnp.exp(-jnp.exp(w))
    a = jax.nn.sigmoid(a0.astype(f32) + a_lo @ a2.astype(f32))
    g = jax.nn.sigmoid(g_lo) @ g2.astype(f32)

    def heads(t):
        return t.reshape(bsz, seq, nh, dh)

    kk = heads(k * k_k.astype(f32))
    kk = kk * lax.rsqrt(jnp.sum(kk * kk, axis=-1, keepdims=True) + 1e-12)
    k = k * (1.0 + (a - 1.0) * k_a.astype(f32))
    rh, kh, vh, wh, ah = heads(r), heads(k), heads(v), heads(decay), heads(a)

    def time_step(state, xs):
        r_t, w_t, k_t, v_t, kk_t, kka_t = xs
        sa = jnp.einsum('bhvk,bhk->bhv', state, -kk_t)
        state = (state * w_t[:, :, None, :] + sa[..., None] * kka_t[:, :, None, :]
                 + v_t[..., None] * k_t[:, :, None, :])
        return state, jnp.einsum('bhvk,bhk->bhv', state, r_t)

    xs = tuple(jnp.moveaxis(t, 1, 0) for t in (rh, wh, kh, vh, kk, kk * ah))
    _, y = lax.scan(time_step, jnp.zeros((bsz, nh, dh, dh), f32), xs)
    y = jnp.moveaxis(y, 0, 1)
    mean = jnp.mean(y, axis=-1, keepdims=True)
    var = jnp.mean(jnp.square(y - mean), axis=-1, keepdims=True)
    y = ((y - mean) * lax.rsqrt(var + RW_GN_EPS) * ln_w.astype(f32).reshape(nh, dh)
         + ln_b.astype(f32).reshape(nh, dh))
    bonus = jnp.sum(rh * kh * r_k.astype(f32).reshape(nh, dh), axis=-1, keepdims=True) * vh
    return (y + bonus).reshape(bsz, seq, GW) * g


def moba_mixer(p, rel_bias):
    f32 = jnp.float32
    bsz, seq, _ = p.shape
    nh, dh, bs, qc = MB_HEADS, HEAD_DIM, MOBA_BLOCK, MOBA_QCHUNK
    nb = -(-seq // bs)
    k_sel_n = max(1, min(MOBA_TOPK, nb - 1))
    nqc = seq // qc
    q, k, v = split_cols(p, (GW, GW, GW))

    def heads(t):
        return t.astype(f32).reshape(bsz, seq, nh, dh).transpose(0, 2, 1, 3)

    qh = heads(q) * dh ** -0.5
    pad = ((0, 0), (0, 0), (0, nb * bs - seq), (0, 0))
    kb = jnp.pad(heads(k), pad).reshape(bsz, nh, nb, bs, dh)
    vb = jnp.pad(heads(v), pad).reshape(bsz, nh, nb, bs, dh)
    q_blk = jnp.arange(seq) // bs
    gate = jnp.einsum('bhld,bhnd->bhln', qh, jnp.mean(kb, axis=3))
    gate = jnp.where(jnp.arange(nb)[None, :] < q_blk[:, None], gate, -jnp.inf)
    _, sel = lax.top_k(gate, k_sel_n)
    q_ch = qh.reshape(bsz, nh, nqc, qc, dh).transpose(2, 0, 1, 3, 4)
    sel_ch = sel.reshape(bsz, nh, nqc, qc, k_sel_n).transpose(2, 0, 1, 3, 4)
    bias_t = rel_bias.astype(f32).T
    b_idx = jnp.arange(bsz)[:, None, None, None]
    h_idx = jnp.arange(nh)[None, :, None, None]
    offs = jnp.arange(bs)

    def query_chunk(args):
        c, q_c, sel_c = args
        qpos = c * qc + jnp.arange(qc)
        own = (c * qc) // bs
        k_g = kb[b_idx, h_idx, sel_c]
        v_g = vb[b_idx, h_idx, sel_c]
        kpos = sel_c[..., None] * bs + offs
        s_sel = (jnp.einsum('bhqd,bhqjtd->bhqjt', q_c, k_g)
                 + bias_t[h_idx[..., None], t5_bucket(qpos[:, None, None] - kpos)])
        ok = jnp.arange(k_sel_n) < own
        s_sel = jnp.where(ok[:, None], s_sel, -jnp.inf)
        k_o = lax.dynamic_index_in_dim(kb, own, axis=2, keepdims=False)
        v_o = lax.dynamic_index_in_dim(vb, own, axis=2, keepdims=False)
        rel = qpos[:, None] - (own * bs + offs)[None, :]
        s_own = jnp.einsum('bhqd,bhtd->bhqt', q_c, k_o) + bias_t[:, t5_bucket(rel)]
        s_own = jnp.where(rel >= 0, s_own, -jnp.inf)
        logits = jnp.concatenate([s_sel.reshape(bsz, nh, qc, k_sel_n * bs), s_own], axis=-1)
        probs = jax.nn.softmax(logits, axis=-1)
        p_sel = probs[..., :k_sel_n * bs].reshape(bsz, nh, qc, k_sel_n, bs)
        p_own = probs[..., k_sel_n * bs:]
        return (jnp.einsum('bhqjt,bhqjtd->bhqd', p_sel, v_g)
                + jnp.einsum('bhqt,bhtd->bhqd', p_own, v_o))

    out = lax.map(query_chunk, (jnp.arange(nqc), q_ch, sel_ch))
    return out.transpose(1, 0, 3, 2, 4).reshape(bsz, seq, GW)


def mixing_block(h, w_in, mix_out_gain, w_out, rel_bias,
                 s5_lam_re, s5_lam_im, s5_log_step, s5_b_re, s5_b_im, s5_c_re, s5_c_im,
                 s5_d, s5_w_glu, s5_b_glu, ml_conv, ml_i_bias, ml_f_bias,
                 rw_mu, rw_w0, rw_w2, rw_a0, rw_a2, rw_g2, rw_k_k, rw_k_a, rw_r_k, rw_ln_w, rw_ln_b):
    bsz, seq, _ = h.shape
    proj = h @ w_in
    u_s5, p_ml, p_rw, p_mb = split_cols(proj, (GW, ML_COLS, RW_COLS, MB_COLS))
    y_s5 = s5_mixer(u_s5, s5_lam_re, s5_lam_im, s5_log_step, s5_b_re, s5_b_im,
                    s5_c_re, s5_c_im, s5_d, s5_w_glu, s5_b_glu)
    y_ml = mlstm_mixer(p_ml, ml_conv, ml_i_bias, ml_f_bias)
    y_rw = rwkv7_mixer(p_rw, rw_mu, rw_w0, rw_w2, rw_a0, rw_a2, rw_g2,
                       rw_k_k, rw_k_a, rw_r_k, rw_ln_w, rw_ln_b)
    y_mb = moba_mixer(p_mb, rel_bias)
    y = jnp.concatenate([y_s5, y_ml, y_rw, y_mb], axis=-1).reshape(bsz, seq, N_GROUPS, GW)
    y = rms_norm(y, mix_out_gain.reshape(N_GROUPS, GW)).reshape(bsz, seq, D_MIX)
    return y @ w_out


def cross_attention(h, mem_n, wq, wkv, wo):
    f32 = jnp.float32
    bsz, seq, _ = h.shape
    m = mem_n.shape[1]
    q = (h @ wq).reshape(bsz, seq, XA_HEADS, XA_HEAD_DIM)
    kv = (mem_n @ wkv).reshape(bsz, m, 2, XA_HEADS, XA_HEAD_DIM)
    s = jnp.einsum('blhd,bmhd->bhlm', q, kv[:, :, 0]).astype(f32) * XA_HEAD_DIM ** -0.5
    probs = jax.nn.softmax(s, axis=-1)
    o = jnp.einsum('bhlm,bmhd->blhd', probs, kv[:, :, 1].astype(f32)).reshape(bsz, seq, D_MODEL)
    return o @ wo


def swiglu(h, w_in, w_out):
    gu = h @ w_in
    return (jax.nn.silu(gu[..., :D_FF]) * gu[..., D_FF:]) @ w_out


def setup_inputs(seed: int = 0) -> dict:
    key = jax.random.key(seed)
    ks = iter(jax.random.split(key, 64))
    f32 = jnp.float32
    D, G, P = D_MODEL, S5_GROUPS, S5_STATE

    def normal(shape, scale):
        return jax.random.normal(next(ks), shape, f32) * scale

    def gain(shape):
        return 1.0 + 0.02 * jax.random.normal(next(ks), shape, f32)

    n_idx = jnp.arange(P, dtype=f32)
    return {
        'x': normal((BATCH, SEQ, D), 1.0),
        'mem': normal((BATCH, N_MEM, D), 1.0),
        'rel_bias': normal((REL_BUCKETS, MB_HEADS), 0.5),
        'norm_pre_mix': gain((DEPTH, D)),
        'norm_post_mix': gain((DEPTH, D)),
        'norm_pre_xa': gain((DEPTH, D)),
        'norm_post_xa': gain((DEPTH, D)),
        'norm_pre_ffn': gain((DEPTH, D)),
        'norm_post_ffn': gain((DEPTH, D)),
        'norm_mem': gain((DEPTH, D)),
        'w_in': normal((DEPTH, D, D_IN), D ** -0.5),
        'mix_out_gain': gain((DEPTH, D_MIX)),
        'w_out': normal((DEPTH, D_MIX, D), D_MIX ** -0.5),
        's5_lam_re': -0.5 + normal((DEPTH, G, P), 0.01),
        's5_lam_im': math.pi * n_idx + normal((DEPTH, G, P), 0.01),
        's5_log_step': jax.random.uniform(next(ks), (DEPTH, G), f32, math.log(1e-3), math.log(1e-1)),
        's5_b_re': normal((DEPTH, G, P, S5_GROUP), (2 * S5_GROUP) ** -0.5),
        's5_b_im': normal((DEPTH, G, P, S5_GROUP), (2 * S5_GROUP) ** -0.5),
        's5_c_re': normal((DEPTH, G, S5_GROUP, P), (2 * P) ** -0.5),
        's5_c_im': normal((DEPTH, G, S5_GROUP, P), (2 * P) ** -0.5),
        's5_d': normal((DEPTH, GW), 0.5),
        's5_w_glu': normal((DEPTH, GW, GW), GW ** -0.5),
        's5_b_glu': normal((DEPTH, GW), 0.01),
        'ml_conv': normal((DEPTH, CONV_K, 2 * GW), CONV_K ** -0.5),
        'ml_i_bias': normal((DEPTH, ML_HEADS), 0.1),
        'ml_f_bias': jnp.linspace(3.0, 6.0, ML_HEADS, dtype=f32) + normal((DEPTH, ML_HEADS), 0.1),
        'rw_mu': jax.random.uniform(next(ks), (DEPTH, RW_COLS), f32),
        'rw_w0': jnp.linspace(-6.0, -1.0, GW, dtype=f32) + normal((DEPTH, GW), 0.1),
        'rw_w2': normal((DEPTH, RW_DECAY_RANK, GW), 0.5 * RW_DECAY_RANK ** -0.5),
        'rw_a0': normal((DEPTH, GW), 0.1),
        'rw_a2': normal((DEPTH, RW_A_RANK, GW), RW_A_RANK ** -0.5),
        'rw_g2': normal((DEPTH, RW_GATE_RANK, GW), RW_GATE_RANK ** -0.5),
        'rw_k_k': 0.85 + normal((DEPTH, GW), 0.02),
        'rw_k_a': 1.0 + normal((DEPTH, GW), 0.02),
        'rw_r_k': normal((DEPTH, GW), 0.1),
        'rw_ln_w': gain((DEPTH, GW)),
        'rw_ln_b': normal((DEPTH, GW), 0.01),
        'xa_wq': normal((DEPTH, D, D), D ** -0.5),
        'xa_wkv': normal((DEPTH, D, 2 * D), D ** -0.5),
        'xa_wo': normal((DEPTH, D, D), D ** -0.5),
        'ffn_w_in': normal((DEPTH, D, 2 * D_FF), D ** -0.5),
        'ffn_w_out': normal((DEPTH, D_FF, D), D_FF ** -0.5),
    }


def reference(x, mem, rel_bias, norm_pre_mix, norm_post_mix, norm_pre_xa, norm_post_xa,
              norm_pre_ffn, norm_post_ffn, norm_mem, w_in, mix_out_gain, w_out,
              s5_lam_re, s5_lam_im, s5_log_step, s5_b_re, s5_b_im, s5_c_re, s5_c_im,
              s5_d, s5_w_glu, s5_b_glu, ml_conv, ml_i_bias, ml_f_bias,
              rw_mu, rw_w0, rw_w2, rw_a0, rw_a2, rw_g2, rw_k_k, rw_k_a, rw_r_k, rw_ln_w, rw_ln_b,
              xa_wq, xa_wkv, xa_wo, ffn_w_in, ffn_w_out):
    for l in range(DEPTH):
        h = rms_norm(x, norm_pre_mix[l])
        y = mixing_block(h, w_in[l], mix_out_gain[l], w_out[l], rel_bias,
                         s5_lam_re[l], s5_lam_im[l], s5_log_step[l], s5_b_re[l], s5_b_im[l],
                         s5_c_re[l], s5_c_im[l], s5_d[l], s5_w_glu[l], s5_b_glu[l],
                         ml_conv[l], ml_i_bias[l], ml_f_bias[l],
                         rw_mu[l], rw_w0[l], rw_w2[l], rw_a0[l], rw_a2[l], rw_g2[l],
                         rw_k_k[l], rw_k_a[l], rw_r_k[l], rw_ln_w[l], rw_ln_b[l])
        x = x + rms_norm(y, norm_post_mix[l]).astype(x.dtype)
        h = rms_norm(x, norm_pre_xa[l])
        y = cross_attention(h, rms_norm(mem, norm_mem[l]), xa_wq[l], xa_wkv[l], xa_wo[l])
        x = x + rms_norm(y, norm_post_xa[l]).astype(x.dtype)
        h = rms_norm(x, norm_pre_ffn[l])
        y = swiglu(h, ffn_w_in[l], ffn_w_out[l])
        x = x + rms_norm(y, norm_post_ffn[l]).astype(x.dtype)
    return x
```

```python
import functools
import math

import jax
import jax.numpy as jnp
from jax import lax
from jax.experimental import pallas as pl
from jax.experimental.pallas import tpu as pltpu

F32 = jnp.float32
BF16 = jnp.bfloat16
HIGHEST = lax.Precision.HIGHEST

D_MODEL = 1024
N_GROUPS = 4
GW = 256
HEAD_DIM = 64
N_HEADS = 4
S5_GROUP = 16
S5_GROUPS = 16
S5_STATE = 64
S5_P = S5_GROUPS * S5_STATE
CONV_K = 4
ML_CHUNK = 128
RW_GN_EPS = 64e-5
MOBA_BLOCK = 256
MOBA_TOPK = 3
REL_BUCKETS = 32
REL_MAX_DIST = 128
XA_HEADS = 4
XA_HEAD_DIM = 256
D_FF = 2816
RMS_EPS = 1e-6
PROJ_COLS = 3200
VMEM_LIMIT = 56 * 1024 * 1024

NT_DIMS = (((1,), (1,)), ((), ()))
TN_DIMS = (((0,), (0,)), ((), ()))


def _rms(x, g):
    return x * lax.rsqrt(jnp.mean(x * x, axis=-1, keepdims=True) + RMS_EPS) * g


def _bdot(a, b):
    return jnp.dot(a.astype(BF16), b.astype(BF16), preferred_element_type=F32)


def _hdot(a, b):
    return jnp.dot(a, b, precision=HIGHEST, preferred_element_type=F32)


def _sigmoid(x):
    return 1.0 / (1.0 + jnp.exp(-x))


def _softplus(x):
    return jnp.maximum(x, 0.0) + jnp.log1p(jnp.exp(-jnp.abs(x)))


def _params(*sem):
    return pltpu.CompilerParams(dimension_semantics=sem, vmem_limit_bytes=VMEM_LIMIT)


def _row_spec(tm, cols):
    return pl.BlockSpec((tm, cols), lambda i: (i, 0))


def _const_spec(shape):
    return pl.BlockSpec(shape, lambda *_: (0,) * len(shape))


def _proj_kernel(x_ref, g_ref, w_ref, s5_ref, ml_ref, rw_ref, mb_ref, gt_ref):
    h = _rms(x_ref[...], g_ref[...]).astype(BF16)

    def mm(lo, hi):
        return jnp.dot(h, w_ref[:, lo:hi], preferred_element_type=F32)

    s5_ref[...] = mm(0, 256)
    ml_ref[...] = mm(256, 1280)
    rw_ref[...] = mm(1280, 2304)
    mb_ref[...] = mm(2304, 3072)
    gt_ref[...] = mm(3072, 3200)


def _proj(x2d, g, w):
    t = x2d.shape[0]
    tm = min(512, t)
    widths = (256, 1024, 1024, 768, 128)
    return pl.pallas_call(
        _proj_kernel,
        grid=(t // tm,),
        in_specs=[_row_spec(tm, D_MODEL), _const_spec((1, D_MODEL)), _const_spec((D_MODEL, PROJ_COLS))],
        out_specs=[_row_spec(tm, c) for c in widths],
        out_shape=[jax.ShapeDtypeStruct((t, c), F32) for c in widths],
        compiler_params=_params("parallel"),
        name="proj",
    )(x2d, g, w)


def _s5_param_kernel(lr_ref, li_ref, ls_ref, br_ref, bi_ref, abr_ref, abi_ref, bbr_ref, bbi_ref):
    lr, li = lr_ref[...], li_ref[...]
    step = jnp.exp(ls_ref[...])
    mag = jnp.exp(lr * step)
    ang = li * step
    ab_re, ab_im = mag * jnp.cos(ang), mag * jnp.sin(ang)
    den = lr * lr + li * li
    co_re = ((ab_re - 1.0) * lr + ab_im * li) / den
    co_im = (ab_im * lr - (ab_re - 1.0) * li) / den
    abr_ref[...] = ab_re
    abi_ref[...] = ab_im
    br, bi = br_ref[...], bi_ref[...]
    bbr_ref[...] = co_re[:, None, :] * br - co_im[:, None, :] * bi
    bbi_ref[...] = co_re[:, None, :] * bi + co_im[:, None, :] * br


def _s5_params(lam_re, lam_im, log_step, b_re, b_im):
    n = lam_re.shape[0]
    return pl.pallas_call(
        _s5_param_kernel,
        out_shape=[jax.ShapeDtypeStruct((n, S5_STATE), F32)] * 2
        + [jax.ShapeDtypeStruct((n, S5_GROUP, S5_STATE), F32)] * 2,
        name="s5_params",
    )(lam_re, lam_im, log_step, b_re, b_im)


def _s5_kernel(u_ref, wb_ref, are_ref, aim_ref, wc_ref, d_ref, wg_ref, bg_ref, o_ref, xs_ref, st_ref,
               *, tm, nb):
    @pl.when(pl.program_id(0) == 0)
    def _():
        st_ref[...] = jnp.zeros_like(st_ref)

    u = u_ref[...]
    xs_ref[...] = _bdot(u, wb_ref[...])
    ch = 256
    for c in range(S5_P // ch):
        re_cols = slice(c * ch, (c + 1) * ch)
        im_cols = slice(S5_P + c * ch, S5_P + (c + 1) * ch)
        a_re = jnp.broadcast_to(are_ref[:, re_cols], (nb, ch))
        a_im = jnp.broadcast_to(aim_ref[:, re_cols], (nb, ch))

        def body(t, carry, re_cols=re_cols, im_cols=im_cols, a_re=a_re, a_im=a_im):
            s_re, s_im = carry
            rows = pl.ds(pl.multiple_of(t * nb, nb), nb)
            n_re = a_re * s_re - a_im * s_im + xs_ref[rows, re_cols]
            n_im = a_re * s_im + a_im * s_re + xs_ref[rows, im_cols]
            xs_ref[rows, re_cols] = n_re
            xs_ref[rows, im_cols] = n_im
            return n_re, n_im

        s_re, s_im = lax.fori_loop(0, tm, body, (st_ref[:, re_cols], st_ref[:, im_cols]))
        st_ref[:, re_cols] = s_re
        st_ref[:, im_cols] = s_im

    y = _bdot(xs_ref[...], wc_ref[...]) + d_ref[...] * u
    z = jax.nn.gelu(y)
    o_ref[...] = z * _sigmoid(_bdot(z, wg_ref[...]) + bg_ref[...])


def _s5(u_tb, nb, wb, a_re, a_im, wc, d, wg, bg):
    rows = u_tb.shape[0]
    seq = rows // nb
    tm = min(64, seq)
    blk = tm * nb
    return pl.pallas_call(
        functools.partial(_s5_kernel, tm=tm, nb=nb),
        grid=(seq // tm,),
        in_specs=[_row_spec(blk, GW), _const_spec((GW, 2 * S5_P)), _const_spec((1, S5_P)),
                  _const_spec((1, S5_P)), _const_spec((2 * S5_P, GW)), _const_spec((1, GW)),
                  _const_spec((GW, GW)), _const_spec((1, GW))],
        out_specs=_row_spec(blk, GW),
        out_shape=jax.ShapeDtypeStruct((rows, GW), F32),
        scratch_shapes=[pltpu.VMEM((blk, 2 * S5_P), F32), pltpu.VMEM((nb, 2 * S5_P), F32)],
        compiler_params=_params("arbitrary"),
        name="s5",
    )(u_tb, wb, a_re, a_im, wc, d, wg, bg)


def _mlstm_kernel(p_ref, g_ref, cw_ref, gb_ref, o_ref, pad_ref, *, seq):
    cl = ML_CHUNK
    dh = HEAD_DIM
    pad_ref[0:8, :] = jnp.zeros((8, 2 * GW), F32)
    pad_ref[8:8 + seq, :] = p_ref[0, :, 0:2 * GW]
    row = lax.broadcasted_iota(jnp.int32, (cl, cl), 0)
    col = lax.broadcasted_iota(jnp.int32, (cl, cl), 1)
    tri = col <= row
    trif = jnp.where(tri, 1.0, 0.0).astype(F32)
    cw = cw_ref[...]

    def chunk(c, carry):
        cs, ns, ms = carry
        r0 = pl.multiple_of(c * cl, cl)
        win = pad_ref[pl.ds(r0, cl + 8), :]
        conv = cw[0:1, :] * win[5:5 + cl, :]
        for j in range(1, CONV_K):
            conv = conv + cw[j:j + 1, :] * win[5 + j:5 + j + cl, :]
        qk = conv * _sigmoid(conv)
        gc = g_ref[0, pl.ds(r0, cl), :] + gb_ref[...]
        logf = -_softplus(-gc)
        bcum = _hdot(trif, logf)
        gct = gc.T
        bct = bcum.T
        outs, cs_n, ns_n, ms_n = [], [], [], []
        for h in range(N_HEADS):
            q = qk[:, h * dh:(h + 1) * dh]
            k = qk[:, GW + h * dh:GW + (h + 1) * dh] * (dh ** -0.5)
            v = p_ref[0, pl.ds(r0, cl), 2 * GW + h * dh:2 * GW + (h + 1) * dh]
            o = p_ref[0, pl.ds(r0, cl), 3 * GW + h * dh:3 * GW + (h + 1) * dh]
            bc = bcum[:, N_HEADS + h:N_HEADS + h + 1]
            li = gc[:, h:h + 1]
            br = bct[N_HEADS + h:N_HEADS + h + 1, :]
            lir = gct[h:h + 1, :]
            m_prev = ms[h]
            dmat = jnp.where(tri, bc - br + lir, -jnp.inf)
            m_intra = jnp.max(dmat, axis=1, keepdims=True)
            b_inter = bc + m_prev
            m_comb = jnp.maximum(b_inter, m_intra)
            w_inter = jnp.exp(b_inter - m_comb)
            qb, kb, vb = q.astype(BF16), k.astype(BF16), v.astype(BF16)
            s = jnp.exp(dmat - m_comb) * lax.dot_general(qb, kb, NT_DIMS, preferred_element_type=F32)
            num = _bdot(s, vb) + w_inter * _bdot(qb, cs[h])
            den = jnp.sum(s, axis=1, keepdims=True) + w_inter * jnp.sum(q * ns[h], axis=1, keepdims=True)
            hh = num / jnp.maximum(jnp.abs(den), jnp.exp(-m_comb))
            outs.append(_sigmoid(o) * hh)
            b_last = bc[cl - 1:cl, :]
            w_st = b_last - bc + li
            m_in = jnp.max(w_st, axis=0, keepdims=True)
            ke = k * jnp.exp(w_st - m_in)
            kv = _bdot(ke.T, vb)
            ks = jnp.sum(ke, axis=0, keepdims=True)
            m_new = jnp.maximum(b_last + m_prev, m_in)
            a = jnp.exp(b_last + m_prev - m_new)
            e = jnp.exp(m_in - m_new)
            cs_n.append(a * cs[h] + e * kv)
            ns_n.append(a * ns[h] + e * ks)
            ms_n.append(m_new)
        o_ref[0, pl.ds(r0, cl), :] = jnp.concatenate(outs, axis=1)
        return tuple(cs_n), tuple(ns_n), tuple(ms_n)

    init = (tuple(jnp.zeros((dh, dh), F32) for _ in range(N_HEADS)),
            tuple(jnp.zeros((1, dh), F32) for _ in range(N_HEADS)),
            tuple(jnp.zeros((1, 1), F32) for _ in range(N_HEADS)))
    lax.fori_loop(0, seq // cl, chunk, init)


def _mlstm(p_ml, gates, conv_w, gate_bias):
    bsz, seq, _ = p_ml.shape
    return pl.pallas_call(
        functools.partial(_mlstm_kernel, seq=seq),
        grid=(bsz,),
        in_specs=[pl.BlockSpec((1, seq, 4 * GW), lambda b: (b, 0, 0)),
                  pl.BlockSpec((1, seq, 128), lambda b: (b, 0, 0)),
                  _const_spec((CONV_K, 2 * GW)), _const_spec((1, 128))],
        out_specs=pl.BlockSpec((1, seq, GW), lambda b: (b, 0, 0)),
        out_shape=jax.ShapeDtypeStruct((bsz, seq, GW), F32),
        scratch_shapes=[pltpu.VMEM((seq + 8, 2 * GW), F32)],
        compiler_params=_params("parallel"),
        name="mlstm",
    )(p_ml, gates, conv_w, gate_bias)


def _rwkv_pre_kernel(p_ref, pv_ref, mu_ref, w0_ref, w2_ref, a0_ref, a2_ref, g2_ref, kk_ref, ka_ref, rk_ref,
                     bd_ref, sela_ref, selb_ref,
                     kap_o, alp_o, w_o, wr_o, km_o, v_o, c_o, bonus_o, g_o):
    x = p_ref[0]
    last = jnp.where(pl.program_id(1) > 0, pv_ref[0][7:8, :], 0.0)
    rowi = lax.broadcasted_iota(jnp.int32, x.shape, 0)
    prev = jnp.where(rowi == 0, last, pltpu.roll(x, 1, 0))
    p = x + (prev - x) * mu_ref[...]
    r, k, v = p[:, 0:GW], p[:, GW:2 * GW], p[:, 2 * GW:3 * GW]
    lo = p[:, 3 * GW:3 * GW + 128]
    g_lo = p[:, 3 * GW + 128:4 * GW]
    w = -_softplus(-(w0_ref[...] + _hdot(jnp.tanh(lo), w2_ref[...]))) - 0.5
    decay = jnp.exp(-jnp.exp(w))
    a = _sigmoid(a0_ref[...] + _hdot(lo, a2_ref[...]))
    g_o[0] = _hdot(_sigmoid(g_lo), g2_ref[...])
    bd = bd_ref[...]
    kk = k * kk_ref[...]
    kk = kk * lax.rsqrt(_hdot(kk * kk, bd) + 1e-12)
    km = k * (1.0 + (a - 1.0) * ka_ref[...])
    alp = kk * a
    kap_o[0] = kk
    alp_o[0] = alp
    w_o[0] = decay
    wr_o[0] = decay * r
    km_o[0] = km
    v_o[0] = v
    c_o[0] = _hdot(alp * r, sela_ref[...]) + _hdot(km * r, selb_ref[...])
    bonus_o[0] = _hdot(r * km * rk_ref[...], bd) * v


def _rwkv_pre(p_rw, mu, w0, w2p, a0, a2p, g2, k_k, k_a, r_k, bd, sela, selb):
    bsz, seq, _ = p_rw.shape
    tr = min(256, seq)
    row = lambda c: pl.BlockSpec((1, tr, c), lambda b, i: (b, i, 0))
    prev = pl.BlockSpec((1, 8, 4 * GW), lambda b, i: (b, jnp.maximum(i * (tr // 8) - 1, 0), 0))
    vec = lambda c: _const_spec((1, c))
    widths = (GW, GW, GW, GW, GW, GW, 128, GW, GW)
    return pl.pallas_call(
        _rwkv_pre_kernel,
        grid=(bsz, seq // tr),
        in_specs=[row(4 * GW), prev, vec(4 * GW), vec(GW), _const_spec((128, GW)), vec(GW),
                  _const_spec((128, GW)), _const_spec((128, GW)), vec(GW), vec(GW), vec(GW),
                  _const_spec((GW, GW)), _const_spec((GW, 128)), _const_spec((GW, 128))],
        out_specs=[row(c) for c in widths],
        out_shape=[jax.ShapeDtypeStruct((bsz, seq, c), F32) for c in widths],
        compiler_params=_params("parallel", "parallel"),
        name="rwkv_pre",
    )(p_rw, p_rw, mu, w0, w2p, a0, a2p, g2, k_k, k_a, r_k, bd, sela, selb)


def _rwkv_seq_kernel(kap_ref, wr_ref, w_ref, alp_ref, km_ref, v_ref, c_ref, y_ref, s_ref, *, tb):
    @pl.when(pl.program_id(0) == 0)
    def _():
        s_ref[...] = jnp.zeros_like(s_ref)

    def step(t, carry):
        s = s_ref[...]
        sa = -jnp.sum(s * kap_ref[t][None], axis=1, keepdims=True)
        y0 = jnp.sum(s * wr_ref[t][None], axis=1, keepdims=True)
        vv = v_ref[t]
        y_ref[t] = y0 + sa * c_ref[t, 0:1] + vv * c_ref[t, 1:2]
        s_ref[...] = s * w_ref[t][None] + sa * alp_ref[t][None] + vv * km_ref[t][None]
        return carry

    lax.fori_loop(0, tb, step, 0)


def _rwkv_seq(kap, wr, w, alp, km, v4, c4):
    seq, dk, nl = kap.shape
    tb = min(32, seq)
    kspec = pl.BlockSpec((tb, dk, nl), lambda i: (i, 0, 0))
    half = HEAD_DIM // 2
    return pl.pallas_call(
        functools.partial(_rwkv_seq_kernel, tb=tb),
        grid=(seq // tb,),
        in_specs=[kspec] * 5 + [pl.BlockSpec((tb, half, 1, nl), lambda i: (i, 0, 0, 0)),
                                pl.BlockSpec((tb, 2, 1, nl), lambda i: (i, 0, 0, 0))],
        out_specs=pl.BlockSpec((tb, half, 1, nl), lambda i: (i, 0, 0, 0)),
        out_shape=jax.ShapeDtypeStruct((seq, half, 1, nl), F32),
        scratch_shapes=[pltpu.VMEM((half, dk, nl), F32)],
        compiler_params=_params("arbitrary"),
        name="rwkv_seq",
    )(kap, wr, w, alp, km, v4, c4)


def _to_time_major(x, bsz, seq):
    t = x.reshape(bsz, seq, N_HEADS, HEAD_DIM).transpose(1, 3, 0, 2).reshape(seq, HEAD_DIM, bsz * N_HEADS)
    return jnp.concatenate([t, t], axis=-1)


def _bias_kernel(rb_ref, bk_ref, o_ref):
    h = pl.program_id(0)
    bk = bk_ref[0]
    out = jnp.full(bk.shape, -jnp.inf, F32)
    for b in range(REL_BUCKETS):
        out = jnp.where(bk == b, rb_ref[b, h], out)
    o_ref[0, 0] = out


def _bias_tiles(rel_bias, buckets):
    bs = MOBA_BLOCK
    return pl.pallas_call(
        _bias_kernel,
        grid=(N_HEADS, 2),
        in_specs=[pl.BlockSpec(memory_space=pltpu.SMEM), pl.BlockSpec((1, bs, bs), lambda h, k: (k, 0, 0))],
        out_specs=pl.BlockSpec((1, 1, bs, bs), lambda h, k: (h, k, 0, 0)),
        out_shape=jax.ShapeDtypeStruct((N_HEADS, 2, bs, bs), F32),
        name="moba_bias",
    )(rel_bias, buckets)


def _moba_kernel(far_ref, k_ref, qt_ref, vt_ref, bias_ref, o_ref, *, seq, nsel):
    bs = MOBA_BLOCK
    nb = seq // bs
    h = pl.program_id(1)
    k = k_ref[...]
    qt = qt_ref[...] * (HEAD_DIM ** -0.5)
    kmean = jnp.mean(k.reshape(nb, bs, HEAD_DIM), axis=1)
    gate = _hdot(kmean, qt)
    jj = lax.broadcasted_iota(jnp.int32, (nb, seq), 0)
    qblk = lax.broadcasted_iota(jnp.int32, (nb, seq), 1) // bs
    g = jnp.where(jj < qblk, gate, -jnp.inf)
    selb = jnp.full((nb, seq), -jnp.inf, F32)
    for r in range(nsel):
        m = jnp.max(g, axis=0, keepdims=True)
        idx = jnp.min(jnp.where(g == m, jj, nb), axis=0, keepdims=True)
        hit = jj == idx
        selb = jnp.where(jnp.logical_and(hit, qblk > r), 0.0, selb)
        g = jnp.where(hit, -jnp.inf, g)
    kb = k.astype(BF16)
    qtb = qt.astype(BF16)
    vtb = vt_ref[...].astype(BF16)
    b_own = bias_ref[0, 0]
    b_prev = bias_ref[0, 1]
    b_far = far_ref[h]
    for i in range(nb):
        qs = slice(i * bs, (i + 1) * bs)
        q_i = qtb[:, qs]
        s = jnp.dot(kb[qs, :], q_i, preferred_element_type=F32) + b_own
        m_run = jnp.max(s, axis=0, keepdims=True)
        p = jnp.exp(s - m_run)
        l_run = jnp.sum(p, axis=0, keepdims=True)
        acc = jnp.dot(vtb[:, qs], p.astype(BF16), preferred_element_type=F32)
        for j in range(i - 1, -1, -1):
            ks = slice(j * bs, (j + 1) * bs)
            s = jnp.dot(kb[ks, :], q_i, preferred_element_type=F32) + selb[j:j + 1, qs]
            s = s + (b_prev if j == i - 1 else b_far)
            m_new = jnp.maximum(m_run, jnp.max(s, axis=0, keepdims=True))
            alpha = jnp.exp(m_run - m_new)
            p = jnp.exp(s - m_new)
            l_run = l_run * alpha + jnp.sum(p, axis=0, keepdims=True)
            acc = acc * alpha + jnp.dot(vtb[:, ks], p.astype(BF16), preferred_element_type=F32)
            m_run = m_new
        o_ref[:, qs] = acc / l_run


def _moba(k_h, q_t, v_t, bias_tiles, far):
    bsz, _, seq, _ = k_h.shape
    nsel = max(1, min(MOBA_TOPK, seq // MOBA_BLOCK - 1))
    tspec = pl.BlockSpec((None, HEAD_DIM, seq), lambda b, h: (b, h, 0))
    return pl.pallas_call(
        functools.partial(_moba_kernel, seq=seq, nsel=nsel),
        grid=(bsz, N_HEADS),
        in_specs=[pl.BlockSpec(memory_space=pltpu.SMEM),
                  pl.BlockSpec((None, None, seq, HEAD_DIM), lambda b, h: (b, h, 0, 0)),
                  tspec, tspec,
                  pl.BlockSpec((1, 2, MOBA_BLOCK, MOBA_BLOCK), lambda b, h: (h, 0, 0, 0))],
        out_specs=tspec,
        out_shape=jax.ShapeDtypeStruct((bsz, GW, seq), F32),
        compiler_params=_params("parallel", "parallel"),
        name="moba",
    )(far, k_h, q_t, v_t, bias_tiles)


def _t5_bucket(rel):
    n = jnp.maximum(rel, 0)
    max_exact = REL_BUCKETS // 2
    nf = jnp.maximum(n, 1).astype(F32)
    large = max_exact + (jnp.log(nf / max_exact) / math.log(REL_MAX_DIST / max_exact)
                         * (REL_BUCKETS - max_exact)).astype(jnp.int32)
    return jnp.where(n < max_exact, n, jnp.minimum(large, REL_BUCKETS - 1))


def _bucket_tiles():
    kpos = jnp.arange(MOBA_BLOCK)[:, None]
    qpos = jnp.arange(MOBA_BLOCK)[None, :]
    rel = qpos - kpos
    own = jnp.where(rel >= 0, _t5_bucket(rel), -1)
    prev = _t5_bucket(rel + MOBA_BLOCK)
    return jnp.stack([own, prev]).astype(jnp.int32)


def _mixout_kernel(ys5_ref, yml_ref, yrw_ref, bonus_ref, g_ref, ymb_ref, lnw_ref, lnb_ref, bd_ref,
                   gain_ref, w_ref, gpost_ref, x_ref, o_ref):
    bd = bd_ref[...]
    y = yrw_ref[...]
    mean = _hdot(y, bd) * (1.0 / HEAD_DIM)
    d = y - mean
    var = _hdot(d * d, bd) * (1.0 / HEAD_DIM)
    yrw = (d * lax.rsqrt(var + RW_GN_EPS) * lnw_ref[...] + lnb_ref[...] + bonus_ref[...]) * g_ref[...]
    acc = None
    for i, yg in enumerate((ys5_ref[...], yml_ref[...], yrw, ymb_ref[...])):
        cols = slice(i * GW, (i + 1) * GW)
        part = jnp.dot(_rms(yg, gain_ref[:, cols]).astype(BF16), w_ref[cols, :], preferred_element_type=F32)
        acc = part if acc is None else acc + part
    o_ref[...] = x_ref[...] + _rms(acc, gpost_ref[...])


def _mixout(ys5, yml, yrw, bonus, g, ymb, ln_w, ln_b, bd, gain, w, gpost, x2d):
    t = x2d.shape[0]
    tm = min(512, t)
    vec = lambda c: _const_spec((1, c))
    return pl.pallas_call(
        _mixout_kernel,
        grid=(t // tm,),
        in_specs=[_row_spec(tm, GW)] * 6 + [vec(GW), vec(GW), _const_spec((GW, GW)), vec(D_MODEL),
                                           _const_spec((D_MODEL, D_MODEL)), vec(D_MODEL),
                                           _row_spec(tm, D_MODEL)],
        out_specs=_row_spec(tm, D_MODEL),
        out_shape=jax.ShapeDtypeStruct((t, D_MODEL), F32),
        compiler_params=_params("parallel"),
        name="mixout",
    )(ys5, yml, yrw, bonus, g, ymb, ln_w, ln_b, bd, gain, w, gpost, x2d)


def _kv_kernel(m_ref, g_ref, w_ref, o_ref):
    h = _rms(m_ref[0], g_ref[...]).astype(BF16)
    o_ref[0] = jnp.dot(h, w_ref[...], preferred_element_type=F32).astype(BF16)


def _kv(mem, g, w):
    bsz, m, _ = mem.shape
    return pl.pallas_call(
        _kv_kernel,
        grid=(bsz,),
        in_specs=[pl.BlockSpec((1, m, D_MODEL), lambda b: (b, 0, 0)), _const_spec((1, D_MODEL)),
                  _const_spec((D_MODEL, 2 * D_MODEL))],
        out_specs=pl.BlockSpec((1, m, 2 * D_MODEL), lambda b: (b, 0, 0)),
        out_shape=jax.ShapeDtypeStruct((bsz, m, 2 * D_MODEL), BF16),
        compiler_params=_params("parallel"),
        name="xa_kv",
    )(mem, g, w)


def _xattn_kernel(x_ref, kv_ref, gpre_ref, wq_ref, wo_ref, gpost_ref, o_ref):
    x = x_ref[...]
    q = jnp.dot(_rms(x, gpre_ref[...]).astype(BF16), wq_ref[...], preferred_element_type=F32)
    acc = None
    for hd in range(XA_HEADS):
        cols = slice(hd * XA_HEAD_DIM, (hd + 1) * XA_HEAD_DIM)
        vcols = slice(D_MODEL + hd * XA_HEAD_DIM, D_MODEL + (hd + 1) * XA_HEAD_DIM)
        s = lax.dot_general(q[:, cols].astype(BF16), kv_ref[0, :, cols], NT_DIMS,
                            preferred_element_type=F32) * (XA_HEAD_DIM ** -0.5)
        p = jnp.exp(s - jnp.max(s, axis=-1, keepdims=True))
        o = jnp.dot(p.astype(BF16), kv_ref[0, :, vcols], preferred_element_type=F32)
        o = o / jnp.sum(p, axis=-1, keepdims=True)
        part = jnp.dot(o.astype(BF16), wo_ref[cols, :], preferred_element_type=F32)
        acc = part if acc is None else acc + part
    o_ref[...] = x + _rms(acc, gpost_ref[...])


def _xattn(x2d, kv, gpre, wq, wo, gpost, seq):
    t = x2d.shape[0]
    m = kv.shape[1]
    tm = min(512, seq)
    per = seq // tm
    vec = _const_spec((1, D_MODEL))
    sq = _const_spec((D_MODEL, D_MODEL))
    return pl.pallas_call(
        _xattn_kernel,
        grid=(t // tm,),
        in_specs=[_row_spec(tm, D_MODEL), pl.BlockSpec((1, m, 2 * D_MODEL), lambda i: (i // per, 0, 0)),
                  vec, sq, sq, vec],
        out_specs=_row_spec(tm, D_MODEL),
        out_shape=jax.ShapeDtypeStruct((t, D_MODEL), F32),
        compiler_params=_params("parallel"),
        name="xattn",
    )(x2d, kv, gpre, wq, wo, gpost)


def _ffn_kernel(x_ref, gpre_ref, wi_ref, wo_ref, gpost_ref, o_ref):
    x = x_ref[...]
    h = _rms(x, gpre_ref[...]).astype(BF16)
    half = D_FF // 2
    acc = None
    for c in range(2):
        gate = jnp.dot(h, wi_ref[:, c * half:(c + 1) * half], preferred_element_type=F32)
        up = jnp.dot(h, wi_ref[:, D_FF + c * half:D_FF + (c + 1) * half], preferred_element_type=F32)
        act = (gate * _sigmoid(gate) * up).astype(BF16)
        part = jnp.dot(act, wo_ref[c * half:(c + 1) * half, :], preferred_element_type=F32)
        acc = part if acc is None else acc + part
    o_ref[...] = x + _rms(acc, gpost_ref[...])


def _ffn(x2d, gpre, wi, wo, gpost):
    t = x2d.shape[0]
    tm = min(256, t)
    vec = _const_spec((1, D_MODEL))
    once = pl.Buffered(1)
    return pl.pallas_call(
        _ffn_kernel,
        grid=(t // tm,),
        in_specs=[_row_spec(tm, D_MODEL), vec,
                  pl.BlockSpec((D_MODEL, 2 * D_FF), lambda i: (0, 0), pipeline_mode=once),
                  pl.BlockSpec((D_FF, D_MODEL), lambda i: (0, 0), pipeline_mode=once), vec],
        out_specs=_row_spec(tm, D_MODEL),
        out_shape=jax.ShapeDtypeStruct((t, D_MODEL), F32),
        compiler_params=_params("parallel"),
        name="ffn",
    )(x2d, gpre, wi, wo, gpost)


def _block_diag(blocks):
    g, a, b = blocks.shape
    eye = jnp.eye(g, dtype=blocks.dtype)
    return jnp.einsum('gab,gk->gakb', blocks, eye).reshape(g * a, g * b)


def kernel(x, mem, rel_bias, norm_pre_mix, norm_post_mix, norm_pre_xa, norm_post_xa, norm_pre_ffn,
           norm_post_ffn, norm_mem, w_in, mix_out_gain, w_out, s5_lam_re, s5_lam_im, s5_log_step,
           s5_b_re, s5_b_im, s5_c_re, s5_c_im, s5_d, s5_w_glu, s5_b_glu, ml_conv, ml_i_bias, ml_f_bias,
           rw_mu, rw_w0, rw_w2, rw_a0, rw_a2, rw_g2, rw_k_k, rw_k_a, rw_r_k, rw_ln_w, rw_ln_b,
           xa_wq, xa_wkv, xa_wo, ffn_w_in, ffn_w_out):
    bsz, seq, _ = x.shape
    depth = w_in.shape[0]
    t = bsz * seq
    row = lambda a: a.reshape(1, -1)

    ml_end = GW + 4 * GW
    w_proj = jnp.concatenate(
        [w_in[:, :, :ml_end], w_in[:, :, ml_end + 2 * N_HEADS:], w_in[:, :, ml_end:ml_end + 2 * N_HEADS],
         jnp.zeros((depth, D_MODEL, PROJ_COLS - w_in.shape[2]), w_in.dtype)], axis=2).astype(BF16)
    w_out_b = w_out.astype(BF16)
    wq_b, wkv_b, wo_b = xa_wq.astype(BF16), xa_wkv.astype(BF16), xa_wo.astype(BF16)
    ffn_wi_b, ffn_wo_b = ffn_w_in.astype(BF16), ffn_w_out.astype(BF16)
    w_glu_b = s5_w_glu.astype(BF16)

    ng = depth * S5_GROUPS
    ab_re, ab_im, bb_re, bb_im = _s5_params(
        s5_lam_re.reshape(ng, S5_STATE), s5_lam_im.reshape(ng, S5_STATE), s5_log_step.reshape(ng, 1),
        s5_b_re.transpose(0, 1, 3, 2).reshape(ng, S5_GROUP, S5_STATE),
        s5_b_im.transpose(0, 1, 3, 2).reshape(ng, S5_GROUP, S5_STATE))
    ab_re = ab_re.reshape(depth, 1, S5_P)
    ab_im = ab_im.reshape(depth, 1, S5_P)
    bb_re = bb_re.reshape(depth, S5_GROUPS, S5_GROUP, S5_STATE)
    bb_im = bb_im.reshape(depth, S5_GROUPS, S5_GROUP, S5_STATE)

    bd = _block_diag(jnp.ones((N_HEADS, HEAD_DIM, HEAD_DIM), F32))
    head_of_lane = jnp.arange(GW) // HEAD_DIM
    sela = (head_of_lane[:, None] == jnp.arange(128)[None, :]).astype(F32)
    selb = (head_of_lane[:, None] + N_HEADS == jnp.arange(128)[None, :]).astype(F32)
    zeros64 = jnp.zeros((HEAD_DIM, GW), F32)

    bias_tiles = _bias_tiles(rel_bias, _bucket_tiles())
    far = rel_bias[REL_BUCKETS - 1, :]

    x2d = x.reshape(t, D_MODEL)
    for l in range(depth):
        u_s5, p_ml, p_rw, p_mb, gates = _proj(x2d, row(norm_pre_mix[l]), w_proj[l])

        wb = jnp.concatenate([_block_diag(bb_re[l]), _block_diag(bb_im[l])], axis=1).astype(BF16)
        wc = jnp.concatenate([_block_diag(s5_c_re[l].transpose(0, 2, 1)),
                              -_block_diag(s5_c_im[l].transpose(0, 2, 1))], axis=0).astype(BF16)
        u_tb = u_s5.reshape(bsz, seq, GW).transpose(1, 0, 2).reshape(t, GW)
        y_s5 = _s5(u_tb, bsz, wb, ab_re[l], ab_im[l], wc, row(s5_d[l]), w_glu_b[l], row(s5_b_glu[l]))
        y_s5 = y_s5.reshape(seq, bsz, GW).transpose(1, 0, 2).reshape(t, GW)

        gate_bias = jnp.concatenate([ml_i_bias[l], ml_f_bias[l], jnp.zeros((128 - 2 * N_HEADS,), F32)])
        y_ml = _mlstm(p_ml.reshape(bsz, seq, 4 * GW), gates.reshape(bsz, seq, 128), ml_conv[l],
                      row(gate_bias)).reshape(t, GW)

        w2p = jnp.concatenate([rw_w2[l], zeros64], axis=0)
        a2p = jnp.concatenate([zeros64, rw_a2[l]], axis=0)
        kap, alp, wdec, wr, km, v_rw, c12, bonus, g_rw = _rwkv_pre(
            p_rw.reshape(bsz, seq, 4 * GW), row(rw_mu[l]), row(rw_w0[l]), w2p, row(rw_a0[l]), a2p, rw_g2[l],
            row(rw_k_k[l]), row(rw_k_a[l]), row(rw_r_k[l]), bd, sela, selb)
        half = HEAD_DIM // 2
        nl = 2 * bsz * N_HEADS
        v4 = v_rw.reshape(bsz, seq, N_HEADS, 2, half).transpose(1, 4, 3, 0, 2).reshape(seq, half, 1, nl)
        c4 = c12[:, :, :2 * N_HEADS].reshape(bsz, seq, 2, N_HEADS).transpose(1, 2, 0, 3)
        c4 = c4.reshape(seq, 2, 1, bsz * N_HEADS)
        c4 = jnp.concatenate([c4, c4], axis=-1)
        y4 = _rwkv_seq(*(_to_time_major(a, bsz, seq) for a in (kap, wr, wdec, alp, km)), v4, c4)
        y_rw = y4.reshape(seq, half, 2, bsz, N_HEADS).transpose(3, 0, 4, 2, 1).reshape(t, GW)

        p4 = p_mb.reshape(bsz, seq, 3, N_HEADS, HEAD_DIM)
        k_h = p4[:, :, 1].transpose(0, 2, 1, 3)
        q_t = p4[:, :, 0].transpose(0, 2, 3, 1).reshape(bsz, GW, seq)
        v_t = p4[:, :, 2].transpose(0, 2, 3, 1).reshape(bsz, GW, seq)
        y_mb = _moba(k_h, q_t, v_t, bias_tiles, far).transpose(0, 2, 1).reshape(t, GW)

        x2d = _mixout(y_s5, y_ml, y_rw, bonus.reshape(t, GW), g_rw.reshape(t, GW), y_mb,
                      row(rw_ln_w[l]), row(rw_ln_b[l]), bd, row(mix_out_gain[l]), w_out_b[l],
                      row(norm_post_mix[l]), x2d)

        kv = _kv(mem, row(norm_mem[l]), wkv_b[l])
        x2d = _xattn(x2d, kv, row(norm_pre_xa[l]), wq_b[l], wo_b[l], row(norm_post_xa[l]), seq)
        x2d = _ffn(x2d, row(norm_pre_ffn[l]), ffn_wi_b[l], ffn_wo_b[l], row(norm_post_ffn[l]))
    return x2d.reshape(bsz, seq, D_MODEL)
```

```python
import functools
import math

import jax
import jax.numpy as jnp
from jax import lax
from jax.experimental import pallas as pl
from jax.experimental.pallas import tpu as pltpu

F32 = jnp.float32
BF16 = jnp.bfloat16
HIGHEST = lax.Precision.HIGHEST

D_MODEL = 1024
N_GROUPS = 4
GW = 256
HEAD_DIM = 64
N_HEADS = 4
S5_GROUP = 16
S5_GROUPS = 16
S5_STATE = 64
S5_P = S5_GROUPS * S5_STATE
CONV_K = 4
ML_CHUNK = 128
RW_GN_EPS = 64e-5
RW_VECS = 5
RW_ROW_UNROLL = 32
MOBA_BLOCK = 256
MOBA_TOPK = 3
REL_BUCKETS = 32
REL_MAX_DIST = 128
XA_HEADS = 4
XA_HEAD_DIM = 256
D_FF = 2816
RMS_EPS = 1e-6
LANES = 128
PROJ_COLS = 10 * GW + LANES
VMEM_LIMIT = 56 * 1024 * 1024

NT_DIMS = (((1,), (1,)), ((), ()))


def _rms(x, g):
    return x * lax.rsqrt(jnp.mean(x * x, axis=-1, keepdims=True) + RMS_EPS) * g


def _bdot(a, b):
    return jnp.dot(a.astype(BF16), b.astype(BF16), preferred_element_type=F32)


def _hdot(a, b):
    return jnp.dot(a, b, precision=HIGHEST, preferred_element_type=F32)


def _sum3(x, ones_b):
    hi = x.astype(BF16)
    r1 = x - hi.astype(F32)
    mid = r1.astype(BF16)
    lo = (r1 - mid.astype(F32)).astype(BF16)
    dot = lambda a: jnp.dot(a, ones_b, preferred_element_type=F32)
    return dot(hi) + dot(mid) + dot(lo)


def _sigmoid(x):
    return 1.0 / (1.0 + jnp.exp(-x))


def _softplus(x):
    return jnp.maximum(x, 0.0) + jnp.log1p(jnp.exp(-jnp.abs(x)))


def _params(*sem):
    return pltpu.CompilerParams(dimension_semantics=sem, vmem_limit_bytes=VMEM_LIMIT)


def _row_spec(tm, cols):
    return pl.BlockSpec((tm, cols), lambda i: (i, 0))


def _const_spec(shape):
    return pl.BlockSpec(shape, lambda *_: (0,) * len(shape))


def _proj_kernel(x_ref, g_ref, w_ref, wt_ref, s5_ref, ml_ref, rw_ref, gt_ref, k4_ref, qvt_ref):
    h = _rms(x_ref[...], g_ref[...]).astype(BF16)

    def mm(lo, hi):
        return jnp.dot(h, w_ref[:, lo:hi], preferred_element_type=F32)

    s5_ref[...] = mm(0, GW)
    ml_ref[...] = mm(GW, 5 * GW)
    rw_ref[...] = mm(5 * GW, 9 * GW)
    kk = mm(9 * GW, 10 * GW)
    for hd in range(N_HEADS):
        k4_ref[hd] = kk[:, hd * HEAD_DIM:(hd + 1) * HEAD_DIM]
    gt_ref[...] = mm(10 * GW, 10 * GW + LANES)
    qvt_ref[...] = lax.dot_general(wt_ref[...], h, NT_DIMS, preferred_element_type=F32)


def _proj(x2d, g, w, wt, bsz, seq):
    t = x2d.shape[0]
    tm = min(512, seq)
    per = seq // tm
    widths = (GW, 4 * GW, 4 * GW, LANES)
    return pl.pallas_call(
        _proj_kernel,
        grid=(t // tm,),
        in_specs=[_row_spec(tm, D_MODEL), _const_spec((1, D_MODEL)), _const_spec((D_MODEL, PROJ_COLS)),
                  _const_spec((2 * GW, D_MODEL))],
        out_specs=[_row_spec(tm, c) for c in widths]
        + [pl.BlockSpec((None, N_HEADS, tm, HEAD_DIM), lambda i: (i // per, 0, i % per, 0)),
           pl.BlockSpec((None, 2 * GW, tm), lambda i: (i // per, 0, i % per))],
        out_shape=[jax.ShapeDtypeStruct((t, c), F32) for c in widths]
        + [jax.ShapeDtypeStruct((bsz, N_HEADS, seq, HEAD_DIM), F32),
           jax.ShapeDtypeStruct((bsz, 2 * GW, seq), F32)],
        compiler_params=_params("parallel"),
        name="proj",
    )(x2d, g, w, wt)


def _s5_param_kernel(lr_ref, li_ref, ls_ref, br_ref, bi_ref, abr_ref, abi_ref, bbr_ref, bbi_ref):
    lr, li = lr_ref[...], li_ref[...]
    step = jnp.exp(ls_ref[...])
    mag = jnp.exp(lr * step)
    ang = li * step
    ab_re, ab_im = mag * jnp.cos(ang), mag * jnp.sin(ang)
    den = lr * lr + li * li
    co_re = ((ab_re - 1.0) * lr + ab_im * li) / den
    co_im = (ab_im * lr - (ab_re - 1.0) * li) / den
    abr_ref[...] = ab_re
    abi_ref[...] = ab_im
    br, bi = br_ref[...], bi_ref[...]
    bbr_ref[...] = co_re[:, None, :] * br - co_im[:, None, :] * bi
    bbi_ref[...] = co_re[:, None, :] * bi + co_im[:, None, :] * br


def _s5_params(lam_re, lam_im, log_step, b_re, b_im):
    n = lam_re.shape[0]
    return pl.pallas_call(
        _s5_param_kernel,
        out_shape=[jax.ShapeDtypeStruct((n, S5_STATE), F32)] * 2
        + [jax.ShapeDtypeStruct((n, S5_GROUP, S5_STATE), F32)] * 2,
        name="s5_params",
    )(lam_re, lam_im, log_step, b_re, b_im)


def _s5_kernel(u_ref, wb_ref, are_ref, aim_ref, wc_ref, d_ref, wg_ref, bg_ref, o_ref, xs_ref, st_ref,
               *, tm, nb):
    @pl.when(pl.program_id(0) == 0)
    def _():
        st_ref[...] = jnp.zeros_like(st_ref)

    u = u_ref[...].reshape(nb * tm, GW)
    bu = _bdot(u, wb_ref[...])
    nslab = S5_P // LANES
    for j in range(2 * nslab):
        xs_ref[j] = bu[:, j * LANES:(j + 1) * LANES]
    grp = 4
    for c0 in range(0, nslab, grp):
        a_re = [jnp.broadcast_to(are_ref[:, (c0 + c) * LANES:(c0 + c + 1) * LANES], (nb, LANES))
                for c in range(grp)]
        a_im = [jnp.broadcast_to(aim_ref[:, (c0 + c) * LANES:(c0 + c + 1) * LANES], (nb, LANES))
                for c in range(grp)]

        def body(t, carry, c0=c0, a_re=a_re, a_im=a_im):
            rows = pl.ds(t, nb, stride=tm)
            out = []
            for c in range(grp):
                s_re, s_im = carry[c]
                n_re = a_re[c] * s_re - a_im[c] * s_im + xs_ref[c0 + c, rows, :]
                n_im = a_re[c] * s_im + a_im[c] * s_re + xs_ref[nslab + c0 + c, rows, :]
                xs_ref[c0 + c, rows, :] = n_re
                xs_ref[nslab + c0 + c, rows, :] = n_im
                out.append((n_re, n_im))
            return tuple(out)

        init = tuple((st_ref[c0 + c], st_ref[nslab + c0 + c]) for c in range(grp))
        fin = lax.fori_loop(0, tm, body, init)
        for c in range(grp):
            st_ref[c0 + c] = fin[c][0]
            st_ref[nslab + c0 + c] = fin[c][1]

    xs = jnp.concatenate([xs_ref[j] for j in range(2 * nslab)], axis=1)
    y = _bdot(xs, wc_ref[...]) + d_ref[...] * u
    z = jax.nn.gelu(y)
    out = z * _sigmoid(_bdot(z, wg_ref[...]) + bg_ref[...])
    o_ref[...] = out.reshape(nb, tm, GW)


def _s5(u3, wb, a_re, a_im, wc, d, wg, bg):
    nb, seq, _ = u3.shape
    tm = min(64, seq)
    blk = pl.BlockSpec((nb, tm, GW), lambda i: (0, i, 0))
    return pl.pallas_call(
        functools.partial(_s5_kernel, tm=tm, nb=nb),
        grid=(seq // tm,),
        in_specs=[blk, _const_spec((GW, 2 * S5_P)), _const_spec((1, S5_P)),
                  _const_spec((1, S5_P)), _const_spec((2 * S5_P, GW)), _const_spec((1, GW)),
                  _const_spec((GW, GW)), _const_spec((1, GW))],
        out_specs=blk,
        out_shape=jax.ShapeDtypeStruct((nb, seq, GW), F32),
        scratch_shapes=[pltpu.VMEM((2 * S5_P // LANES, nb * tm, LANES), F32),
                        pltpu.VMEM((2 * S5_P // LANES, nb, LANES), F32)],
        compiler_params=_params("arbitrary"),
        name="s5",
    )(u3, wb, a_re, a_im, wc, d, wg, bg)


def _mlstm_kernel(p_ref, g_ref, cw_ref, gb_ref, o_ref, pad_ref, *, seq):
    cl = ML_CHUNK
    dh = HEAD_DIM
    pad_ref[0:8, :] = jnp.zeros((8, 2 * GW), F32)
    pad_ref[8:8 + seq, :] = p_ref[0, :, 0:2 * GW]
    row = lax.broadcasted_iota(jnp.int32, (cl, cl), 0)
    col = lax.broadcasted_iota(jnp.int32, (cl, cl), 1)
    trif = jnp.where(col <= row, 1.0, 0.0).astype(F32)
    causal_t = row <= col
    cw = cw_ref[...]

    def chunk(c, carry):
        cts, n8s, ms = carry
        r0 = pl.multiple_of(c * cl, cl)
        win = pad_ref[pl.ds(r0, cl + 8), :]
        conv = cw[0:1, :] * win[5:5 + cl, :]
        for j in range(1, CONV_K):
            conv = conv + cw[j:j + 1, :] * win[5 + j:5 + j + cl, :]
        qk = conv * _sigmoid(conv)
        qt = qk[:, 0:GW].T.astype(BF16)
        kb = (qk[:, GW:2 * GW] * (dh ** -0.5)).astype(BF16)
        vt = p_ref[0, pl.ds(r0, cl), 2 * GW:3 * GW].T.astype(BF16)
        ot = p_ref[0, pl.ds(r0, cl), 3 * GW:4 * GW].T
        gc = g_ref[0, pl.ds(r0, cl), :] + gb_ref[...]
        bcum = _hdot(trif, -_softplus(-gc))
        gct = gc.T
        bct = bcum.T
        outs, cts_n, n8s_n, ms_n = [], [], [], []
        for h in range(N_HEADS):
            hs = slice(h * dh, (h + 1) * dh)
            q_t, k_h, v_t = qt[hs, :], kb[:, hs], vt[hs, :]
            colv = bcum[:, N_HEADS + h:N_HEADS + h + 1] - gc[:, h:h + 1]
            br = bct[N_HEADS + h:N_HEADS + h + 1, :]
            lir = gct[h:h + 1, :]
            m_prev = ms[h]
            dmat = jnp.where(causal_t, br - colv, -jnp.inf)
            b_inter = br + m_prev
            m_comb = jnp.maximum(b_inter, jnp.max(dmat, axis=0, keepdims=True))
            w_inter = jnp.exp(b_inter - m_comb)
            s_t = jnp.exp(dmat - m_comb) * jnp.dot(k_h, q_t, preferred_element_type=F32)
            num = (jnp.dot(v_t, s_t.astype(BF16), preferred_element_type=F32)
                   + w_inter * jnp.dot(cts[h].astype(BF16), q_t, preferred_element_type=F32))
            nq = jnp.dot(n8s[h].astype(BF16), q_t, preferred_element_type=F32)[0:1, :]
            den = jnp.sum(s_t, axis=0, keepdims=True) + w_inter * nq
            hh = num / jnp.maximum(jnp.abs(den), jnp.exp(-m_comb))
            outs.append(_sigmoid(ot[hs, :]) * hh)
            b_last = br[:, cl - 1:cl]
            w_st = b_last - br + lir
            m_in = jnp.max(w_st, axis=1, keepdims=True)
            e_row = jnp.exp(w_st - m_in)
            kv_t = jnp.dot((v_t * e_row).astype(BF16), k_h, preferred_element_type=F32)
            ks8 = jnp.dot(jnp.broadcast_to(e_row, (8, cl)).astype(BF16), k_h, preferred_element_type=F32)
            m_new = jnp.maximum(b_last + m_prev, m_in)
            a = jnp.exp(b_last + m_prev - m_new)
            e = jnp.exp(m_in - m_new)
            cts_n.append(a * cts[h] + e * kv_t)
            n8s_n.append(a * n8s[h] + e * ks8)
            ms_n.append(m_new)
        o_ref[0, pl.ds(r0, cl), :] = jnp.concatenate(outs, axis=0).T
        return tuple(cts_n), tuple(n8s_n), tuple(ms_n)

    init = (tuple(jnp.zeros((dh, dh), F32) for _ in range(N_HEADS)),
            tuple(jnp.zeros((8, dh), F32) for _ in range(N_HEADS)),
            tuple(jnp.zeros((1, 1), F32) for _ in range(N_HEADS)))
    lax.fori_loop(0, seq // cl, chunk, init)


def _mlstm(p_ml, gates, conv_w, gate_bias):
    bsz, seq, _ = p_ml.shape
    return pl.pallas_call(
        functools.partial(_mlstm_kernel, seq=seq),
        grid=(bsz,),
        in_specs=[pl.BlockSpec((1, seq, 4 * GW), lambda b: (b, 0, 0)),
                  pl.BlockSpec((1, seq, LANES), lambda b: (b, 0, 0)),
                  _const_spec((CONV_K, 2 * GW)), _const_spec((1, LANES))],
        out_specs=pl.BlockSpec((1, seq, GW), lambda b: (b, 0, 0)),
        out_shape=jax.ShapeDtypeStruct((bsz, seq, GW), F32),
        scratch_shapes=[pltpu.VMEM((seq + 8, 2 * GW), F32)],
        compiler_params=_params("parallel"),
        name="mlstm",
    )(p_ml, gates, conv_w, gate_bias)


def _rwkv_pre_kernel(p_ref, pv_ref, mu_ref, w0_ref, w2_ref, a0_ref, a2_ref, g2_ref, kk_ref, ka_ref, rk_ref,
                     bd_ref, sela_ref, selb_ref, kw_o, v_o, c_o, bonus_o, g_o):
    x = p_ref[0]
    last = jnp.where(pl.program_id(1) > 0, pv_ref[0][7:8, :], 0.0)
    rowi = lax.broadcasted_iota(jnp.int32, x.shape, 0)
    prev = jnp.where(rowi == 0, last, pltpu.roll(x, 1, 0))
    p = x + (prev - x) * mu_ref[...]
    r, k, v = p[:, 0:GW], p[:, GW:2 * GW], p[:, 2 * GW:3 * GW]
    lo = p[:, 3 * GW:3 * GW + LANES]
    g_lo = p[:, 3 * GW + LANES:4 * GW]
    w = -_softplus(-(w0_ref[...] + _bdot(jnp.tanh(lo), w2_ref[...]))) - 0.5
    decay = jnp.exp(-jnp.exp(w))
    a = _sigmoid(a0_ref[...] + _bdot(lo, a2_ref[...]))
    g_o[0] = _bdot(_sigmoid(g_lo), g2_ref[...])
    bd = bd_ref[...]
    kk = k * kk_ref[...]
    kk = kk * lax.rsqrt(_sum3(kk * kk, bd) + 1e-12)
    km = k * (1.0 + (a - 1.0) * ka_ref[...])
    alp = kk * a
    for i, val in enumerate((kk, decay * r, decay, alp, km)):
        kw_o[0, :, i * GW:(i + 1) * GW] = val
    v_o[0] = v
    c_o[0] = _sum3(alp * r, sela_ref[...]) + _sum3(km * r, selb_ref[...])
    bonus_o[0] = _sum3(r * km * rk_ref[...], bd) * v


def _rwkv_pre(p_rw, mu, w0, w2p, a0, a2p, g2, k_k, k_a, r_k, bd, sela, selb):
    bsz, seq, _ = p_rw.shape
    tr = min(256, seq)
    row = lambda c: pl.BlockSpec((1, tr, c), lambda b, i: (b, i, 0))
    prev = pl.BlockSpec((1, 8, 4 * GW), lambda b, i: (b, jnp.maximum(i * (tr // 8) - 1, 0), 0))
    vec = lambda c: _const_spec((1, c))
    widths = (RW_VECS * GW, GW, LANES, GW, GW)
    return pl.pallas_call(
        _rwkv_pre_kernel,
        grid=(bsz, seq // tr),
        in_specs=[row(4 * GW), prev, vec(4 * GW), vec(GW), _const_spec((LANES, GW)), vec(GW),
                  _const_spec((LANES, GW)), _const_spec((LANES, GW)), vec(GW), vec(GW), vec(GW),
                  _const_spec((GW, GW)), _const_spec((GW, LANES)), _const_spec((GW, LANES))],
        out_specs=[row(c) for c in widths],
        out_shape=[jax.ShapeDtypeStruct((bsz, seq, c), F32) for c in widths],
        compiler_params=_params("parallel", "parallel"),
        name="rwkv_pre",
    )(p_rw, p_rw, mu, w0, w2p, a0, a2p, g2, k_k, k_a, r_k, bd, sela, selb)


def _rwkv_seq_kernel(kw_ref, v_ref, c_ref, y_ref, s_ref, *, tb):
    @pl.when(pl.program_id(0) == 0)
    def _():
        s_ref[...] = jnp.zeros_like(s_ref)

    def step(t, carry):
        c1 = c_ref[t, 0]
        c2 = c_ref[t, 1]

        def value_row(vp, inner):
            s = s_ref[vp]
            sa = -jnp.sum(s * kw_ref[t, 0], axis=0, keepdims=True)
            y0 = jnp.sum(s * kw_ref[t, 1], axis=0, keepdims=True)
            vv = v_ref[t, vp]
            y_ref[t, vp] = y0 + sa * c1 + vv * c2
            s_ref[vp] = s * kw_ref[t, 2] + sa * kw_ref[t, 3] + vv * kw_ref[t, 4]
            return inner

        lax.fori_loop(0, HEAD_DIM // 2, value_row, 0, unroll=RW_ROW_UNROLL)
        return carry

    lax.fori_loop(0, tb, step, 0)


def _rwkv_seq(kw, v4, c4):
    seq, _, dk, nl = kw.shape
    tb = min(32, seq)
    half = HEAD_DIM // 2
    return pl.pallas_call(
        functools.partial(_rwkv_seq_kernel, tb=tb),
        grid=(seq // tb,),
        in_specs=[pl.BlockSpec((tb, RW_VECS, dk, nl), lambda i: (i, 0, 0, 0)),
                  pl.BlockSpec((tb, half, 1, nl), lambda i: (i, 0, 0, 0)),
                  pl.BlockSpec((tb, 2, 1, nl), lambda i: (i, 0, 0, 0))],
        out_specs=pl.BlockSpec((tb, half, 1, nl), lambda i: (i, 0, 0, 0)),
        out_shape=jax.ShapeDtypeStruct((seq, half, 1, nl), F32),
        scratch_shapes=[pltpu.VMEM((half, dk, nl), F32)],
        compiler_params=_params("arbitrary"),
        name="rwkv_seq",
    )(kw, v4, c4)


def _bias_kernel(rb_ref, bk_ref, o_ref):
    h = pl.program_id(0)
    bk = bk_ref[0]
    out = jnp.full(bk.shape, -jnp.inf, F32)
    for b in range(REL_BUCKETS):
        out = jnp.where(bk == b, rb_ref[b, h], out)
    o_ref[0, 0] = out


def _bias_tiles(rel_bias, buckets):
    bs = MOBA_BLOCK
    return pl.pallas_call(
        _bias_kernel,
        grid=(N_HEADS, 2),
        in_specs=[pl.BlockSpec(memory_space=pltpu.SMEM), pl.BlockSpec((1, bs, bs), lambda h, k: (k, 0, 0))],
        out_specs=pl.BlockSpec((1, 1, bs, bs), lambda h, k: (h, k, 0, 0)),
        out_shape=jax.ShapeDtypeStruct((N_HEADS, 2, bs, bs), F32),
        name="moba_bias",
    )(rel_bias, buckets)


def _moba_kernel(far_ref, k_ref, qt_ref, vt_ref, bias_ref, o_ref, s_ref, p_ref, *, seq, nsel):
    bs = MOBA_BLOCK
    nb = seq // bs
    h = pl.program_id(1)
    k = k_ref[...]
    qt = qt_ref[...] * (HEAD_DIM ** -0.5)
    kmean = jnp.mean(k.reshape(nb, bs, HEAD_DIM), axis=1)
    gate = _hdot(kmean, qt)
    jj = lax.broadcasted_iota(jnp.int32, (nb, seq), 0)
    qblk = lax.broadcasted_iota(jnp.int32, (nb, seq), 1) // bs
    g = jnp.where(jj < qblk, gate, -jnp.inf)
    selb = jnp.full((nb, seq), -jnp.inf, F32)
    for r in range(nsel):
        m = jnp.max(g, axis=0, keepdims=True)
        idx = jnp.min(jnp.where(g == m, jj, nb), axis=0, keepdims=True)
        hit = jj == idx
        selb = jnp.where(jnp.logical_and(hit, qblk > r), 0.0, selb)
        g = jnp.where(hit, -jnp.inf, g)
    selfar = selb + far_ref[h]
    kb = k.astype(BF16)
    qtb = qt.astype(BF16)
    vtb = vt_ref[...].astype(BF16)
    for i in range(nb):
        qs = slice(i * bs, (i + 1) * bs)
        q_i = qtb[:, qs]
        mx = None
        for j in range(i + 1):
            ks = slice(j * bs, (j + 1) * bs)
            s = jnp.dot(kb[ks, :], q_i, preferred_element_type=F32)
            if j == i:
                s = s + bias_ref[0, 0]
            elif j == i - 1:
                s = s + bias_ref[0, 1] + selb[j:j + 1, qs]
            else:
                s = s + selfar[j:j + 1, qs]
            s_ref[ks, :] = s
            mx = s if mx is None else jnp.maximum(mx, s)
        m = jnp.max(mx, axis=0, keepdims=True)
        lsum = None
        for j in range(i + 1):
            ks = slice(j * bs, (j + 1) * bs)
            p = jnp.exp(s_ref[ks, :] - m)
            p_ref[ks, :] = p.astype(BF16)
            lsum = p if lsum is None else lsum + p
        acc = jnp.dot(vtb[:, 0:(i + 1) * bs], p_ref[0:(i + 1) * bs, :], preferred_element_type=F32)
        o_ref[:, qs] = acc / jnp.sum(lsum, axis=0, keepdims=True)


def _moba(k4, qv_t, bias_tiles, far):
    bsz, _, seq, _ = k4.shape
    nsel = max(1, min(MOBA_TOPK, seq // MOBA_BLOCK - 1))
    return pl.pallas_call(
        functools.partial(_moba_kernel, seq=seq, nsel=nsel),
        grid=(bsz, N_HEADS),
        in_specs=[pl.BlockSpec(memory_space=pltpu.SMEM),
                  pl.BlockSpec((None, None, seq, HEAD_DIM), lambda b, h: (b, h, 0, 0)),
                  pl.BlockSpec((None, HEAD_DIM, seq), lambda b, h: (b, h, 0)),
                  pl.BlockSpec((None, HEAD_DIM, seq), lambda b, h: (b, N_HEADS + h, 0)),
                  pl.BlockSpec((1, 2, MOBA_BLOCK, MOBA_BLOCK), lambda b, h: (h, 0, 0, 0))],
        out_specs=pl.BlockSpec((None, HEAD_DIM, seq), lambda b, h: (b, h, 0)),
        out_shape=jax.ShapeDtypeStruct((bsz, GW, seq), F32),
        scratch_shapes=[pltpu.VMEM((seq, MOBA_BLOCK), F32), pltpu.VMEM((seq, MOBA_BLOCK), BF16)],
        compiler_params=_params("parallel", "parallel"),
        name="moba",
    )(far, k4, qv_t, qv_t, bias_tiles)


def _t5_bucket(rel):
    n = jnp.maximum(rel, 0)
    max_exact = REL_BUCKETS // 2
    nf = jnp.maximum(n, 1).astype(F32)
    large = max_exact + (jnp.log(nf / max_exact) / math.log(REL_MAX_DIST / max_exact)
                         * (REL_BUCKETS - max_exact)).astype(jnp.int32)
    return jnp.where(n < max_exact, n, jnp.minimum(large, REL_BUCKETS - 1))


def _bucket_tiles():
    kpos = jnp.arange(MOBA_BLOCK)[:, None]
    qpos = jnp.arange(MOBA_BLOCK)[None, :]
    rel = qpos - kpos
    own = jnp.where(rel >= 0, _t5_bucket(rel), -1)
    prev = _t5_bucket(rel + MOBA_BLOCK)
    return jnp.stack([own, prev]).astype(jnp.int32)


def _mixout_kernel(ys5_ref, yml_ref, yrw_ref, bonus_ref, g_ref, ymbt_ref, lnw_ref, lnb_ref, bd_ref,
                   gain_ref, w_ref, gpost_ref, x_ref, o_ref):
    bd = bd_ref[...]
    y = yrw_ref[...]
    mean = _sum3(y, bd) * (1.0 / HEAD_DIM)
    d = y - mean
    var = _sum3(d * d, bd) * (1.0 / HEAD_DIM)
    yrw = (d * lax.rsqrt(var + RW_GN_EPS) * lnw_ref[...] + lnb_ref[...] + bonus_ref[...]) * g_ref[...]
    acc = None
    for i, yg in enumerate((ys5_ref[...], yml_ref[...], yrw, ymbt_ref[...].T)):
        cols = slice(i * GW, (i + 1) * GW)
        part = jnp.dot(_rms(yg, gain_ref[:, cols]).astype(BF16), w_ref[cols, :], preferred_element_type=F32)
        acc = part if acc is None else acc + part
    o_ref[...] = x_ref[...] + _rms(acc, gpost_ref[...])


def _mixout(ys5, yml, yrw, bonus, g, ymb_t, ln_w, ln_b, bd, gain, w, gpost, x2d, seq):
    t = x2d.shape[0]
    tm = min(512, seq)
    per = seq // tm
    vec = lambda c: _const_spec((1, c))
    return pl.pallas_call(
        _mixout_kernel,
        grid=(t // tm,),
        in_specs=[_row_spec(tm, GW)] * 5
        + [pl.BlockSpec((None, GW, tm), lambda i: (i // per, 0, i % per)),
           vec(GW), vec(GW), _const_spec((GW, GW)), vec(D_MODEL),
           _const_spec((D_MODEL, D_MODEL)), vec(D_MODEL), _row_spec(tm, D_MODEL)],
        out_specs=_row_spec(tm, D_MODEL),
        out_shape=jax.ShapeDtypeStruct((t, D_MODEL), F32),
        compiler_params=_params("parallel"),
        name="mixout",
    )(ys5, yml, yrw, bonus, g, ymb_t, ln_w, ln_b, bd, gain, w, gpost, x2d)


def _kv_kernel(m_ref, g_ref, w_ref, o_ref):
    h = _rms(m_ref[0], g_ref[...]).astype(BF16)
    o_ref[0] = jnp.dot(h, w_ref[...], preferred_element_type=F32).astype(BF16)


def _kv(mem, g, w):
    bsz, m, _ = mem.shape
    return pl.pallas_call(
        _kv_kernel,
        grid=(bsz,),
        in_specs=[pl.BlockSpec((1, m, D_MODEL), lambda b: (b, 0, 0)), _const_spec((1, D_MODEL)),
                  _const_spec((D_MODEL, 2 * D_MODEL))],
        out_specs=pl.BlockSpec((1, m, 2 * D_MODEL), lambda b: (b, 0, 0)),
        out_shape=jax.ShapeDtypeStruct((bsz, m, 2 * D_MODEL), BF16),
        compiler_params=_params("parallel"),
        name="xa_kv",
    )(mem, g, w)


def _xattn_kernel(x_ref, kv_ref, gpre_ref, wq_ref, wo_ref, gpost_ref, o_ref):
    x = x_ref[...]
    q = jnp.dot(_rms(x, gpre_ref[...]).astype(BF16), wq_ref[...], preferred_element_type=F32)
    acc = None
    for hd in range(XA_HEADS):
        cols = slice(hd * XA_HEAD_DIM, (hd + 1) * XA_HEAD_DIM)
        vcols = slice(D_MODEL + hd * XA_HEAD_DIM, D_MODEL + (hd + 1) * XA_HEAD_DIM)
        s = lax.dot_general(q[:, cols].astype(BF16), kv_ref[0, :, cols], NT_DIMS,
                            preferred_element_type=F32) * (XA_HEAD_DIM ** -0.5)
        p = jnp.exp(s - jnp.max(s, axis=-1, keepdims=True))
        o = jnp.dot(p.astype(BF16), kv_ref[0, :, vcols], preferred_element_type=F32)
        o = o / jnp.sum(p, axis=-1, keepdims=True)
        part = jnp.dot(o.astype(BF16), wo_ref[cols, :], preferred_element_type=F32)
        acc = part if acc is None else acc + part
    o_ref[...] = x + _rms(acc, gpost_ref[...])


def _xattn(x2d, kv, gpre, wq, wo, gpost, seq):
    t = x2d.shape[0]
    m = kv.shape[1]
    tm = min(512, seq)
    per = seq // tm
    vec = _const_spec((1, D_MODEL))
    sq = _const_spec((D_MODEL, D_MODEL))
    return pl.pallas_call(
        _xattn_kernel,
        grid=(t // tm,),
        in_specs=[_row_spec(tm, D_MODEL), pl.BlockSpec((1, m, 2 * D_MODEL), lambda i: (i // per, 0, 0)),
                  vec, sq, sq, vec],
        out_specs=_row_spec(tm, D_MODEL),
        out_shape=jax.ShapeDtypeStruct((t, D_MODEL), F32),
        compiler_params=_params("parallel"),
        name="xattn",
    )(x2d, kv, gpre, wq, wo, gpost)


def _ffn_kernel(x_ref, gpre_ref, wi_ref, wo_ref, gpost_ref, o_ref):
    x = x_ref[...]
    h = _rms(x, gpre_ref[...]).astype(BF16)
    half = D_FF // 2
    acc = None
    for c in range(2):
        gate = jnp.dot(h, wi_ref[:, c * half:(c + 1) * half], preferred_element_type=F32)
        up = jnp.dot(h, wi_ref[:, D_FF + c * half:D_FF + (c + 1) * half], preferred_element_type=F32)
        act = (gate * _sigmoid(gate) * up).astype(BF16)
        part = jnp.dot(act, wo_ref[c * half:(c + 1) * half, :], preferred_element_type=F32)
        acc = part if acc is None else acc + part
    o_ref[...] = x + _rms(acc, gpost_ref[...])


def _ffn(x2d, gpre, wi, wo, gpost):
    t = x2d.shape[0]
    tm = min(256, t)
    vec = _const_spec((1, D_MODEL))
    once = pl.Buffered(1)
    return pl.pallas_call(
        _ffn_kernel,
        grid=(t // tm,),
        in_specs=[_row_spec(tm, D_MODEL), vec,
                  pl.BlockSpec((D_MODEL, 2 * D_FF), lambda i: (0, 0), pipeline_mode=once),
                  pl.BlockSpec((D_FF, D_MODEL), lambda i: (0, 0), pipeline_mode=once), vec],
        out_specs=_row_spec(tm, D_MODEL),
        out_shape=jax.ShapeDtypeStruct((t, D_MODEL), F32),
        compiler_params=_params("parallel"),
        name="ffn",
    )(x2d, gpre, wi, wo, gpost)


def _block_diag(blocks):
    g, a, b = blocks.shape
    eye = jnp.eye(g, dtype=blocks.dtype)
    return jnp.einsum('gab,gk->gakb', blocks, eye).reshape(g * a, g * b)


def kernel(x, mem, rel_bias, norm_pre_mix, norm_post_mix, norm_pre_xa, norm_post_xa, norm_pre_ffn,
           norm_post_ffn, norm_mem, w_in, mix_out_gain, w_out, s5_lam_re, s5_lam_im, s5_log_step,
           s5_b_re, s5_b_im, s5_c_re, s5_c_im, s5_d, s5_w_glu, s5_b_glu, ml_conv, ml_i_bias, ml_f_bias,
           rw_mu, rw_w0, rw_w2, rw_a0, rw_a2, rw_g2, rw_k_k, rw_k_a, rw_r_k, rw_ln_w, rw_ln_b,
           xa_wq, xa_wkv, xa_wo, ffn_w_in, ffn_w_out):
    bsz, seq, _ = x.shape
    depth = w_in.shape[0]
    t = bsz * seq
    row = lambda a: a.reshape(1, -1)

    ml_end = GW + 4 * GW
    rw_lo = ml_end + 2 * N_HEADS
    mb_lo = rw_lo + 4 * GW
    w_proj = jnp.concatenate(
        [w_in[:, :, :ml_end], w_in[:, :, rw_lo:mb_lo], w_in[:, :, mb_lo + GW:mb_lo + 2 * GW],
         w_in[:, :, ml_end:rw_lo], jnp.zeros((depth, D_MODEL, LANES - 2 * N_HEADS), w_in.dtype)],
        axis=2).astype(BF16)
    w_qv_t = jnp.concatenate([w_in[:, :, mb_lo:mb_lo + GW], w_in[:, :, mb_lo + 2 * GW:mb_lo + 3 * GW]],
                             axis=2).transpose(0, 2, 1).astype(BF16)
    w_out_b = w_out.astype(BF16)
    wq_b, wkv_b, wo_b = xa_wq.astype(BF16), xa_wkv.astype(BF16), xa_wo.astype(BF16)
    ffn_wi_b, ffn_wo_b = ffn_w_in.astype(BF16), ffn_w_out.astype(BF16)
    w_glu_b = s5_w_glu.astype(BF16)

    ng = depth * S5_GROUPS
    ab_re, ab_im, bb_re, bb_im = _s5_params(
        s5_lam_re.reshape(ng, S5_STATE), s5_lam_im.reshape(ng, S5_STATE), s5_log_step.reshape(ng, 1),
        s5_b_re.transpose(0, 1, 3, 2).reshape(ng, S5_GROUP, S5_STATE),
        s5_b_im.transpose(0, 1, 3, 2).reshape(ng, S5_GROUP, S5_STATE))
    ab_re = ab_re.reshape(depth, 1, S5_P)
    ab_im = ab_im.reshape(depth, 1, S5_P)
    bb_re = bb_re.reshape(depth, S5_GROUPS, S5_GROUP, S5_STATE)
    bb_im = bb_im.reshape(depth, S5_GROUPS, S5_GROUP, S5_STATE)

    bd = _block_diag(jnp.ones((N_HEADS, HEAD_DIM, HEAD_DIM), BF16))
    head_of_lane = jnp.arange(GW) // HEAD_DIM
    sela = (head_of_lane[:, None] == jnp.arange(LANES)[None, :]).astype(BF16)
    selb = (head_of_lane[:, None] + N_HEADS == jnp.arange(LANES)[None, :]).astype(BF16)
    zeros64 = jnp.zeros((HEAD_DIM, GW), F32)

    bias_tiles = _bias_tiles(rel_bias, _bucket_tiles())
    far = rel_bias[REL_BUCKETS - 1, :]

    half = HEAD_DIM // 2
    nbh = bsz * N_HEADS
    x2d = x.reshape(t, D_MODEL)
    for l in range(depth):
        u_s5, p_ml, p_rw, gates, k4, qv_t = _proj(x2d, row(norm_pre_mix[l]), w_proj[l], w_qv_t[l], bsz, seq)

        wb = jnp.concatenate([_block_diag(bb_re[l]), _block_diag(bb_im[l])], axis=1).astype(BF16)
        wc = jnp.concatenate([_block_diag(s5_c_re[l].transpose(0, 2, 1)),
                              -_block_diag(s5_c_im[l].transpose(0, 2, 1))], axis=0).astype(BF16)
        y_s5 = _s5(u_s5.reshape(bsz, seq, GW), wb, ab_re[l], ab_im[l], wc, row(s5_d[l]), w_glu_b[l],
                   row(s5_b_glu[l])).reshape(t, GW)

        gate_bias = jnp.concatenate([ml_i_bias[l], ml_f_bias[l], jnp.zeros((LANES - 2 * N_HEADS,), F32)])
        y_ml = _mlstm(p_ml.reshape(bsz, seq, 4 * GW), gates.reshape(bsz, seq, LANES), ml_conv[l],
                      row(gate_bias)).reshape(t, GW)

        w2p = jnp.concatenate([rw_w2[l], zeros64], axis=0)
        a2p = jnp.concatenate([zeros64, rw_a2[l]], axis=0)
        kw, v_rw, c12, bonus, g_rw = _rwkv_pre(
            p_rw.reshape(bsz, seq, 4 * GW), row(rw_mu[l]), row(rw_w0[l]), w2p, row(rw_a0[l]), a2p, rw_g2[l],
            row(rw_k_k[l]), row(rw_k_a[l]), row(rw_r_k[l]), bd, sela, selb)
        kw_t = kw.reshape(bsz, seq, RW_VECS, N_HEADS, HEAD_DIM).transpose(1, 2, 4, 0, 3)
        kw_t = kw_t.reshape(seq, RW_VECS, HEAD_DIM, nbh)
        kw_t = jnp.concatenate([kw_t, kw_t], axis=-1)
        v4 = v_rw.reshape(bsz, seq, N_HEADS, 2, half).transpose(1, 4, 3, 0, 2).reshape(seq, half, 1, 2 * nbh)
        c4 = c12[:, :, :2 * N_HEADS].reshape(bsz, seq, 2, N_HEADS).transpose(1, 2, 0, 3)
        c4 = c4.reshape(seq, 2, 1, nbh)
        c4 = jnp.concatenate([c4, c4], axis=-1)
        y4 = _rwkv_seq(kw_t, v4, c4)
        y_rw = y4.reshape(seq, half, 2, bsz, N_HEADS).transpose(3, 0, 4, 2, 1).reshape(t, GW)

        y_mb_t = _moba(k4, qv_t, bias_tiles, far)

        x2d = _mixout(y_s5, y_ml, y_rw, bonus.reshape(t, GW), g_rw.reshape(t, GW), y_mb_t,
                      row(rw_ln_w[l]), row(rw_ln_b[l]), bd, row(mix_out_gain[l]), w_out_b[l],
                      row(norm_post_mix[l]), x2d, seq)

        kv = _kv(mem, row(norm_mem[l]), wkv_b[l])
        x2d = _xattn(x2d, kv, row(norm_pre_xa[l]), wq_b[l], wo_b[l], row(norm_post_xa[l]), seq)
        x2d = _ffn(x2d, row(norm_pre_ffn[l]), ffn_wi_b[l], ffn_wo_b[l], row(norm_post_ffn[l]))
    return x2d.reshape(bsz, seq, D_MODEL)
```

```python
import functools
import math

import jax
import jax.numpy as jnp
from jax import lax
from jax.experimental import pallas as pl
from jax.experimental.pallas import tpu as pltpu

F32 = jnp.float32
BF16 = jnp.bfloat16
HIGHEST = lax.Precision.HIGHEST

D_MODEL = 1024
N_GROUPS = 4
GW = 256
HEAD_DIM = 64
N_HEADS = 4
S5_GROUP = 16
S5_GROUPS = 16
S5_STATE = 64
S5_P = S5_GROUPS * S5_STATE
CONV_K = 4
ML_CHUNK = 128
RW_GN_EPS = 64e-5
RW_VECS = 5
MOBA_BLOCK = 256
MOBA_TOPK = 3
REL_BUCKETS = 32
REL_MAX_DIST = 128
XA_HEADS = 4
XA_HEAD_DIM = 256
D_FF = 2816
RMS_EPS = 1e-6
LANES = 128
PROJ_COLS = 10 * GW + LANES
VMEM_LIMIT = 56 * 1024 * 1024

NT_DIMS = (((1,), (1,)), ((), ()))


def _rms(x, g):
    return x * lax.rsqrt(jnp.mean(x * x, axis=-1, keepdims=True) + RMS_EPS) * g


def _bdot(a, b):
    return jnp.dot(a.astype(BF16), b.astype(BF16), preferred_element_type=F32)


def _hdot(a, b):
    return jnp.dot(a, b, precision=HIGHEST, preferred_element_type=F32)


def _sum3(x, ones_b):
    hi = x.astype(BF16)
    r1 = x - hi.astype(F32)
    mid = r1.astype(BF16)
    lo = (r1 - mid.astype(F32)).astype(BF16)
    dot = lambda a: jnp.dot(a, ones_b, preferred_element_type=F32)
    return dot(hi) + dot(mid) + dot(lo)


def _sigmoid(x):
    return 1.0 / (1.0 + jnp.exp(-x))


def _softplus(x):
    return jnp.maximum(x, 0.0) + jnp.log1p(jnp.exp(-jnp.abs(x)))


def _params(*sem):
    return pltpu.CompilerParams(dimension_semantics=sem, vmem_limit_bytes=VMEM_LIMIT)


def _row_spec(tm, cols):
    return pl.BlockSpec((tm, cols), lambda i: (i, 0))


def _const_spec(shape):
    return pl.BlockSpec(shape, lambda *_: (0,) * len(shape))


def _proj_kernel(x_ref, g_ref, w_ref, wt_ref, s5_ref, ml_ref, rw_ref, gt_ref, k4_ref, qvt_ref):
    h = _rms(x_ref[...], g_ref[...]).astype(BF16)

    def mm(lo, hi):
        return jnp.dot(h, w_ref[:, lo:hi], preferred_element_type=F32)

    s5_ref[...] = mm(0, GW)
    ml_ref[...] = mm(GW, 5 * GW)
    rw_ref[...] = mm(5 * GW, 9 * GW)
    kk = mm(9 * GW, 10 * GW)
    for hd in range(N_HEADS):
        k4_ref[hd] = kk[:, hd * HEAD_DIM:(hd + 1) * HEAD_DIM]
    gt_ref[...] = mm(10 * GW, 10 * GW + LANES)
    qvt_ref[...] = lax.dot_general(wt_ref[...], h, NT_DIMS, preferred_element_type=F32)


def _proj(x2d, g, w, wt, bsz, seq):
    t = x2d.shape[0]
    tm = min(512, seq)
    per = seq // tm
    widths = (GW, 4 * GW, 4 * GW, LANES)
    return pl.pallas_call(
        _proj_kernel,
        grid=(t // tm,),
        in_specs=[_row_spec(tm, D_MODEL), _const_spec((1, D_MODEL)), _const_spec((D_MODEL, PROJ_COLS)),
                  _const_spec((2 * GW, D_MODEL))],
        out_specs=[_row_spec(tm, c) for c in widths]
        + [pl.BlockSpec((None, N_HEADS, tm, HEAD_DIM), lambda i: (i // per, 0, i % per, 0)),
           pl.BlockSpec((None, 2 * GW, tm), lambda i: (i // per, 0, i % per))],
        out_shape=[jax.ShapeDtypeStruct((t, c), F32) for c in widths]
        + [jax.ShapeDtypeStruct((bsz, N_HEADS, seq, HEAD_DIM), F32),
           jax.ShapeDtypeStruct((bsz, 2 * GW, seq), F32)],
        compiler_params=_params("parallel"),
        name="proj",
    )(x2d, g, w, wt)


def _s5_param_kernel(lr_ref, li_ref, ls_ref, br_ref, bi_ref, abr_ref, abi_ref, bbr_ref, bbi_ref):
    lr, li = lr_ref[...], li_ref[...]
    step = jnp.exp(ls_ref[...])
    mag = jnp.exp(lr * step)
    ang = li * step
    ab_re, ab_im = mag * jnp.cos(ang), mag * jnp.sin(ang)
    den = lr * lr + li * li
    co_re = ((ab_re - 1.0) * lr + ab_im * li) / den
    co_im = (ab_im * lr - (ab_re - 1.0) * li) / den
    abr_ref[...] = ab_re
    abi_ref[...] = ab_im
    br, bi = br_ref[...], bi_ref[...]
    bbr_ref[...] = co_re[:, None, :] * br - co_im[:, None, :] * bi
    bbi_ref[...] = co_re[:, None, :] * bi + co_im[:, None, :] * br


def _s5_params(lam_re, lam_im, log_step, b_re, b_im):
    n = lam_re.shape[0]
    return pl.pallas_call(
        _s5_param_kernel,
        out_shape=[jax.ShapeDtypeStruct((n, S5_STATE), F32)] * 2
        + [jax.ShapeDtypeStruct((n, S5_GROUP, S5_STATE), F32)] * 2,
        name="s5_params",
    )(lam_re, lam_im, log_step, b_re, b_im)


def _s5_kernel(u_ref, wb_ref, are_ref, aim_ref, wc_ref, d_ref, wg_ref, bg_ref, after_ref, o_ref, xs_ref,
               st_ref, *, tm, nb):
    del after_ref

    @pl.when(pl.program_id(0) == 0)
    def _():
        st_ref[...] = jnp.zeros_like(st_ref)

    u = jnp.swapaxes(u_ref[...], 0, 1).reshape(tm * nb, GW)
    xs_ref[...] = _bdot(u, wb_ref[...])
    ch = 256
    for c in range(S5_P // ch):
        re_cols = slice(c * ch, (c + 1) * ch)
        im_cols = slice(S5_P + c * ch, S5_P + (c + 1) * ch)
        a_re = jnp.broadcast_to(are_ref[:, re_cols], (nb, ch))
        a_im = jnp.broadcast_to(aim_ref[:, re_cols], (nb, ch))

        def body(t, carry, re_cols=re_cols, im_cols=im_cols, a_re=a_re, a_im=a_im):
            s_re, s_im = carry
            rows = pl.ds(pl.multiple_of(t * nb, nb), nb)
            n_re = a_re * s_re - a_im * s_im + xs_ref[rows, re_cols]
            n_im = a_re * s_im + a_im * s_re + xs_ref[rows, im_cols]
            xs_ref[rows, re_cols] = n_re
            xs_ref[rows, im_cols] = n_im
            return n_re, n_im

        s_re, s_im = lax.fori_loop(0, tm, body, (st_ref[:, re_cols], st_ref[:, im_cols]))
        st_ref[:, re_cols] = s_re
        st_ref[:, im_cols] = s_im

    y = _bdot(xs_ref[...], wc_ref[...]) + d_ref[...] * u
    z = jax.nn.gelu(y)
    out = z * _sigmoid(_bdot(z, wg_ref[...]) + bg_ref[...])
    o_ref[...] = jnp.swapaxes(out.reshape(tm, nb, GW), 0, 1)


def _s5(u3, wb, a_re, a_im, wc, d, wg, bg, after):
    nb, seq, _ = u3.shape
    tm = min(64, seq)
    blk = pl.BlockSpec((nb, tm, GW), lambda i: (0, i, 0))
    return pl.pallas_call(
        functools.partial(_s5_kernel, tm=tm, nb=nb),
        grid=(seq // tm,),
        in_specs=[blk, _const_spec((GW, 2 * S5_P)), _const_spec((1, S5_P)),
                  _const_spec((1, S5_P)), _const_spec((2 * S5_P, GW)), _const_spec((1, GW)),
                  _const_spec((GW, GW)), _const_spec((1, GW)), pl.BlockSpec(memory_space=pl.ANY)],
        out_specs=blk,
        out_shape=jax.ShapeDtypeStruct((nb, seq, GW), F32),
        scratch_shapes=[pltpu.VMEM((nb * tm, 2 * S5_P), F32), pltpu.VMEM((nb, 2 * S5_P), F32)],
        compiler_params=_params("arbitrary"),
        name="s5",
    )(u3, wb, a_re, a_im, wc, d, wg, bg, after)


def _mlstm_kernel(p_ref, g_ref, cw_ref, gb_ref, o_ref, pad_ref, *, seq):
    cl = ML_CHUNK
    dh = HEAD_DIM
    pad_ref[0:8, :] = jnp.zeros((8, 2 * GW), F32)
    pad_ref[8:8 + seq, :] = p_ref[0, :, 0:2 * GW]
    row = lax.broadcasted_iota(jnp.int32, (cl, cl), 0)
    col = lax.broadcasted_iota(jnp.int32, (cl, cl), 1)
    trif = jnp.where(col <= row, 1.0, 0.0).astype(F32)
    causal_t = row <= col
    cw = cw_ref[...]

    def chunk(c, carry):
        cts, n8s, ms = carry
        r0 = pl.multiple_of(c * cl, cl)
        win = pad_ref[pl.ds(r0, cl + 8), :]
        conv = cw[0:1, :] * win[5:5 + cl, :]
        for j in range(1, CONV_K):
            conv = conv + cw[j:j + 1, :] * win[5 + j:5 + j + cl, :]
        qk = conv * _sigmoid(conv)
        qt = qk[:, 0:GW].T.astype(BF16)
        kb = (qk[:, GW:2 * GW] * (dh ** -0.5)).astype(BF16)
        vt = p_ref[0, pl.ds(r0, cl), 2 * GW:3 * GW].T.astype(BF16)
        ot = p_ref[0, pl.ds(r0, cl), 3 * GW:4 * GW].T
        gc = g_ref[0, pl.ds(r0, cl), :] + gb_ref[...]
        bcum = _hdot(trif, -_softplus(-gc))
        gct = gc.T
        bct = bcum.T
        outs, cts_n, n8s_n, ms_n = [], [], [], []
        for h in range(N_HEADS):
            hs = slice(h * dh, (h + 1) * dh)
            q_t, k_h, v_t = qt[hs, :], kb[:, hs], vt[hs, :]
            colv = bcum[:, N_HEADS + h:N_HEADS + h + 1] - gc[:, h:h + 1]
            br = bct[N_HEADS + h:N_HEADS + h + 1, :]
            lir = gct[h:h + 1, :]
            m_prev = ms[h]
            dmat = jnp.where(causal_t, br - colv, -jnp.inf)
            b_inter = br + m_prev
            m_comb = jnp.maximum(b_inter, jnp.max(dmat, axis=0, keepdims=True))
            w_inter = jnp.exp(b_inter - m_comb)
            s_t = jnp.exp(dmat - m_comb) * jnp.dot(k_h, q_t, preferred_element_type=F32)
            num = (jnp.dot(v_t, s_t.astype(BF16), preferred_element_type=F32)
                   + w_inter * jnp.dot(cts[h].astype(BF16), q_t, preferred_element_type=F32))
            nq = jnp.dot(n8s[h].astype(BF16), q_t, preferred_element_type=F32)[0:1, :]
            den = jnp.sum(s_t, axis=0, keepdims=True) + w_inter * nq
            hh = num / jnp.maximum(jnp.abs(den), jnp.exp(-m_comb))
            outs.append(_sigmoid(ot[hs, :]) * hh)
            b_last = br[:, cl - 1:cl]
            w_st = b_last - br + lir
            m_in = jnp.max(w_st, axis=1, keepdims=True)
            e_row = jnp.exp(w_st - m_in)
            kv_t = jnp.dot((v_t * e_row).astype(BF16), k_h, preferred_element_type=F32)
            ks8 = jnp.dot(jnp.broadcast_to(e_row, (8, cl)).astype(BF16), k_h, preferred_element_type=F32)
            m_new = jnp.maximum(b_last + m_prev, m_in)
            a = jnp.exp(b_last + m_prev - m_new)
            e = jnp.exp(m_in - m_new)
            cts_n.append(a * cts[h] + e * kv_t)
            n8s_n.append(a * n8s[h] + e * ks8)
            ms_n.append(m_new)
        o_ref[0, pl.ds(r0, cl), :] = jnp.concatenate(outs, axis=0).T
        return tuple(cts_n), tuple(n8s_n), tuple(ms_n)

    init = (tuple(jnp.zeros((dh, dh), F32) for _ in range(N_HEADS)),
            tuple(jnp.zeros((8, dh), F32) for _ in range(N_HEADS)),
            tuple(jnp.zeros((1, 1), F32) for _ in range(N_HEADS)))
    lax.fori_loop(0, seq // cl, chunk, init)


def _mlstm(p_ml, gates, conv_w, gate_bias):
    bsz, seq, _ = p_ml.shape
    return pl.pallas_call(
        functools.partial(_mlstm_kernel, seq=seq),
        grid=(bsz,),
        in_specs=[pl.BlockSpec((1, seq, 4 * GW), lambda b: (b, 0, 0)),
                  pl.BlockSpec((1, seq, LANES), lambda b: (b, 0, 0)),
                  _const_spec((CONV_K, 2 * GW)), _const_spec((1, LANES))],
        out_specs=pl.BlockSpec((1, seq, GW), lambda b: (b, 0, 0)),
        out_shape=jax.ShapeDtypeStruct((bsz, seq, GW), F32),
        scratch_shapes=[pltpu.VMEM((seq + 8, 2 * GW), F32)],
        compiler_params=_params("parallel"),
        name="mlstm",
    )(p_ml, gates, conv_w, gate_bias)


def _rwkv_pre_kernel(p_ref, pv_ref, mu_ref, w0_ref, w2_ref, a0_ref, a2_ref, g2_ref, kk_ref, ka_ref, rk_ref,
                     bd_ref, sela_ref, selb_ref, kw_o, v_o, c_o, bonus_o, g_o):
    x = p_ref[0]
    last = jnp.where(pl.program_id(1) > 0, pv_ref[0][7:8, :], 0.0)
    rowi = lax.broadcasted_iota(jnp.int32, x.shape, 0)
    prev = jnp.where(rowi == 0, last, pltpu.roll(x, 1, 0))
    p = x + (prev - x) * mu_ref[...]
    r, k, v = p[:, 0:GW], p[:, GW:2 * GW], p[:, 2 * GW:3 * GW]
    lo = p[:, 3 * GW:3 * GW + LANES]
    g_lo = p[:, 3 * GW + LANES:4 * GW]
    w = -_softplus(-(w0_ref[...] + _bdot(jnp.tanh(lo), w2_ref[...]))) - 0.5
    decay = jnp.exp(-jnp.exp(w))
    a = _sigmoid(a0_ref[...] + _bdot(lo, a2_ref[...]))
    g_o[0] = _bdot(_sigmoid(g_lo), g2_ref[...])
    bd = bd_ref[...]
    kk = k * kk_ref[...]
    kk = kk * lax.rsqrt(_sum3(kk * kk, bd) + 1e-12)
    km = k * (1.0 + (a - 1.0) * ka_ref[...])
    alp = kk * a
    for i, val in enumerate((kk, decay * r, decay, alp, km)):
        for copy in range(2):
            kw_o[copy, 0, :, i * GW:(i + 1) * GW] = val
    v_o[0] = v
    c_o[0] = _sum3(alp * r, sela_ref[...]) + _sum3(km * r, selb_ref[...])
    bonus_o[0] = _sum3(r * km * rk_ref[...], bd) * v


def _rwkv_pre(p_rw, mu, w0, w2p, a0, a2p, g2, k_k, k_a, r_k, bd, sela, selb):
    bsz, seq, _ = p_rw.shape
    tr = min(256, seq)
    row = lambda c: pl.BlockSpec((1, tr, c), lambda b, i: (b, i, 0))
    prev = pl.BlockSpec((1, 8, 4 * GW), lambda b, i: (b, jnp.maximum(i * (tr // 8) - 1, 0), 0))
    vec = lambda c: _const_spec((1, c))
    widths = (GW, LANES, GW, GW)
    return pl.pallas_call(
        _rwkv_pre_kernel,
        grid=(bsz, seq // tr),
        in_specs=[row(4 * GW), prev, vec(4 * GW), vec(GW), _const_spec((LANES, GW)), vec(GW),
                  _const_spec((LANES, GW)), _const_spec((LANES, GW)), vec(GW), vec(GW), vec(GW),
                  _const_spec((GW, GW)), _const_spec((GW, LANES)), _const_spec((GW, LANES))],
        out_specs=[pl.BlockSpec((2, 1, tr, RW_VECS * GW), lambda b, i: (0, b, i, 0))]
        + [row(c) for c in widths],
        out_shape=[jax.ShapeDtypeStruct((2, bsz, seq, RW_VECS * GW), F32)]
        + [jax.ShapeDtypeStruct((bsz, seq, c), F32) for c in widths],
        compiler_params=_params("parallel", "parallel"),
        name="rwkv_pre",
    )(p_rw, p_rw, mu, w0, w2p, a0, a2p, g2, k_k, k_a, r_k, bd, sela, selb)


def _rwkv_seq_kernel(kw_ref, v_ref, c_ref, after_a, after_b, y_ref, s_ref, *, tb):
    del after_a, after_b

    @pl.when(pl.program_id(0) == 0)
    def _():
        s_ref[...] = jnp.zeros_like(s_ref)

    def step(t, carry):
        s = s_ref[...]
        sa = -jnp.sum(s * kw_ref[t, 0][None], axis=1, keepdims=True)
        y0 = jnp.sum(s * kw_ref[t, 1][None], axis=1, keepdims=True)
        vv = v_ref[t]
        y_ref[t] = y0 + sa * c_ref[t, 0:1] + vv * c_ref[t, 1:2]
        s_ref[...] = s * kw_ref[t, 2][None] + sa * kw_ref[t, 3][None] + vv * kw_ref[t, 4][None]
        return carry

    lax.fori_loop(0, tb, step, 0)


def _rwkv_seq(kw, v4, c4, after_a, after_b):
    seq, _, dk, nl = kw.shape
    tb = min(32, seq)
    half = HEAD_DIM // 2
    return pl.pallas_call(
        functools.partial(_rwkv_seq_kernel, tb=tb),
        grid=(seq // tb,),
        in_specs=[pl.BlockSpec((tb, RW_VECS, dk, nl), lambda i: (i, 0, 0, 0)),
                  pl.BlockSpec((tb, half, 1, nl), lambda i: (i, 0, 0, 0)),
                  pl.BlockSpec((tb, 2, 1, nl), lambda i: (i, 0, 0, 0)),
                  pl.BlockSpec(memory_space=pl.ANY), pl.BlockSpec(memory_space=pl.ANY)],
        out_specs=pl.BlockSpec((tb, half, 1, nl), lambda i: (i, 0, 0, 0)),
        out_shape=jax.ShapeDtypeStruct((seq, half, 1, nl), F32),
        scratch_shapes=[pltpu.VMEM((half, dk, nl), F32)],
        compiler_params=_params("arbitrary"),
        name="rwkv_seq",
    )(kw, v4, c4, after_a, after_b)


def _bias_kernel(rb_ref, bk_ref, o_ref):
    h = pl.program_id(0)
    bk = bk_ref[0]
    out = jnp.full(bk.shape, -jnp.inf, F32)
    for b in range(REL_BUCKETS):
        out = jnp.where(bk == b, rb_ref[b, h], out)
    o_ref[0, 0] = out


def _bias_tiles(rel_bias, buckets):
    bs = MOBA_BLOCK
    return pl.pallas_call(
        _bias_kernel,
        grid=(N_HEADS, 2),
        in_specs=[pl.BlockSpec(memory_space=pltpu.SMEM), pl.BlockSpec((1, bs, bs), lambda h, k: (k, 0, 0))],
        out_specs=pl.BlockSpec((1, 1, bs, bs), lambda h, k: (h, k, 0, 0)),
        out_shape=jax.ShapeDtypeStruct((N_HEADS, 2, bs, bs), F32),
        name="moba_bias",
    )(rel_bias, buckets)


def _moba_kernel(far_ref, k_ref, qt_ref, vt_ref, bias_ref, o_ref, s_ref, p_ref, *, seq, nsel):
    bs = MOBA_BLOCK
    nb = seq // bs
    h = pl.program_id(1)
    k = k_ref[...]
    qt = qt_ref[...] * (HEAD_DIM ** -0.5)
    kmean = jnp.mean(k.reshape(nb, bs, HEAD_DIM), axis=1)
    gate = _hdot(kmean, qt)
    jj = lax.broadcasted_iota(jnp.int32, (nb, seq), 0)
    qblk = lax.broadcasted_iota(jnp.int32, (nb, seq), 1) // bs
    g = jnp.where(jj < qblk, gate, -jnp.inf)
    selb = jnp.full((nb, seq), -jnp.inf, F32)
    for r in range(nsel):
        m = jnp.max(g, axis=0, keepdims=True)
        idx = jnp.min(jnp.where(g == m, jj, nb), axis=0, keepdims=True)
        hit = jj == idx
        selb = jnp.where(jnp.logical_and(hit, qblk > r), 0.0, selb)
        g = jnp.where(hit, -jnp.inf, g)
    selfar = selb + far_ref[h]
    kb = k.astype(BF16)
    qtb = qt.astype(BF16)
    vtb = vt_ref[...].astype(BF16)
    for i in range(nb):
        qs = slice(i * bs, (i + 1) * bs)
        q_i = qtb[:, qs]
        mx = None
        for j in range(i + 1):
            ks = slice(j * bs, (j + 1) * bs)
            s = jnp.dot(kb[ks, :], q_i, preferred_element_type=F32)
            if j == i:
                s = s + bias_ref[0, 0]
            elif j == i - 1:
                s = s + bias_ref[0, 1] + selb[j:j + 1, qs]
            else:
                s = s + selfar[j:j + 1, qs]
            s_ref[ks, :] = s
            mx = s if mx is None else jnp.maximum(mx, s)
        m = jnp.max(mx, axis=0, keepdims=True)
        lsum = None
        for j in range(i + 1):
            ks = slice(j * bs, (j + 1) * bs)
            p = jnp.exp(s_ref[ks, :] - m)
            p_ref[ks, :] = p.astype(BF16)
            lsum = p if lsum is None else lsum + p
        acc = jnp.dot(vtb[:, 0:(i + 1) * bs], p_ref[0:(i + 1) * bs, :], preferred_element_type=F32)
        o_ref[:, qs] = acc / jnp.sum(lsum, axis=0, keepdims=True)


def _moba(k4, qv_t, bias_tiles, far):
    bsz, _, seq, _ = k4.shape
    nsel = max(1, min(MOBA_TOPK, seq // MOBA_BLOCK - 1))
    return pl.pallas_call(
        functools.partial(_moba_kernel, seq=seq, nsel=nsel),
        grid=(bsz, N_HEADS),
        in_specs=[pl.BlockSpec(memory_space=pltpu.SMEM),
                  pl.BlockSpec((None, None, seq, HEAD_DIM), lambda b, h: (b, h, 0, 0)),
                  pl.BlockSpec((None, HEAD_DIM, seq), lambda b, h: (b, h, 0)),
                  pl.BlockSpec((None, HEAD_DIM, seq), lambda b, h: (b, N_HEADS + h, 0)),
                  pl.BlockSpec((1, 2, MOBA_BLOCK, MOBA_BLOCK), lambda b, h: (h, 0, 0, 0))],
        out_specs=pl.BlockSpec((None, HEAD_DIM, seq), lambda b, h: (b, h, 0)),
        out_shape=jax.ShapeDtypeStruct((bsz, GW, seq), F32),
        scratch_shapes=[pltpu.VMEM((seq, MOBA_BLOCK), F32), pltpu.VMEM((seq, MOBA_BLOCK), BF16)],
        compiler_params=_params("parallel", "parallel"),
        name="moba",
    )(far, k4, qv_t, qv_t, bias_tiles)


def _t5_bucket(rel):
    n = jnp.maximum(rel, 0)
    max_exact = REL_BUCKETS // 2
    nf = jnp.maximum(n, 1).astype(F32)
    large = max_exact + (jnp.log(nf / max_exact) / math.log(REL_MAX_DIST / max_exact)
                         * (REL_BUCKETS - max_exact)).astype(jnp.int32)
    return jnp.where(n < max_exact, n, jnp.minimum(large, REL_BUCKETS - 1))


def _bucket_tiles():
    kpos = jnp.arange(MOBA_BLOCK)[:, None]
    qpos = jnp.arange(MOBA_BLOCK)[None, :]
    rel = qpos - kpos
    own = jnp.where(rel >= 0, _t5_bucket(rel), -1)
    prev = _t5_bucket(rel + MOBA_BLOCK)
    return jnp.stack([own, prev]).astype(jnp.int32)


def _mixout_kernel(ys5_ref, yml_ref, yrw_ref, bonus_ref, g_ref, ymbt_ref, lnw_ref, lnb_ref, bd_ref,
                   gain_ref, w_ref, gpost_ref, x_ref, o_ref):
    bd = bd_ref[...]
    y = yrw_ref[...]
    mean = _sum3(y, bd) * (1.0 / HEAD_DIM)
    d = y - mean
    var = _sum3(d * d, bd) * (1.0 / HEAD_DIM)
    yrw = (d * lax.rsqrt(var + RW_GN_EPS) * lnw_ref[...] + lnb_ref[...] + bonus_ref[...]) * g_ref[...]
    acc = None
    for i, yg in enumerate((ys5_ref[...], yml_ref[...], yrw, ymbt_ref[...].T)):
        cols = slice(i * GW, (i + 1) * GW)
        part = jnp.dot(_rms(yg, gain_ref[:, cols]).astype(BF16), w_ref[cols, :], preferred_element_type=F32)
        acc = part if acc is None else acc + part
    o_ref[...] = x_ref[...] + _rms(acc, gpost_ref[...])


def _mixout(ys5, yml, yrw, bonus, g, ymb_t, ln_w, ln_b, bd, gain, w, gpost, x2d, seq):
    t = x2d.shape[0]
    tm = min(512, seq)
    per = seq // tm
    vec = lambda c: _const_spec((1, c))
    return pl.pallas_call(
        _mixout_kernel,
        grid=(t // tm,),
        in_specs=[_row_spec(tm, GW)] * 5
        + [pl.BlockSpec((None, GW, tm), lambda i: (i // per, 0, i % per)),
           vec(GW), vec(GW), _const_spec((GW, GW)), vec(D_MODEL),
           _const_spec((D_MODEL, D_MODEL)), vec(D_MODEL), _row_spec(tm, D_MODEL)],
        out_specs=_row_spec(tm, D_MODEL),
        out_shape=jax.ShapeDtypeStruct((t, D_MODEL), F32),
        compiler_params=_params("parallel"),
        name="mixout",
    )(ys5, yml, yrw, bonus, g, ymb_t, ln_w, ln_b, bd, gain, w, gpost, x2d)


def _kv_kernel(m_ref, g_ref, w_ref, o_ref):
    h = _rms(m_ref[0], g_ref[...]).astype(BF16)
    o_ref[0] = jnp.dot(h, w_ref[...], preferred_element_type=F32).astype(BF16)


def _kv(mem, g, w):
    bsz, m, _ = mem.shape
    return pl.pallas_call(
        _kv_kernel,
        grid=(bsz,),
        in_specs=[pl.BlockSpec((1, m, D_MODEL), lambda b: (b, 0, 0)), _const_spec((1, D_MODEL)),
                  _const_spec((D_MODEL, 2 * D_MODEL))],
        out_specs=pl.BlockSpec((1, m, 2 * D_MODEL), lambda b: (b, 0, 0)),
        out_shape=jax.ShapeDtypeStruct((bsz, m, 2 * D_MODEL), BF16),
        compiler_params=_params("parallel"),
        name="xa_kv",
    )(mem, g, w)


def _xattn_kernel(x_ref, kv_ref, gpre_ref, wq_ref, wo_ref, gpost_ref, o_ref):
    x = x_ref[...]
    q = jnp.dot(_rms(x, gpre_ref[...]).astype(BF16), wq_ref[...], preferred_element_type=F32)
    acc = None
    for hd in range(XA_HEADS):
        cols = slice(hd * XA_HEAD_DIM, (hd + 1) * XA_HEAD_DIM)
        vcols = slice(D_MODEL + hd * XA_HEAD_DIM, D_MODEL + (hd + 1) * XA_HEAD_DIM)
        s = lax.dot_general(q[:, cols].astype(BF16), kv_ref[0, :, cols], NT_DIMS,
                            preferred_element_type=F32) * (XA_HEAD_DIM ** -0.5)
        p = jnp.exp(s - jnp.max(s, axis=-1, keepdims=True))
        o = jnp.dot(p.astype(BF16), kv_ref[0, :, vcols], preferred_element_type=F32)
        o = o / jnp.sum(p, axis=-1, keepdims=True)
        part = jnp.dot(o.astype(BF16), wo_ref[cols, :], preferred_element_type=F32)
        acc = part if acc is None else acc + part
    o_ref[...] = x + _rms(acc, gpost_ref[...])


def _xattn(x2d, kv, gpre, wq, wo, gpost, seq):
    t = x2d.shape[0]
    m = kv.shape[1]
    tm = min(512, seq)
    per = seq // tm
    vec = _const_spec((1, D_MODEL))
    sq = _const_spec((D_MODEL, D_MODEL))
    return pl.pallas_call(
        _xattn_kernel,
        grid=(t // tm,),
        in_specs=[_row_spec(tm, D_MODEL), pl.BlockSpec((1, m, 2 * D_MODEL), lambda i: (i // per, 0, 0)),
                  vec, sq, sq, vec],
        out_specs=_row_spec(tm, D_MODEL),
        out_shape=jax.ShapeDtypeStruct((t, D_MODEL), F32),
        compiler_params=_params("parallel"),
        name="xattn",
    )(x2d, kv, gpre, wq, wo, gpost)


def _ffn_kernel(x_ref, gpre_ref, wi_ref, wo_ref, gpost_ref, o_ref):
    x = x_ref[...]
    h = _rms(x, gpre_ref[...]).astype(BF16)
    half = D_FF // 2
    acc = None
    for c in range(2):
        gate = jnp.dot(h, wi_ref[:, c * half:(c + 1) * half], preferred_element_type=F32)
        up = jnp.dot(h, wi_ref[:, D_FF + c * half:D_FF + (c + 1) * half], preferred_element_type=F32)
        act = (gate * _sigmoid(gate) * up).astype(BF16)
        part = jnp.dot(act, wo_ref[c * half:(c + 1) * half, :], preferred_element_type=F32)
        acc = part if acc is None else acc + part
    o_ref[...] = x + _rms(acc, gpost_ref[...])


def _ffn(x2d, gpre, wi, wo, gpost):
    t = x2d.shape[0]
    tm = min(256, t)
    vec = _const_spec((1, D_MODEL))
    once = pl.Buffered(1)
    return pl.pallas_call(
        _ffn_kernel,
        grid=(t // tm,),
        in_specs=[_row_spec(tm, D_MODEL), vec,
                  pl.BlockSpec((D_MODEL, 2 * D_FF), lambda i: (0, 0), pipeline_mode=once),
                  pl.BlockSpec((D_FF, D_MODEL), lambda i: (0, 0), pipeline_mode=once), vec],
        out_specs=_row_spec(tm, D_MODEL),
        out_shape=jax.ShapeDtypeStruct((t, D_MODEL), F32),
        compiler_params=_params("parallel"),
        name="ffn",
    )(x2d, gpre, wi, wo, gpost)


def _block_diag(blocks):
    g, a, b = blocks.shape
    eye = jnp.eye(g, dtype=blocks.dtype)
    return jnp.einsum('gab,gk->gakb', blocks, eye).reshape(g * a, g * b)


def kernel(x, mem, rel_bias, norm_pre_mix, norm_post_mix, norm_pre_xa, norm_post_xa, norm_pre_ffn,
           norm_post_ffn, norm_mem, w_in, mix_out_gain, w_out, s5_lam_re, s5_lam_im, s5_log_step,
           s5_b_re, s5_b_im, s5_c_re, s5_c_im, s5_d, s5_w_glu, s5_b_glu, ml_conv, ml_i_bias, ml_f_bias,
           rw_mu, rw_w0, rw_w2, rw_a0, rw_a2, rw_g2, rw_k_k, rw_k_a, rw_r_k, rw_ln_w, rw_ln_b,
           xa_wq, xa_wkv, xa_wo, ffn_w_in, ffn_w_out):
    bsz, seq, _ = x.shape
    depth = w_in.shape[0]
    t = bsz * seq
    row = lambda a: a.reshape(1, -1)

    ml_end = GW + 4 * GW
    rw_lo = ml_end + 2 * N_HEADS
    mb_lo = rw_lo + 4 * GW
    w_proj = jnp.concatenate(
        [w_in[:, :, :ml_end], w_in[:, :, rw_lo:mb_lo], w_in[:, :, mb_lo + GW:mb_lo + 2 * GW],
         w_in[:, :, ml_end:rw_lo], jnp.zeros((depth, D_MODEL, LANES - 2 * N_HEADS), w_in.dtype)],
        axis=2).astype(BF16)
    w_qv_t = jnp.concatenate([w_in[:, :, mb_lo:mb_lo + GW], w_in[:, :, mb_lo + 2 * GW:mb_lo + 3 * GW]],
                             axis=2).transpose(0, 2, 1).astype(BF16)
    w_out_b = w_out.astype(BF16)
    wq_b, wkv_b, wo_b = xa_wq.astype(BF16), xa_wkv.astype(BF16), xa_wo.astype(BF16)
    ffn_wi_b, ffn_wo_b = ffn_w_in.astype(BF16), ffn_w_out.astype(BF16)
    w_glu_b = s5_w_glu.astype(BF16)

    ng = depth * S5_GROUPS
    ab_re, ab_im, bb_re, bb_im = _s5_params(
        s5_lam_re.reshape(ng, S5_STATE), s5_lam_im.reshape(ng, S5_STATE), s5_log_step.reshape(ng, 1),
        s5_b_re.transpose(0, 1, 3, 2).reshape(ng, S5_GROUP, S5_STATE),
        s5_b_im.transpose(0, 1, 3, 2).reshape(ng, S5_GROUP, S5_STATE))
    ab_re = ab_re.reshape(depth, 1, S5_P)
    ab_im = ab_im.reshape(depth, 1, S5_P)
    bb_re = bb_re.reshape(depth, S5_GROUPS, S5_GROUP, S5_STATE)
    bb_im = bb_im.reshape(depth, S5_GROUPS, S5_GROUP, S5_STATE)

    bd = _block_diag(jnp.ones((N_HEADS, HEAD_DIM, HEAD_DIM), BF16))
    head_of_lane = jnp.arange(GW) // HEAD_DIM
    sela = (head_of_lane[:, None] == jnp.arange(LANES)[None, :]).astype(BF16)
    selb = (head_of_lane[:, None] + N_HEADS == jnp.arange(LANES)[None, :]).astype(BF16)
    zeros64 = jnp.zeros((HEAD_DIM, GW), F32)

    bias_tiles = _bias_tiles(rel_bias, _bucket_tiles())
    far = rel_bias[REL_BUCKETS - 1, :]

    half = HEAD_DIM // 2
    nbh = bsz * N_HEADS
    x2d = x.reshape(t, D_MODEL)
    for l in range(depth):
        u_s5, p_ml, p_rw, gates, k4, qv_t = _proj(x2d, row(norm_pre_mix[l]), w_proj[l], w_qv_t[l], bsz, seq)

        gate_bias = jnp.concatenate([ml_i_bias[l], ml_f_bias[l], jnp.zeros((LANES - 2 * N_HEADS,), F32)])
        y_ml = _mlstm(p_ml.reshape(bsz, seq, 4 * GW), gates.reshape(bsz, seq, LANES), ml_conv[l],
                      row(gate_bias)).reshape(t, GW)

        w2p = jnp.concatenate([rw_w2[l], zeros64], axis=0)
        a2p = jnp.concatenate([zeros64, rw_a2[l]], axis=0)
        kw, v_rw, c12, bonus, g_rw = _rwkv_pre(
            p_rw.reshape(bsz, seq, 4 * GW), row(rw_mu[l]), row(rw_w0[l]), w2p, row(rw_a0[l]), a2p, rw_g2[l],
            row(rw_k_k[l]), row(rw_k_a[l]), row(rw_r_k[l]), bd, sela, selb)
        kw_t = kw.reshape(2, bsz, seq, RW_VECS, N_HEADS, HEAD_DIM).transpose(2, 3, 5, 0, 1, 4)
        kw_t = kw_t.reshape(seq, RW_VECS, HEAD_DIM, 2 * nbh)
        v4 = v_rw.reshape(bsz, seq, N_HEADS, 2, half).transpose(1, 4, 3, 0, 2).reshape(seq, half, 1, 2 * nbh)
        c4 = c12[:, :, :2 * N_HEADS].reshape(bsz, seq, 2, N_HEADS).transpose(1, 2, 0, 3)
        c4 = c4.reshape(seq, 2, 1, nbh)
        c4 = jnp.concatenate([c4, c4], axis=-1)

        y_mb_t = _moba(k4, qv_t, bias_tiles, far)
        y4 = _rwkv_seq(kw_t, v4, c4, y_ml, y_mb_t)
        y_rw = y4.reshape(seq, half, 2, bsz, N_HEADS).transpose(3, 0, 4, 2, 1).reshape(t, GW)

        wb = jnp.concatenate([_block_diag(bb_re[l]), _block_diag(bb_im[l])], axis=1).astype(BF16)
        wc = jnp.concatenate([_block_diag(s5_c_re[l].transpose(0, 2, 1)),
                              -_block_diag(s5_c_im[l].transpose(0, 2, 1))], axis=0).astype(BF16)
        y_s5 = _s5(u_s5.reshape(bsz, seq, GW), wb, ab_re[l], ab_im[l], wc, row(s5_d[l]), w_glu_b[l],
                   row(s5_b_glu[l]), y4).reshape(t, GW)

        x2d = _mixout(y_s5, y_ml, y_rw, bonus.reshape(t, GW), g_rw.reshape(t, GW), y_mb_t,
                      row(rw_ln_w[l]), row(rw_ln_b[l]), bd, row(mix_out_gain[l]), w_out_b[l],
                      row(norm_post_mix[l]), x2d, seq)

        kv = _kv(mem, row(norm_mem[l]), wkv_b[l])
        x2d = _xattn(x2d, kv, row(norm_pre_xa[l]), wq_b[l], wo_b[l], row(norm_post_xa[l]), seq)
        x2d = _ffn(x2d, row(norm_pre_ffn[l]), ffn_wi_b[l], ffn_wo_b[l], row(norm_post_ffn[l]))
    return x2d.reshape(bsz, seq, D_MODEL)
```

```python
import functools
import math

import jax
import jax.numpy as jnp
from jax import lax
from jax.experimental import pallas as pl
from jax.experimental.pallas import tpu as pltpu

F32 = jnp.float32
BF16 = jnp.bfloat16
HIGHEST = lax.Precision.HIGHEST

D_MODEL = 1024
N_GROUPS = 4
GW = 256
HEAD_DIM = 64
N_HEADS = 4
S5_GROUP = 16
S5_GROUPS = 16
S5_STATE = 64
S5_P = S5_GROUPS * S5_STATE
CONV_K = 4
ML_CHUNK = 128
RW_GN_EPS = 64e-5
RW_VECS = 5
MOBA_BLOCK = 256
MOBA_TOPK = 3
REL_BUCKETS = 32
REL_MAX_DIST = 128
XA_HEADS = 4
XA_HEAD_DIM = 256
D_FF = 2816
RMS_EPS = 1e-6
LOG2E = 1.4426950408889634
LANES = 128
PROJ_COLS = 10 * GW + LANES
VMEM_LIMIT = 56 * 1024 * 1024

NT_DIMS = (((1,), (1,)), ((), ()))


def _rms(x, g):
    return x * lax.rsqrt(jnp.mean(x * x, axis=-1, keepdims=True) + RMS_EPS) * g


def _bdot(a, b):
    return jnp.dot(a.astype(BF16), b.astype(BF16), preferred_element_type=F32)


def _hdot(a, b):
    return jnp.dot(a, b, precision=HIGHEST, preferred_element_type=F32)


def _sum3(x, ones_b):
    hi = x.astype(BF16)
    r1 = x - hi.astype(F32)
    mid = r1.astype(BF16)
    lo = (r1 - mid.astype(F32)).astype(BF16)
    dot = lambda a: jnp.dot(a, ones_b, preferred_element_type=F32)
    return dot(hi) + dot(mid) + dot(lo)


def _sigmoid(x):
    return 1.0 / (1.0 + jnp.exp(-x))


def _softplus(x):
    return jnp.maximum(x, 0.0) + jnp.log1p(jnp.exp(-jnp.abs(x)))


def _params(*sem):
    return pltpu.CompilerParams(dimension_semantics=sem, vmem_limit_bytes=VMEM_LIMIT)


def _row_spec(tm, cols):
    return pl.BlockSpec((tm, cols), lambda i: (i, 0))


def _const_spec(shape):
    return pl.BlockSpec(shape, lambda *_: (0,) * len(shape))


def _proj_kernel(x_ref, g_ref, w_ref, wt_ref, s5_ref, ml_ref, rw_ref, gt_ref, k4_ref, qvt_ref):
    h = _rms(x_ref[...], g_ref[...]).astype(BF16)

    def mm(lo, hi):
        return jnp.dot(h, w_ref[:, lo:hi], preferred_element_type=F32)

    s5_ref[...] = mm(0, GW)
    ml_ref[...] = mm(GW, 5 * GW)
    rw_ref[...] = mm(5 * GW, 9 * GW)
    kk = mm(9 * GW, 10 * GW)
    for hd in range(N_HEADS):
        k4_ref[hd] = kk[:, hd * HEAD_DIM:(hd + 1) * HEAD_DIM]
    gt_ref[...] = mm(10 * GW, 10 * GW + LANES)
    qvt_ref[...] = lax.dot_general(wt_ref[...], h, NT_DIMS, preferred_element_type=F32)


def _proj(x2d, g, w, wt, bsz, seq):
    t = x2d.shape[0]
    tm = min(512, seq)
    per = seq // tm
    widths = (GW, 4 * GW, 4 * GW, LANES)
    return pl.pallas_call(
        _proj_kernel,
        grid=(t // tm,),
        in_specs=[_row_spec(tm, D_MODEL), _const_spec((1, D_MODEL)), _const_spec((D_MODEL, PROJ_COLS)),
                  _const_spec((2 * GW, D_MODEL))],
        out_specs=[_row_spec(tm, c) for c in widths]
        + [pl.BlockSpec((None, N_HEADS, tm, HEAD_DIM), lambda i: (i // per, 0, i % per, 0)),
           pl.BlockSpec((None, 2 * GW, tm), lambda i: (i // per, 0, i % per))],
        out_shape=[jax.ShapeDtypeStruct((t, c), F32) for c in widths]
        + [jax.ShapeDtypeStruct((bsz, N_HEADS, seq, HEAD_DIM), F32),
           jax.ShapeDtypeStruct((bsz, 2 * GW, seq), F32)],
        compiler_params=_params("parallel"),
        name="proj",
    )(x2d, g, w, wt)


def _s5_param_kernel(lr_ref, li_ref, ls_ref, br_ref, bi_ref, abr_ref, abi_ref, bbr_ref, bbi_ref):
    lr, li = lr_ref[...], li_ref[...]
    step = jnp.exp(ls_ref[...])
    mag = jnp.exp(lr * step)
    ang = li * step
    ab_re, ab_im = mag * jnp.cos(ang), mag * jnp.sin(ang)
    den = lr * lr + li * li
    co_re = ((ab_re - 1.0) * lr + ab_im * li) / den
    co_im = (ab_im * lr - (ab_re - 1.0) * li) / den
    abr_ref[...] = ab_re
    abi_ref[...] = ab_im
    br, bi = br_ref[...], bi_ref[...]
    bbr_ref[...] = co_re[:, None, :] * br - co_im[:, None, :] * bi
    bbi_ref[...] = co_re[:, None, :] * bi + co_im[:, None, :] * br


def _s5_params(lam_re, lam_im, log_step, b_re, b_im):
    n = lam_re.shape[0]
    return pl.pallas_call(
        _s5_param_kernel,
        out_shape=[jax.ShapeDtypeStruct((n, S5_STATE), F32)] * 2
        + [jax.ShapeDtypeStruct((n, S5_GROUP, S5_STATE), F32)] * 2,
        name="s5_params",
    )(lam_re, lam_im, log_step, b_re, b_im)


def _s5_kernel(u_ref, wb_ref, are_ref, aim_ref, wc_ref, d_ref, wg_ref, bg_ref, after_ref, o_ref, xs_ref,
               st_ref, *, tm, nb):
    del after_ref

    @pl.when(pl.program_id(0) == 0)
    def _():
        st_ref[...] = jnp.zeros_like(st_ref)

    u = jnp.swapaxes(u_ref[...], 0, 1).reshape(tm * nb, GW)
    xs_ref[...] = _bdot(u, wb_ref[...])
    ch = 256
    for c in range(S5_P // ch):
        re_cols = slice(c * ch, (c + 1) * ch)
        im_cols = slice(S5_P + c * ch, S5_P + (c + 1) * ch)
        a_re = jnp.broadcast_to(are_ref[:, re_cols], (nb, ch))
        a_im = jnp.broadcast_to(aim_ref[:, re_cols], (nb, ch))

        def body(t, carry, re_cols=re_cols, im_cols=im_cols, a_re=a_re, a_im=a_im):
            s_re, s_im = carry
            rows = pl.ds(pl.multiple_of(t * nb, nb), nb)
            n_re = a_re * s_re - a_im * s_im + xs_ref[rows, re_cols]
            n_im = a_re * s_im + a_im * s_re + xs_ref[rows, im_cols]
            xs_ref[rows, re_cols] = n_re
            xs_ref[rows, im_cols] = n_im
            return n_re, n_im

        s_re, s_im = lax.fori_loop(0, tm, body, (st_ref[:, re_cols], st_ref[:, im_cols]))
        st_ref[:, re_cols] = s_re
        st_ref[:, im_cols] = s_im

    y = _bdot(xs_ref[...], wc_ref[...]) + d_ref[...] * u
    z = jax.nn.gelu(y)
    out = z * _sigmoid(_bdot(z, wg_ref[...]) + bg_ref[...])
    o_ref[...] = jnp.swapaxes(out.reshape(tm, nb, GW), 0, 1)


def _s5(u3, wb, a_re, a_im, wc, d, wg, bg, after):
    nb, seq, _ = u3.shape
    tm = min(64, seq)
    blk = pl.BlockSpec((nb, tm, GW), lambda i: (0, i, 0))
    return pl.pallas_call(
        functools.partial(_s5_kernel, tm=tm, nb=nb),
        grid=(seq // tm,),
        in_specs=[blk, _const_spec((GW, 2 * S5_P)), _const_spec((1, S5_P)),
                  _const_spec((1, S5_P)), _const_spec((2 * S5_P, GW)), _const_spec((1, GW)),
                  _const_spec((GW, GW)), _const_spec((1, GW)), pl.BlockSpec(memory_space=pl.ANY)],
        out_specs=blk,
        out_shape=jax.ShapeDtypeStruct((nb, seq, GW), F32),
        scratch_shapes=[pltpu.VMEM((nb * tm, 2 * S5_P), F32), pltpu.VMEM((nb, 2 * S5_P), F32)],
        compiler_params=_params("arbitrary"),
        name="s5",
    )(u3, wb, a_re, a_im, wc, d, wg, bg, after)


def _mlstm_kernel(p_ref, g_ref, cw_ref, gb_ref, o_ref, pad_ref, *, seq):
    cl = ML_CHUNK
    dh = HEAD_DIM
    pad_ref[0:8, :] = jnp.zeros((8, 2 * GW), F32)
    pad_ref[8:8 + seq, :] = p_ref[0, :, 0:2 * GW]
    row = lax.broadcasted_iota(jnp.int32, (cl, cl), 0)
    col = lax.broadcasted_iota(jnp.int32, (cl, cl), 1)
    trif = jnp.where(col <= row, 1.0, 0.0).astype(F32)
    causal_t = row <= col
    cw = cw_ref[...]

    def chunk(c, carry):
        cts, n8s, ms = carry
        r0 = pl.multiple_of(c * cl, cl)
        win = pad_ref[pl.ds(r0, cl + 8), :]
        conv = cw[0:1, :] * win[5:5 + cl, :]
        for j in range(1, CONV_K):
            conv = conv + cw[j:j + 1, :] * win[5 + j:5 + j + cl, :]
        qk = conv * _sigmoid(conv)
        qt = qk[:, 0:GW].T.astype(BF16)
        kb = (qk[:, GW:2 * GW] * (dh ** -0.5)).astype(BF16)
        vt = p_ref[0, pl.ds(r0, cl), 2 * GW:3 * GW].T.astype(BF16)
        ot = p_ref[0, pl.ds(r0, cl), 3 * GW:4 * GW].T
        gc = g_ref[0, pl.ds(r0, cl), :] + gb_ref[...]
        bcum = _hdot(trif, -_softplus(-gc))
        gct = gc.T
        bct = bcum.T
        outs, cts_n, n8s_n, ms_n = [], [], [], []
        for h in range(N_HEADS):
            hs = slice(h * dh, (h + 1) * dh)
            q_t, k_h, v_t = qt[hs, :], kb[:, hs], vt[hs, :]
            colv = bcum[:, N_HEADS + h:N_HEADS + h + 1] - gc[:, h:h + 1]
            br = bct[N_HEADS + h:N_HEADS + h + 1, :]
            lir = gct[h:h + 1, :]
            m_prev = ms[h]
            dmat = jnp.where(causal_t, br - colv, -jnp.inf)
            b_inter = br + m_prev
            m_comb = jnp.maximum(b_inter, jnp.max(dmat, axis=0, keepdims=True))
            w_inter = jnp.exp(b_inter - m_comb)
            s_t = jnp.exp(dmat - m_comb) * jnp.dot(k_h, q_t, preferred_element_type=F32)
            num = (jnp.dot(v_t, s_t.astype(BF16), preferred_element_type=F32)
                   + w_inter * jnp.dot(cts[h].astype(BF16), q_t, preferred_element_type=F32))
            nq = jnp.dot(n8s[h].astype(BF16), q_t, preferred_element_type=F32)[0:1, :]
            den = jnp.sum(s_t, axis=0, keepdims=True) + w_inter * nq
            hh = num / jnp.maximum(jnp.abs(den), jnp.exp(-m_comb))
            outs.append(_sigmoid(ot[hs, :]) * hh)
            b_last = br[:, cl - 1:cl]
            w_st = b_last - br + lir
            m_in = jnp.max(w_st, axis=1, keepdims=True)
            e_row = jnp.exp(w_st - m_in)
            kv_t = jnp.dot((v_t * e_row).astype(BF16), k_h, preferred_element_type=F32)
            ks8 = jnp.dot(jnp.broadcast_to(e_row, (8, cl)).astype(BF16), k_h, preferred_element_type=F32)
            m_new = jnp.maximum(b_last + m_prev, m_in)
            a = jnp.exp(b_last + m_prev - m_new)
            e = jnp.exp(m_in - m_new)
            cts_n.append(a * cts[h] + e * kv_t)
            n8s_n.append(a * n8s[h] + e * ks8)
            ms_n.append(m_new)
        o_ref[0, pl.ds(r0, cl), :] = jnp.concatenate(outs, axis=0).T
        return tuple(cts_n), tuple(n8s_n), tuple(ms_n)

    init = (tuple(jnp.zeros((dh, dh), F32) for _ in range(N_HEADS)),
            tuple(jnp.zeros((8, dh), F32) for _ in range(N_HEADS)),
            tuple(jnp.zeros((1, 1), F32) for _ in range(N_HEADS)))
    lax.fori_loop(0, seq // cl, chunk, init)


def _mlstm(p_ml, gates, conv_w, gate_bias):
    bsz, seq, _ = p_ml.shape
    return pl.pallas_call(
        functools.partial(_mlstm_kernel, seq=seq),
        grid=(bsz,),
        in_specs=[pl.BlockSpec((1, seq, 4 * GW), lambda b: (b, 0, 0)),
                  pl.BlockSpec((1, seq, LANES), lambda b: (b, 0, 0)),
                  _const_spec((CONV_K, 2 * GW)), _const_spec((1, LANES))],
        out_specs=pl.BlockSpec((1, seq, GW), lambda b: (b, 0, 0)),
        out_shape=jax.ShapeDtypeStruct((bsz, seq, GW), F32),
        scratch_shapes=[pltpu.VMEM((seq + 8, 2 * GW), F32)],
        compiler_params=_params("parallel"),
        name="mlstm",
    )(p_ml, gates, conv_w, gate_bias)


def _rwkv_pre_kernel(p_ref, pv_ref, mu_ref, w0_ref, w2_ref, a0_ref, a2_ref, g2_ref, kk_ref, ka_ref, rk_ref,
                     bd_ref, sela_ref, selb_ref, kw_o, v_o, c_o, bonus_o, g_o):
    x = p_ref[0]
    last = jnp.where(pl.program_id(1) > 0, pv_ref[0][7:8, :], 0.0)
    rowi = lax.broadcasted_iota(jnp.int32, x.shape, 0)
    prev = jnp.where(rowi == 0, last, pltpu.roll(x, 1, 0))
    p = x + (prev - x) * mu_ref[...]
    r, k, v = p[:, 0:GW], p[:, GW:2 * GW], p[:, 2 * GW:3 * GW]
    lo = p[:, 3 * GW:3 * GW + LANES]
    g_lo = p[:, 3 * GW + LANES:4 * GW]
    w = -_softplus(-(w0_ref[...] + _bdot(jnp.tanh(lo), w2_ref[...]))) - 0.5
    decay = jnp.exp(-jnp.exp(w))
    a = _sigmoid(a0_ref[...] + _bdot(lo, a2_ref[...]))
    g_o[0] = _bdot(_sigmoid(g_lo), g2_ref[...])
    bd = bd_ref[...]
    kk = k * kk_ref[...]
    kk = kk * lax.rsqrt(_sum3(kk * kk, bd) + 1e-12)
    km = k * (1.0 + (a - 1.0) * ka_ref[...])
    alp = kk * a
    for i, val in enumerate((kk, decay * r, 1.0 - decay, alp, km)):
        for copy in range(2):
            kw_o[copy, 0, :, i * GW:(i + 1) * GW] = val.astype(BF16)
    v_o[0] = v
    c_o[0] = _sum3(alp * r, sela_ref[...]) + _sum3(km * r, selb_ref[...])
    bonus_o[0] = _sum3(r * km * rk_ref[...], bd) * v


def _rwkv_pre(p_rw, mu, w0, w2p, a0, a2p, g2, k_k, k_a, r_k, bd, sela, selb):
    bsz, seq, _ = p_rw.shape
    tr = min(256, seq)
    row = lambda c: pl.BlockSpec((1, tr, c), lambda b, i: (b, i, 0))
    prev = pl.BlockSpec((1, 8, 4 * GW), lambda b, i: (b, jnp.maximum(i * (tr // 8) - 1, 0), 0))
    vec = lambda c: _const_spec((1, c))
    widths = (GW, LANES, GW, GW)
    return pl.pallas_call(
        _rwkv_pre_kernel,
        grid=(bsz, seq // tr),
        in_specs=[row(4 * GW), prev, vec(4 * GW), vec(GW), _const_spec((LANES, GW)), vec(GW),
                  _const_spec((LANES, GW)), _const_spec((LANES, GW)), vec(GW), vec(GW), vec(GW),
                  _const_spec((GW, GW)), _const_spec((GW, LANES)), _const_spec((GW, LANES))],
        out_specs=[pl.BlockSpec((2, 1, tr, RW_VECS * GW), lambda b, i: (0, b, i, 0))]
        + [row(c) for c in widths],
        out_shape=[jax.ShapeDtypeStruct((2, bsz, seq, RW_VECS * GW), BF16)]
        + [jax.ShapeDtypeStruct((bsz, seq, c), F32) for c in widths],
        compiler_params=_params("parallel", "parallel"),
        name="rwkv_pre",
    )(p_rw, p_rw, mu, w0, w2p, a0, a2p, g2, k_k, k_a, r_k, bd, sela, selb)


def _rwkv_seq_kernel(kw_ref, v_ref, c_ref, after_a, after_b, y_ref, s_ref, *, tb):
    del after_a, after_b

    @pl.when(pl.program_id(0) == 0)
    def _():
        s_ref[...] = jnp.zeros_like(s_ref)

    def step(t, carry):
        def vec(i):
            return kw_ref[t, i].astype(F32)[None]

        s = s_ref[...]
        sa = -jnp.sum(s * vec(0), axis=1, keepdims=True)
        y0 = jnp.sum(s * vec(1), axis=1, keepdims=True)
        vv = v_ref[t]
        y_ref[t] = y0 + sa * c_ref[t, 0:1] + vv * c_ref[t, 1:2]
        s_ref[...] = s * (1.0 - vec(2)) + sa * vec(3) + vv * vec(4)
        return carry

    lax.fori_loop(0, tb, step, 0)


def _rwkv_seq(kw, v4, c4, after_a, after_b):
    seq, _, dk, nl = kw.shape
    tb = min(32, seq)
    half = HEAD_DIM // 2
    return pl.pallas_call(
        functools.partial(_rwkv_seq_kernel, tb=tb),
        grid=(seq // tb,),
        in_specs=[pl.BlockSpec((tb, RW_VECS, dk, nl), lambda i: (i, 0, 0, 0)),
                  pl.BlockSpec((tb, half, 1, nl), lambda i: (i, 0, 0, 0)),
                  pl.BlockSpec((tb, 2, 1, nl), lambda i: (i, 0, 0, 0)),
                  pl.BlockSpec(memory_space=pl.ANY), pl.BlockSpec(memory_space=pl.ANY)],
        out_specs=pl.BlockSpec((tb, half, 1, nl), lambda i: (i, 0, 0, 0)),
        out_shape=jax.ShapeDtypeStruct((seq, half, 1, nl), F32),
        scratch_shapes=[pltpu.VMEM((half, dk, nl), F32)],
        compiler_params=_params("arbitrary"),
        name="rwkv_seq",
    )(kw, v4, c4, after_a, after_b)


def _bias_kernel(rb_ref, bk_ref, o_ref):
    h = pl.program_id(0)
    bk = bk_ref[0]
    out = jnp.full(bk.shape, -jnp.inf, F32)
    for b in range(REL_BUCKETS):
        out = jnp.where(bk == b, rb_ref[b, h], out)
    o_ref[0, 0] = out * LOG2E


def _bias_tiles(rel_bias, buckets):
    bs = MOBA_BLOCK
    return pl.pallas_call(
        _bias_kernel,
        grid=(N_HEADS, 2),
        in_specs=[pl.BlockSpec(memory_space=pltpu.SMEM), pl.BlockSpec((1, bs, bs), lambda h, k: (k, 0, 0))],
        out_specs=pl.BlockSpec((1, 1, bs, bs), lambda h, k: (h, k, 0, 0)),
        out_shape=jax.ShapeDtypeStruct((N_HEADS, 2, bs, bs), F32),
        name="moba_bias",
    )(rel_bias, buckets)


def _moba_kernel(far_ref, k_ref, qt_ref, vt_ref, bias_ref, o_ref, s_ref, p_ref, *, seq, nsel):
    bs = MOBA_BLOCK
    nb = seq // bs
    h = pl.program_id(1)
    k = k_ref[...]
    qt = qt_ref[...] * (HEAD_DIM ** -0.5)
    kmean = jnp.mean(k.reshape(nb, bs, HEAD_DIM), axis=1)
    gate = _hdot(kmean, qt)
    jj = lax.broadcasted_iota(jnp.int32, (nb, seq), 0)
    qblk = lax.broadcasted_iota(jnp.int32, (nb, seq), 1) // bs
    g = jnp.where(jj < qblk, gate, -jnp.inf)
    selb = jnp.full((nb, seq), -jnp.inf, F32)
    for r in range(nsel):
        m = jnp.max(g, axis=0, keepdims=True)
        idx = jnp.min(jnp.where(g == m, jj, nb), axis=0, keepdims=True)
        hit = jj == idx
        selb = jnp.where(jnp.logical_and(hit, qblk > r), 0.0, selb)
        g = jnp.where(hit, -jnp.inf, g)
    selfar = selb + far_ref[h] * LOG2E
    kb = k.astype(BF16)
    qtb = (qt * LOG2E).astype(BF16)
    vtb = vt_ref[...].astype(BF16)
    for i in range(nb):
        qs = slice(i * bs, (i + 1) * bs)
        q_i = qtb[:, qs]
        mx = None
        for j in range(i + 1):
            ks = slice(j * bs, (j + 1) * bs)
            s = jnp.dot(kb[ks, :], q_i, preferred_element_type=F32)
            if j == i:
                s = s + bias_ref[0, 0]
            elif j == i - 1:
                s = s + bias_ref[0, 1] + selb[j:j + 1, qs]
            else:
                s = s + selfar[j:j + 1, qs]
            s_ref[ks, :] = s
            mx = s if mx is None else jnp.maximum(mx, s)
        m = jnp.max(mx, axis=0, keepdims=True)
        lsum = None
        for j in range(i + 1):
            ks = slice(j * bs, (j + 1) * bs)
            p = jnp.exp2(s_ref[ks, :] - m)
            p_ref[ks, :] = p.astype(BF16)
            lsum = p if lsum is None else lsum + p
        acc = jnp.dot(vtb[:, 0:(i + 1) * bs], p_ref[0:(i + 1) * bs, :], preferred_element_type=F32)
        o_ref[:, qs] = acc / jnp.sum(lsum, axis=0, keepdims=True)


def _moba(k4, qv_t, bias_tiles, far):
    bsz, _, seq, _ = k4.shape
    nsel = max(1, min(MOBA_TOPK, seq // MOBA_BLOCK - 1))
    return pl.pallas_call(
        functools.partial(_moba_kernel, seq=seq, nsel=nsel),
        grid=(bsz, N_HEADS),
        in_specs=[pl.BlockSpec(memory_space=pltpu.SMEM),
                  pl.BlockSpec((None, None, seq, HEAD_DIM), lambda b, h: (b, h, 0, 0)),
                  pl.BlockSpec((None, HEAD_DIM, seq), lambda b, h: (b, h, 0)),
                  pl.BlockSpec((None, HEAD_DIM, seq), lambda b, h: (b, N_HEADS + h, 0)),
                  pl.BlockSpec((1, 2, MOBA_BLOCK, MOBA_BLOCK), lambda b, h: (h, 0, 0, 0))],
        out_specs=pl.BlockSpec((None, HEAD_DIM, seq), lambda b, h: (b, h, 0)),
        out_shape=jax.ShapeDtypeStruct((bsz, GW, seq), F32),
        scratch_shapes=[pltpu.VMEM((seq, MOBA_BLOCK), F32), pltpu.VMEM((seq, MOBA_BLOCK), BF16)],
        compiler_params=_params("parallel", "parallel"),
        name="moba",
    )(far, k4, qv_t, qv_t, bias_tiles)


def _t5_bucket(rel):
    n = jnp.maximum(rel, 0)
    max_exact = REL_BUCKETS // 2
    nf = jnp.maximum(n, 1).astype(F32)
    large = max_exact + (jnp.log(nf / max_exact) / math.log(REL_MAX_DIST / max_exact)
                         * (REL_BUCKETS - max_exact)).astype(jnp.int32)
    return jnp.where(n < max_exact, n, jnp.minimum(large, REL_BUCKETS - 1))


def _bucket_tiles():
    kpos = jnp.arange(MOBA_BLOCK)[:, None]
    qpos = jnp.arange(MOBA_BLOCK)[None, :]
    rel = qpos - kpos
    own = jnp.where(rel >= 0, _t5_bucket(rel), -1)
    prev = _t5_bucket(rel + MOBA_BLOCK)
    return jnp.stack([own, prev]).astype(jnp.int32)


def _mixout_kernel(ys5_ref, yml_ref, yrw_ref, bonus_ref, g_ref, ymbt_ref, lnw_ref, lnb_ref, bd_ref,
                   gain_ref, w_ref, gpost_ref, x_ref, o_ref):
    bd = bd_ref[...]
    y = yrw_ref[...]
    mean = _sum3(y, bd) * (1.0 / HEAD_DIM)
    d = y - mean
    var = _sum3(d * d, bd) * (1.0 / HEAD_DIM)
    yrw = (d * lax.rsqrt(var + RW_GN_EPS) * lnw_ref[...] + lnb_ref[...] + bonus_ref[...]) * g_ref[...]
    acc = None
    for i, yg in enumerate((ys5_ref[...], yml_ref[...], yrw, ymbt_ref[...].T)):
        cols = slice(i * GW, (i + 1) * GW)
        part = jnp.dot(_rms(yg, gain_ref[:, cols]).astype(BF16), w_ref[cols, :], preferred_element_type=F32)
        acc = part if acc is None else acc + part
    o_ref[...] = x_ref[...] + _rms(acc, gpost_ref[...])


def _mixout(ys5, yml, yrw, bonus, g, ymb_t, ln_w, ln_b, bd, gain, w, gpost, x2d, seq):
    t = x2d.shape[0]
    tm = min(512, seq)
    per = seq // tm
    vec = lambda c: _const_spec((1, c))
    return pl.pallas_call(
        _mixout_kernel,
        grid=(t // tm,),
        in_specs=[_row_spec(tm, GW)] * 5
        + [pl.BlockSpec((None, GW, tm), lambda i: (i // per, 0, i % per)),
           vec(GW), vec(GW), _const_spec((GW, GW)), vec(D_MODEL),
           _const_spec((D_MODEL, D_MODEL)), vec(D_MODEL), _row_spec(tm, D_MODEL)],
        out_specs=_row_spec(tm, D_MODEL),
        out_shape=jax.ShapeDtypeStruct((t, D_MODEL), F32),
        compiler_params=_params("parallel"),
        name="mixout",
    )(ys5, yml, yrw, bonus, g, ymb_t, ln_w, ln_b, bd, gain, w, gpost, x2d)


def _kv_kernel(m_ref, g_ref, w_ref, o_ref):
    h = _rms(m_ref[0], g_ref[...]).astype(BF16)
    o_ref[0] = jnp.dot(h, w_ref[...], preferred_element_type=F32).astype(BF16)


def _kv(mem, g, w):
    bsz, m, _ = mem.shape
    return pl.pallas_call(
        _kv_kernel,
        grid=(bsz,),
        in_specs=[pl.BlockSpec((1, m, D_MODEL), lambda b: (b, 0, 0)), _const_spec((1, D_MODEL)),
                  _const_spec((D_MODEL, 2 * D_MODEL))],
        out_specs=pl.BlockSpec((1, m, 2 * D_MODEL), lambda b: (b, 0, 0)),
        out_shape=jax.ShapeDtypeStruct((bsz, m, 2 * D_MODEL), BF16),
        compiler_params=_params("parallel"),
        name="xa_kv",
    )(mem, g, w)


def _xattn_kernel(x_ref, kv_ref, gpre_ref, wq_ref, wo_ref, gpost_ref, o_ref):
    x = x_ref[...]
    q = jnp.dot(_rms(x, gpre_ref[...]).astype(BF16), wq_ref[...], preferred_element_type=F32)
    acc = None
    for hd in range(XA_HEADS):
        cols = slice(hd * XA_HEAD_DIM, (hd + 1) * XA_HEAD_DIM)
        vcols = slice(D_MODEL + hd * XA_HEAD_DIM, D_MODEL + (hd + 1) * XA_HEAD_DIM)
        s = lax.dot_general(q[:, cols].astype(BF16), kv_ref[0, :, cols], NT_DIMS,
                            preferred_element_type=F32) * (XA_HEAD_DIM ** -0.5)
        p = jnp.exp(s - jnp.max(s, axis=-1, keepdims=True))
        o = jnp.dot(p.astype(BF16), kv_ref[0, :, vcols], preferred_element_type=F32)
        o = o / jnp.sum(p, axis=-1, keepdims=True)
        part = jnp.dot(o.astype(BF16), wo_ref[cols, :], preferred_element_type=F32)
        acc = part if acc is None else acc + part
    o_ref[...] = x + _rms(acc, gpost_ref[...])


def _xattn(x2d, kv, gpre, wq, wo, gpost, seq):
    t = x2d.shape[0]
    m = kv.shape[1]
    tm = min(512, seq)
    per = seq // tm
    vec = _const_spec((1, D_MODEL))
    sq = _const_spec((D_MODEL, D_MODEL))
    return pl.pallas_call(
        _xattn_kernel,
        grid=(t // tm,),
        in_specs=[_row_spec(tm, D_MODEL), pl.BlockSpec((1, m, 2 * D_MODEL), lambda i: (i // per, 0, 0)),
                  vec, sq, sq, vec],
        out_specs=_row_spec(tm, D_MODEL),
        out_shape=jax.ShapeDtypeStruct((t, D_MODEL), F32),
        compiler_params=_params("parallel"),
        name="xattn",
    )(x2d, kv, gpre, wq, wo, gpost)


def _ffn_kernel(x_ref, gpre_ref, wi_ref, wo_ref, gpost_ref, o_ref):
    x = x_ref[...]
    h = _rms(x, gpre_ref[...]).astype(BF16)
    half = D_FF // 2
    acc = None
    for c in range(2):
        gate = jnp.dot(h, wi_ref[:, c * half:(c + 1) * half], preferred_element_type=F32)
        up = jnp.dot(h, wi_ref[:, D_FF + c * half:D_FF + (c + 1) * half], preferred_element_type=F32)
        act = (gate * _sigmoid(gate) * up).astype(BF16)
        part = jnp.dot(act, wo_ref[c * half:(c + 1) * half, :], preferred_element_type=F32)
        acc = part if acc is None else acc + part
    o_ref[...] = x + _rms(acc, gpost_ref[...])


def _ffn(x2d, gpre, wi, wo, gpost):
    t = x2d.shape[0]
    tm = min(512, t)
    vec = _const_spec((1, D_MODEL))
    once = pl.Buffered(1)
    return pl.pallas_call(
        _ffn_kernel,
        grid=(t // tm,),
        in_specs=[_row_spec(tm, D_MODEL), vec,
                  pl.BlockSpec((D_MODEL, 2 * D_FF), lambda i: (0, 0), pipeline_mode=once),
                  pl.BlockSpec((D_FF, D_MODEL), lambda i: (0, 0), pipeline_mode=once), vec],
        out_specs=_row_spec(tm, D_MODEL),
        out_shape=jax.ShapeDtypeStruct((t, D_MODEL), F32),
        compiler_params=_params("parallel"),
        name="ffn",
    )(x2d, gpre, wi, wo, gpost)


def _block_diag(blocks):
    g, a, b = blocks.shape
    eye = jnp.eye(g, dtype=blocks.dtype)
    return jnp.einsum('gab,gk->gakb', blocks, eye).reshape(g * a, g * b)


def kernel(x, mem, rel_bias, norm_pre_mix, norm_post_mix, norm_pre_xa, norm_post_xa, norm_pre_ffn,
           norm_post_ffn, norm_mem, w_in, mix_out_gain, w_out, s5_lam_re, s5_lam_im, s5_log_step,
           s5_b_re, s5_b_im, s5_c_re, s5_c_im, s5_d, s5_w_glu, s5_b_glu, ml_conv, ml_i_bias, ml_f_bias,
           rw_mu, rw_w0, rw_w2, rw_a0, rw_a2, rw_g2, rw_k_k, rw_k_a, rw_r_k, rw_ln_w, rw_ln_b,
           xa_wq, xa_wkv, xa_wo, ffn_w_in, ffn_w_out):
    bsz, seq, _ = x.shape
    depth = w_in.shape[0]
    t = bsz * seq
    row = lambda a: a.reshape(1, -1)

    ml_end = GW + 4 * GW
    rw_lo = ml_end + 2 * N_HEADS
    mb_lo = rw_lo + 4 * GW
    w_proj = jnp.concatenate(
        [w_in[:, :, :ml_end], w_in[:, :, rw_lo:mb_lo], w_in[:, :, mb_lo + GW:mb_lo + 2 * GW],
         w_in[:, :, ml_end:rw_lo], jnp.zeros((depth, D_MODEL, LANES - 2 * N_HEADS), w_in.dtype)],
        axis=2).astype(BF16)
    w_qv_t = jnp.concatenate([w_in[:, :, mb_lo:mb_lo + GW], w_in[:, :, mb_lo + 2 * GW:mb_lo + 3 * GW]],
                             axis=2).transpose(0, 2, 1).astype(BF16)
    w_out_b = w_out.astype(BF16)
    wq_b, wkv_b, wo_b = xa_wq.astype(BF16), xa_wkv.astype(BF16), xa_wo.astype(BF16)
    ffn_wi_b, ffn_wo_b = ffn_w_in.astype(BF16), ffn_w_out.astype(BF16)
    w_glu_b = s5_w_glu.astype(BF16)

    ng = depth * S5_GROUPS
    ab_re, ab_im, bb_re, bb_im = _s5_params(
        s5_lam_re.reshape(ng, S5_STATE), s5_lam_im.reshape(ng, S5_STATE), s5_log_step.reshape(ng, 1),
        s5_b_re.transpose(0, 1, 3, 2).reshape(ng, S5_GROUP, S5_STATE),
        s5_b_im.transpose(0, 1, 3, 2).reshape(ng, S5_GROUP, S5_STATE))
    ab_re = ab_re.reshape(depth, 1, S5_P)
    ab_im = ab_im.reshape(depth, 1, S5_P)
    bb_re = bb_re.reshape(depth, S5_GROUPS, S5_GROUP, S5_STATE)
    bb_im = bb_im.reshape(depth, S5_GROUPS, S5_GROUP, S5_STATE)

    bd = _block_diag(jnp.ones((N_HEADS, HEAD_DIM, HEAD_DIM), BF16))
    head_of_lane = jnp.arange(GW) // HEAD_DIM
    sela = (head_of_lane[:, None] == jnp.arange(LANES)[None, :]).astype(BF16)
    selb = (head_of_lane[:, None] + N_HEADS == jnp.arange(LANES)[None, :]).astype(BF16)
    zeros64 = jnp.zeros((HEAD_DIM, GW), F32)

    bias_tiles = _bias_tiles(rel_bias, _bucket_tiles())
    far = rel_bias[REL_BUCKETS - 1, :]

    half = HEAD_DIM // 2
    nbh = bsz * N_HEADS
    x2d = x.reshape(t, D_MODEL)
    for l in range(depth):
        u_s5, p_ml, p_rw, gates, k4, qv_t = _proj(x2d, row(norm_pre_mix[l]), w_proj[l], w_qv_t[l], bsz, seq)

        gate_bias = jnp.concatenate([ml_i_bias[l], ml_f_bias[l], jnp.zeros((LANES - 2 * N_HEADS,), F32)])
        y_ml = _mlstm(p_ml.reshape(bsz, seq, 4 * GW), gates.reshape(bsz, seq, LANES), ml_conv[l],
                      row(gate_bias)).reshape(t, GW)

        w2p = jnp.concatenate([rw_w2[l], zeros64], axis=0)
        a2p = jnp.concatenate([zeros64, rw_a2[l]], axis=0)
        kw, v_rw, c12, bonus, g_rw = _rwkv_pre(
            p_rw.reshape(bsz, seq, 4 * GW), row(rw_mu[l]), row(rw_w0[l]), w2p, row(rw_a0[l]), a2p, rw_g2[l],
            row(rw_k_k[l]), row(rw_k_a[l]), row(rw_r_k[l]), bd, sela, selb)
        kw_t = kw.reshape(2, bsz, seq, RW_VECS, N_HEADS, HEAD_DIM).transpose(2, 3, 5, 0, 1, 4)
        kw_t = kw_t.reshape(seq, RW_VECS, HEAD_DIM, 2 * nbh)
        v4 = v_rw.reshape(bsz, seq, N_HEADS, 2, half).transpose(1, 4, 3, 0, 2).reshape(seq, half, 1, 2 * nbh)
        c4 = c12[:, :, :2 * N_HEADS].reshape(bsz, seq, 2, N_HEADS).transpose(1, 2, 0, 3)
        c4 = c4.reshape(seq, 2, 1, nbh)
        c4 = jnp.concatenate([c4, c4], axis=-1)

        y_mb_t = _moba(k4, qv_t, bias_tiles, far)
        y4 = _rwkv_seq(kw_t, v4, c4, y_ml, y_mb_t)
        y_rw = y4.reshape(seq, half, 2, bsz, N_HEADS).transpose(3, 0, 4, 2, 1).reshape(t, GW)

        wb = jnp.concatenate([_block_diag(bb_re[l]), _block_diag(bb_im[l])], axis=1).astype(BF16)
        wc = jnp.concatenate([_block_diag(s5_c_re[l].transpose(0, 2, 1)),
                              -_block_diag(s5_c_im[l].transpose(0, 2, 1))], axis=0).astype(BF16)
        y_s5 = _s5(u_s5.reshape(bsz, seq, GW), wb, ab_re[l], ab_im[l], wc, row(s5_d[l]), w_glu_b[l],
                   row(s5_b_glu[l]), y4).reshape(t, GW)

        x2d = _mixout(y_s5, y_ml, y_rw, bonus.reshape(t, GW), g_rw.reshape(t, GW), y_mb_t,
                      row(rw_ln_w[l]), row(rw_ln_b[l]), bd, row(mix_out_gain[l]), w_out_b[l],
                      row(norm_post_mix[l]), x2d, seq)

        kv = _kv(mem, row(norm_mem[l]), wkv_b[l])
        x2d = _xattn(x2d, kv, row(norm_pre_xa[l]), wq_b[l], wo_b[l], row(norm_post_xa[l]), seq)
        x2d = _ffn(x2d, row(norm_pre_ffn[l]), ffn_wi_b[l], ffn_wo_b[l], row(norm_post_ffn[l]))
    return x2d.reshape(bsz, seq, D_MODEL)
```

```python
import functools
import math

import jax
import jax.numpy as jnp
import numpy as np
from jax import lax
from jax.experimental import pallas as pl
from jax.experimental.pallas import tpu as pltpu

F32 = jnp.float32
BF16 = jnp.bfloat16
HIGHEST = lax.Precision.HIGHEST

D_MODEL = 1024
N_GROUPS = 4
GW = 256
HEAD_DIM = 64
N_HEADS = 4
S5_GROUP = 16
S5_GROUPS = 16
S5_STATE = 64
S5_P = S5_GROUPS * S5_STATE
CONV_K = 4
ML_CHUNK = 128
RW_GN_EPS = 64e-5
RW_VECS = 5
MOBA_BLOCK = 256
MOBA_TOPK = 3
REL_BUCKETS = 32
REL_MAX_DIST = 128
XA_HEADS = 4
XA_HEAD_DIM = 256
D_FF = 2816
RMS_EPS = 1e-6
LOG2E = 1.4426950408889634
LANES = 128
RW_COLS = 5 * GW
PROJ_COLS = 6 * GW + RW_COLS + LANES
VMEM_LIMIT = 56 * 1024 * 1024

NT_DIMS = (((1,), (1,)), ((), ()))


def _rms(x, g):
    return x * lax.rsqrt(jnp.mean(x * x, axis=-1, keepdims=True) + RMS_EPS) * g


def _bdot(a, b):
    return jnp.dot(a.astype(BF16), b.astype(BF16), preferred_element_type=F32)


def _hdot(a, b):
    return jnp.dot(a, b, precision=HIGHEST, preferred_element_type=F32)


def _sum3(x, ones_b):
    hi = x.astype(BF16)
    r1 = x - hi.astype(F32)
    mid = r1.astype(BF16)
    lo = (r1 - mid.astype(F32)).astype(BF16)
    dot = lambda a: jnp.dot(a, ones_b, preferred_element_type=F32)
    return dot(hi) + dot(mid) + dot(lo)


def _sigmoid(x):
    return 1.0 / (1.0 + jnp.exp(-x))


def _softplus(x):
    return jnp.maximum(x, 0.0) + jnp.log1p(jnp.exp(-jnp.abs(x)))


def _params(*sem):
    return pltpu.CompilerParams(dimension_semantics=sem, vmem_limit_bytes=VMEM_LIMIT)


def _row_spec(tm, cols):
    return pl.BlockSpec((tm, cols), lambda i: (i, 0))


def _const_spec(shape):
    return pl.BlockSpec(shape, lambda *_: (0,) * len(shape))


def _proj_kernel(x_ref, g_ref, w_ref, wt_ref, s5_ref, ml_ref, rw_ref, gt_ref, k4_ref, qvt_ref):
    h = _rms(x_ref[...], g_ref[...]).astype(BF16)

    def mm(lo, hi):
        return jnp.dot(h, w_ref[:, lo:hi], preferred_element_type=F32)

    s5_ref[...] = mm(0, GW)
    ml_ref[...] = mm(GW, 5 * GW)
    rw_ref[...] = mm(5 * GW, 5 * GW + RW_COLS)
    kk = mm(5 * GW + RW_COLS, 6 * GW + RW_COLS)
    for hd in range(N_HEADS):
        k4_ref[hd] = kk[:, hd * HEAD_DIM:(hd + 1) * HEAD_DIM]
    gt_ref[...] = mm(6 * GW + RW_COLS, PROJ_COLS)
    qvt_ref[...] = lax.dot_general(wt_ref[...], h, NT_DIMS, preferred_element_type=F32)


def _proj(x2d, g, w, wt, bsz, seq):
    t = x2d.shape[0]
    tm = min(512, seq)
    per = seq // tm
    widths = (GW, 4 * GW, RW_COLS, LANES)
    return pl.pallas_call(
        _proj_kernel,
        grid=(t // tm,),
        in_specs=[_row_spec(tm, D_MODEL), _const_spec((1, D_MODEL)), _const_spec((D_MODEL, PROJ_COLS)),
                  _const_spec((2 * GW, D_MODEL))],
        out_specs=[_row_spec(tm, c) for c in widths]
        + [pl.BlockSpec((None, N_HEADS, tm, HEAD_DIM), lambda i: (i // per, 0, i % per, 0)),
           pl.BlockSpec((None, 2 * GW, tm), lambda i: (i // per, 0, i % per))],
        out_shape=[jax.ShapeDtypeStruct((t, c), F32) for c in widths]
        + [jax.ShapeDtypeStruct((bsz, N_HEADS, seq, HEAD_DIM), F32),
           jax.ShapeDtypeStruct((bsz, 2 * GW, seq), F32)],
        compiler_params=_params("parallel"),
        name="proj",
    )(x2d, g, w, wt)


def _s5_param_kernel(lr_ref, li_ref, ls_ref, br_ref, bi_ref, abr_ref, abi_ref, bbr_ref, bbi_ref):
    lr, li = lr_ref[...], li_ref[...]
    step = jnp.exp(ls_ref[...])
    mag = jnp.exp(lr * step)
    ang = li * step
    ab_re, ab_im = mag * jnp.cos(ang), mag * jnp.sin(ang)
    den = lr * lr + li * li
    co_re = ((ab_re - 1.0) * lr + ab_im * li) / den
    co_im = (ab_im * lr - (ab_re - 1.0) * li) / den
    abr_ref[...] = ab_re
    abi_ref[...] = ab_im
    br, bi = br_ref[...], bi_ref[...]
    bbr_ref[...] = co_re[:, None, :] * br - co_im[:, None, :] * bi
    bbi_ref[...] = co_re[:, None, :] * bi + co_im[:, None, :] * br


def _s5_params(lam_re, lam_im, log_step, b_re, b_im):
    n = lam_re.shape[0]
    return pl.pallas_call(
        _s5_param_kernel,
        out_shape=[jax.ShapeDtypeStruct((n, S5_STATE), F32)] * 2
        + [jax.ShapeDtypeStruct((n, S5_GROUP, S5_STATE), F32)] * 2,
        name="s5_params",
    )(lam_re, lam_im, log_step, b_re, b_im)


def _s5_kernel(u_ref, wb_ref, are_ref, aim_ref, wc_ref, d_ref, wg_ref, bg_ref, o_ref, xs_ref, st_ref,
               *, tm, nb):
    @pl.when(pl.program_id(0) == 0)
    def _():
        st_ref[...] = jnp.zeros_like(st_ref)

    u = jnp.swapaxes(u_ref[...], 0, 1).reshape(tm * nb, GW)
    xs_ref[...] = _bdot(u, wb_ref[...])
    ch = 256
    for c in range(S5_P // ch):
        re_cols = slice(c * ch, (c + 1) * ch)
        im_cols = slice(S5_P + c * ch, S5_P + (c + 1) * ch)
        a_re = jnp.broadcast_to(are_ref[:, re_cols], (nb, ch))
        a_im = jnp.broadcast_to(aim_ref[:, re_cols], (nb, ch))

        def body(t, carry, re_cols=re_cols, im_cols=im_cols, a_re=a_re, a_im=a_im):
            s_re, s_im = carry
            rows = pl.ds(pl.multiple_of(t * nb, nb), nb)
            n_re = a_re * s_re - a_im * s_im + xs_ref[rows, re_cols]
            n_im = a_re * s_im + a_im * s_re + xs_ref[rows, im_cols]
            xs_ref[rows, re_cols] = n_re
            xs_ref[rows, im_cols] = n_im
            return n_re, n_im

        s_re, s_im = lax.fori_loop(0, tm, body, (st_ref[:, re_cols], st_ref[:, im_cols]))
        st_ref[:, re_cols] = s_re
        st_ref[:, im_cols] = s_im

    y = _bdot(xs_ref[...], wc_ref[...]) + d_ref[...] * u
    z = jax.nn.gelu(y)
    out = z * _sigmoid(_bdot(z, wg_ref[...]) + bg_ref[...])
    o_ref[...] = jnp.swapaxes(out.reshape(tm, nb, GW), 0, 1)


def _s5(u3, wb, a_re, a_im, wc, d, wg, bg):
    nb, seq, _ = u3.shape
    tm = min(64, seq)
    blk = pl.BlockSpec((nb, tm, GW), lambda i: (0, i, 0))
    return pl.pallas_call(
        functools.partial(_s5_kernel, tm=tm, nb=nb),
        grid=(seq // tm,),
        in_specs=[blk, _const_spec((GW, 2 * S5_P)), _const_spec((1, S5_P)),
                  _const_spec((1, S5_P)), _const_spec((2 * S5_P, GW)), _const_spec((1, GW)),
                  _const_spec((GW, GW)), _const_spec((1, GW))],
        out_specs=blk,
        out_shape=jax.ShapeDtypeStruct((nb, seq, GW), F32),
        scratch_shapes=[pltpu.VMEM((nb * tm, 2 * S5_P), F32), pltpu.VMEM((nb, 2 * S5_P), F32)],
        compiler_params=_params("arbitrary"),
        name="s5",
    )(u3, wb, a_re, a_im, wc, d, wg, bg)


def _mlstm_kernel(p_ref, g_ref, cw_ref, gb_ref, o_ref, pad_ref, *, seq):
    cl = ML_CHUNK
    dh = HEAD_DIM
    pad_ref[0:8, :] = jnp.zeros((8, 2 * GW), F32)
    pad_ref[8:8 + seq, :] = p_ref[0, :, 0:2 * GW]
    row = lax.broadcasted_iota(jnp.int32, (cl, cl), 0)
    col = lax.broadcasted_iota(jnp.int32, (cl, cl), 1)
    trif = jnp.where(col <= row, 1.0, 0.0).astype(F32)
    causal_t = row <= col
    cw = cw_ref[...]

    def chunk(c, carry):
        cts, n8s, ms = carry
        r0 = pl.multiple_of(c * cl, cl)
        win = pad_ref[pl.ds(r0, cl + 8), :]
        conv = cw[0:1, :] * win[5:5 + cl, :]
        for j in range(1, CONV_K):
            conv = conv + cw[j:j + 1, :] * win[5 + j:5 + j + cl, :]
        qk = conv * _sigmoid(conv)
        qt = qk[:, 0:GW].T.astype(BF16)
        kb = (qk[:, GW:2 * GW] * (dh ** -0.5)).astype(BF16)
        vt = p_ref[0, pl.ds(r0, cl), 2 * GW:3 * GW].T.astype(BF16)
        ot = p_ref[0, pl.ds(r0, cl), 3 * GW:4 * GW].T
        gc = g_ref[0, pl.ds(r0, cl), :] + gb_ref[...]
        bcum = _hdot(trif, -_softplus(-gc))
        gct = gc.T
        bct = bcum.T
        outs, cts_n, n8s_n, ms_n = [], [], [], []
        for h in range(N_HEADS):
            hs = slice(h * dh, (h + 1) * dh)
            q_t, k_h, v_t = qt[hs, :], kb[:, hs], vt[hs, :]
            colv = bcum[:, N_HEADS + h:N_HEADS + h + 1] - gc[:, h:h + 1]
            br = bct[N_HEADS + h:N_HEADS + h + 1, :]
            lir = gct[h:h + 1, :]
            m_prev = ms[h]
            dmat = jnp.where(causal_t, br - colv, -jnp.inf)
            b_inter = br + m_prev
            m_comb = jnp.maximum(b_inter, jnp.max(dmat, axis=0, keepdims=True))
            w_inter = jnp.exp(b_inter - m_comb)
            s_t = jnp.exp(dmat - m_comb) * jnp.dot(k_h, q_t, preferred_element_type=F32)
            num = (jnp.dot(v_t, s_t.astype(BF16), preferred_element_type=F32)
                   + w_inter * jnp.dot(cts[h].astype(BF16), q_t, preferred_element_type=F32))
            nq = jnp.dot(n8s[h].astype(BF16), q_t, preferred_element_type=F32)[0:1, :]
            den = jnp.sum(s_t, axis=0, keepdims=True) + w_inter * nq
            hh = num / jnp.maximum(jnp.abs(den), jnp.exp(-m_comb))
            outs.append(_sigmoid(ot[hs, :]) * hh)
            b_last = br[:, cl - 1:cl]
            w_st = b_last - br + lir
            m_in = jnp.max(w_st, axis=1, keepdims=True)
            e_row = jnp.exp(w_st - m_in)
            kv_t = jnp.dot((v_t * e_row).astype(BF16), k_h, preferred_element_type=F32)
            ks8 = jnp.dot(jnp.broadcast_to(e_row, (8, cl)).astype(BF16), k_h, preferred_element_type=F32)
            m_new = jnp.maximum(b_last + m_prev, m_in)
            a = jnp.exp(b_last + m_prev - m_new)
            e = jnp.exp(m_in - m_new)
            cts_n.append(a * cts[h] + e * kv_t)
            n8s_n.append(a * n8s[h] + e * ks8)
            ms_n.append(m_new)
        o_ref[0, pl.ds(r0, cl), :] = jnp.concatenate(outs, axis=0).T
        return tuple(cts_n), tuple(n8s_n), tuple(ms_n)

    init = (tuple(jnp.zeros((dh, dh), F32) for _ in range(N_HEADS)),
            tuple(jnp.zeros((8, dh), F32) for _ in range(N_HEADS)),
            tuple(jnp.zeros((1, 1), F32) for _ in range(N_HEADS)))
    lax.fori_loop(0, seq // cl, chunk, init)


def _mlstm(p_ml, gates, conv_w, gate_bias):
    bsz, seq, _ = p_ml.shape
    return pl.pallas_call(
        functools.partial(_mlstm_kernel, seq=seq),
        grid=(bsz,),
        in_specs=[pl.BlockSpec((1, seq, 4 * GW), lambda b: (b, 0, 0)),
                  pl.BlockSpec((1, seq, LANES), lambda b: (b, 0, 0)),
                  _const_spec((CONV_K, 2 * GW)), _const_spec((1, LANES))],
        out_specs=pl.BlockSpec((1, seq, GW), lambda b: (b, 0, 0)),
        out_shape=jax.ShapeDtypeStruct((bsz, seq, GW), F32),
        scratch_shapes=[pltpu.VMEM((seq + 8, 2 * GW), F32)],
        compiler_params=_params("parallel"),
        name="mlstm",
    )(p_ml, gates, conv_w, gate_bias)


def _rwkv_pre_kernel(p_ref, pv_ref, mu_ref, w0_ref, w2_ref, a0_ref, a2_ref, g2_ref, kk_ref, ka_ref, rk_ref,
                     bd_ref, bdp_ref, sela_ref, selb_ref, kw_o, v_o, c_o, bonus_o, g_o):
    x = p_ref[0]
    last = jnp.where(pl.program_id(1) > 0, pv_ref[0][7:8, :], 0.0)
    rowi = lax.broadcasted_iota(jnp.int32, x.shape, 0)
    prev = jnp.where(rowi == 0, last, pltpu.roll(x, 1, 0))
    p = x + (prev - x) * mu_ref[...]
    r, k = p[:, 0:GW], p[:, GW:2 * GW]
    lo = p[:, 3 * GW:3 * GW + LANES]
    g_lo = p[:, 3 * GW + LANES:4 * GW]
    vp = p[:, 4 * GW:5 * GW]
    w = -_softplus(-(w0_ref[...] + _bdot(jnp.tanh(lo), w2_ref[...]))) - 0.5
    decay = jnp.exp(-jnp.exp(w))
    a = _sigmoid(a0_ref[...] + _bdot(lo, a2_ref[...]))
    g_o[0] = _bdot(_sigmoid(g_lo), g2_ref[...])
    kk = k * kk_ref[...]
    kk = kk * lax.rsqrt(_sum3(kk * kk, bd_ref[...]) + 1e-12)
    km = k * (1.0 + (a - 1.0) * ka_ref[...])
    alp = kk * a
    for i, val in enumerate((kk, decay * r, decay, alp, km)):
        kw_o[0, :, i * GW:(i + 1) * GW] = val
    v_o[0] = vp
    c_o[0] = _sum3(alp * r, sela_ref[...]) + _sum3(km * r, selb_ref[...])
    bonus_o[0] = _sum3(r * km * rk_ref[...], bdp_ref[...]) * vp


def _rwkv_pre(p_rw, mu, w0, w2p, a0, a2p, g2p, k_k, k_a, r_k, bd, bdp, sela, selb):
    bsz, seq, _ = p_rw.shape
    tr = min(256, seq)
    row = lambda c: pl.BlockSpec((1, tr, c), lambda b, i: (b, i, 0))
    prev = pl.BlockSpec((1, 8, RW_COLS), lambda b, i: (b, jnp.maximum(i * (tr // 8) - 1, 0), 0))
    vec = lambda c: _const_spec((1, c))
    widths = (RW_VECS * GW, GW, LANES, GW, GW)
    return pl.pallas_call(
        _rwkv_pre_kernel,
        grid=(bsz, seq // tr),
        in_specs=[row(RW_COLS), prev, vec(RW_COLS), vec(GW), _const_spec((LANES, GW)), vec(GW),
                  _const_spec((LANES, GW)), _const_spec((LANES, GW)), vec(GW), vec(GW), vec(GW),
                  _const_spec((GW, GW)), _const_spec((GW, GW)), _const_spec((GW, LANES)),
                  _const_spec((GW, LANES))],
        out_specs=[row(c) for c in widths],
        out_shape=[jax.ShapeDtypeStruct((bsz, seq, c), F32) for c in widths],
        compiler_params=_params("parallel", "parallel"),
        name="rwkv_pre",
    )(p_rw, p_rw, mu, w0, w2p, a0, a2p, g2p, k_k, k_a, r_k, bd, bdp, sela, selb)


def _rwkv_value_perm(nb):
    xg = LANES // nb
    vr = HEAD_DIM // xg
    j = np.arange(GW)
    return (j % N_HEADS) * HEAD_DIM + (j // (vr * N_HEADS)) * vr + (j // N_HEADS) % vr


def _rwkv_seq_kernel(kw_ref, v_ref, c_ref, y_ref, s_ref, kt_ref, ka_ref, kb_ref, vs_ref, cs_ref, ys_ref,
                     *, tb, nb):
    xg = LANES // nb
    vr = HEAD_DIM // xg
    pw = N_HEADS * vr

    @pl.when(pl.program_id(0) == 0)
    def _():
        s_ref[...] = jnp.zeros_like(s_ref)

    def tile_steps(x, width):
        n = LANES // width
        x = x.reshape(tb // n, n, x.shape[1], width)
        return jnp.concatenate([x[:, j] for j in range(n)], axis=2)

    kt_ref[...] = jnp.swapaxes(kw_ref[...], 0, 1)
    vt = jnp.swapaxes(v_ref[...], 0, 1)
    vtile = jnp.concatenate([vt[:, :, x * pw:(x + 1) * pw] for x in range(xg)], axis=1)
    vs_ref[...] = jnp.swapaxes(tile_steps(vtile, pw), 1, 2).reshape(tb, pw, LANES)
    cw = 2 * N_HEADS
    ct = jnp.swapaxes(c_ref[...], 0, 1)[:, :, 0:cw]
    ctile = jnp.concatenate([ct] * xg, axis=1)
    cs_ref[...] = jnp.swapaxes(tile_steps(ctile, cw), 1, 2).reshape(tb, cw, LANES)

    def to_lanes(x):
        return jnp.concatenate([x] * xg, axis=0).T

    def advance(t, cur_ref, nxt_ref):
        nxt_ref[...] = to_lanes(kt_ref[jnp.minimum(t + 1, tb - 1)])

        def vec(i):
            return cur_ref[i * GW:(i + 1) * GW].reshape(N_HEADS, HEAD_DIM, LANES)[None]

        c1 = jnp.stack([cs_ref[t, h:h + 1] for h in range(N_HEADS)])[None]
        c2 = jnp.stack([cs_ref[t, N_HEADS + h:N_HEADS + h + 1] for h in range(N_HEADS)])[None]
        vv = jnp.stack([vs_ref[t, r:r + 1] for r in range(pw)]).reshape(vr, N_HEADS, 1, LANES)
        s = s_ref[...]
        sa = -jnp.sum(s * vec(0), axis=2, keepdims=True)
        y0 = jnp.sum(s * vec(1), axis=2, keepdims=True)
        y = (y0 + sa * c1 + vv * c2).reshape(pw, 1, LANES)
        s_ref[...] = s * vec(2) + sa * vec(3) + vv * vec(4)
        for r in range(pw):
            ys_ref[t, r:r + 1] = y[r]

    ka_ref[...] = to_lanes(kt_ref[0])

    def step_pair(i, carry):
        advance(2 * i, ka_ref, kb_ref)
        advance(2 * i + 1, kb_ref, ka_ref)
        return carry

    lax.fori_loop(0, tb // 2, step_pair, 0)
    n = LANES // pw
    yt = jnp.swapaxes(ys_ref[...].reshape(tb // n, LANES, LANES), 1, 2)
    pieces = [jnp.concatenate([yt[:, x * nb:(x + 1) * nb, j * pw:(j + 1) * pw] for x in range(xg)], axis=2)
              for j in range(n)]
    y_ref[...] = jnp.swapaxes(jnp.stack(pieces, axis=1).reshape(tb, nb, GW), 0, 1)


def _rwkv_seq(kw, vp, c12):
    nb, seq, _ = kw.shape
    tb = min(32, seq)
    xg = LANES // nb
    vr = HEAD_DIM // xg
    pw = N_HEADS * vr
    blk = lambda c: pl.BlockSpec((nb, tb, c), lambda i: (0, i, 0))
    return pl.pallas_call(
        functools.partial(_rwkv_seq_kernel, tb=tb, nb=nb),
        grid=(seq // tb,),
        in_specs=[blk(RW_VECS * GW), blk(GW), blk(LANES)],
        out_specs=blk(GW),
        out_shape=jax.ShapeDtypeStruct((nb, seq, GW), F32),
        scratch_shapes=[pltpu.VMEM((vr, N_HEADS, HEAD_DIM, LANES), F32),
                        pltpu.VMEM((tb, nb, RW_VECS * GW), F32),
                        pltpu.VMEM((RW_VECS * GW, LANES), F32), pltpu.VMEM((RW_VECS * GW, LANES), F32),
                        pltpu.VMEM((tb, pw, LANES), F32), pltpu.VMEM((tb, 2 * N_HEADS, LANES), F32),
                        pltpu.VMEM((tb, pw, LANES), F32)],
        compiler_params=_params("arbitrary"),
        name="rwkv_seq",
    )(kw, vp, c12)


def _bias_kernel(rb_ref, bk_ref, o_ref):
    h = pl.program_id(0)
    bk = bk_ref[0]
    out = jnp.full(bk.shape, -jnp.inf, F32)
    for b in range(REL_BUCKETS):
        out = jnp.where(bk == b, rb_ref[b, h], out)
    o_ref[0, 0] = out * LOG2E


def _bias_tiles(rel_bias, buckets):
    bs = MOBA_BLOCK
    return pl.pallas_call(
        _bias_kernel,
        grid=(N_HEADS, 2),
        in_specs=[pl.BlockSpec(memory_space=pltpu.SMEM), pl.BlockSpec((1, bs, bs), lambda h, k: (k, 0, 0))],
        out_specs=pl.BlockSpec((1, 1, bs, bs), lambda h, k: (h, k, 0, 0)),
        out_shape=jax.ShapeDtypeStruct((N_HEADS, 2, bs, bs), F32),
        name="moba_bias",
    )(rel_bias, buckets)


def _moba_kernel(far_ref, k_ref, qt_ref, vt_ref, bias_ref, o_ref, s_ref, p_ref, *, seq, nsel):
    bs = MOBA_BLOCK
    nb = seq // bs
    h = pl.program_id(1)
    k = k_ref[...]
    qt = qt_ref[...] * (HEAD_DIM ** -0.5)
    kmean = jnp.mean(k.reshape(nb, bs, HEAD_DIM), axis=1)
    gate = _hdot(kmean, qt)
    jj = lax.broadcasted_iota(jnp.int32, (nb, seq), 0)
    qblk = lax.broadcasted_iota(jnp.int32, (nb, seq), 1) // bs
    g = jnp.where(jj < qblk, gate, -jnp.inf)
    selb = jnp.full((nb, seq), -jnp.inf, F32)
    for r in range(nsel):
        m = jnp.max(g, axis=0, keepdims=True)
        idx = jnp.min(jnp.where(g == m, jj, nb), axis=0, keepdims=True)
        hit = jj == idx
        selb = jnp.where(jnp.logical_and(hit, qblk > r), 0.0, selb)
        g = jnp.where(hit, -jnp.inf, g)
    selfar = selb + far_ref[h] * LOG2E
    kb = k.astype(BF16)
    qtb = (qt * LOG2E).astype(BF16)
    vtb = vt_ref[...].astype(BF16)
    for i in range(nb):
        qs = slice(i * bs, (i + 1) * bs)
        q_i = qtb[:, qs]
        mx = None
        for j in range(i + 1):
            ks = slice(j * bs, (j + 1) * bs)
            s = jnp.dot(kb[ks, :], q_i, preferred_element_type=F32)
            if j == i:
                s = s + bias_ref[0, 0]
            elif j == i - 1:
                s = s + bias_ref[0, 1] + selb[j:j + 1, qs]
            else:
                s = s + selfar[j:j + 1, qs]
            s_ref[ks, :] = s
            mx = s if mx is None else jnp.maximum(mx, s)
        m = jnp.max(mx, axis=0, keepdims=True)
        lsum = None
        for j in range(i + 1):
            ks = slice(j * bs, (j + 1) * bs)
            p = jnp.exp2(s_ref[ks, :] - m)
            p_ref[ks, :] = p.astype(BF16)
            lsum = p if lsum is None else lsum + p
        acc = jnp.dot(vtb[:, 0:(i + 1) * bs], p_ref[0:(i + 1) * bs, :], preferred_element_type=F32)
        o_ref[:, qs] = acc / jnp.sum(lsum, axis=0, keepdims=True)


def _moba(k4, qv_t, bias_tiles, far):
    bsz, _, seq, _ = k4.shape
    nsel = max(1, min(MOBA_TOPK, seq // MOBA_BLOCK - 1))
    return pl.pallas_call(
        functools.partial(_moba_kernel, seq=seq, nsel=nsel),
        grid=(bsz, N_HEADS),
        in_specs=[pl.BlockSpec(memory_space=pltpu.SMEM),
                  pl.BlockSpec((None, None, seq, HEAD_DIM), lambda b, h: (b, h, 0, 0)),
                  pl.BlockSpec((None, HEAD_DIM, seq), lambda b, h: (b, h, 0)),
                  pl.BlockSpec((None, HEAD_DIM, seq), lambda b, h: (b, N_HEADS + h, 0)),
                  pl.BlockSpec((1, 2, MOBA_BLOCK, MOBA_BLOCK), lambda b, h: (h, 0, 0, 0))],
        out_specs=pl.BlockSpec((None, HEAD_DIM, seq), lambda b, h: (b, h, 0)),
        out_shape=jax.ShapeDtypeStruct((bsz, GW, seq), F32),
        scratch_shapes=[pltpu.VMEM((seq, MOBA_BLOCK), F32), pltpu.VMEM((seq, MOBA_BLOCK), BF16)],
        compiler_params=_params("parallel", "parallel"),
        name="moba",
    )(far, k4, qv_t, qv_t, bias_tiles)


def _t5_bucket(rel):
    n = jnp.maximum(rel, 0)
    max_exact = REL_BUCKETS // 2
    nf = jnp.maximum(n, 1).astype(F32)
    large = max_exact + (jnp.log(nf / max_exact) / math.log(REL_MAX_DIST / max_exact)
                         * (REL_BUCKETS - max_exact)).astype(jnp.int32)
    return jnp.where(n < max_exact, n, jnp.minimum(large, REL_BUCKETS - 1))


def _bucket_tiles():
    kpos = jnp.arange(MOBA_BLOCK)[:, None]
    qpos = jnp.arange(MOBA_BLOCK)[None, :]
    rel = qpos - kpos
    own = jnp.where(rel >= 0, _t5_bucket(rel), -1)
    prev = _t5_bucket(rel + MOBA_BLOCK)
    return jnp.stack([own, prev]).astype(jnp.int32)


def _mixout_kernel(ys5_ref, yml_ref, yrw_ref, bonus_ref, g_ref, ymbt_ref, lnw_ref, lnb_ref, bd_ref,
                   gain_ref, w_ref, gpost_ref, x_ref, o_ref):
    bd = bd_ref[...]
    y = yrw_ref[...]
    mean = _sum3(y, bd) * (1.0 / HEAD_DIM)
    d = y - mean
    var = _sum3(d * d, bd) * (1.0 / HEAD_DIM)
    yrw = (d * lax.rsqrt(var + RW_GN_EPS) * lnw_ref[...] + lnb_ref[...] + bonus_ref[...]) * g_ref[...]
    acc = None
    for i, yg in enumerate((ys5_ref[...], yml_ref[...], yrw, ymbt_ref[...].T)):
        cols = slice(i * GW, (i + 1) * GW)
        part = jnp.dot(_rms(yg, gain_ref[:, cols]).astype(BF16), w_ref[cols, :], preferred_element_type=F32)
        acc = part if acc is None else acc + part
    o_ref[...] = x_ref[...] + _rms(acc, gpost_ref[...])


def _mixout(ys5, yml, yrw, bonus, g, ymb_t, ln_w, ln_b, bd, gain, w, gpost, x2d, seq):
    t = x2d.shape[0]
    tm = min(512, seq)
    per = seq // tm
    vec = lambda c: _const_spec((1, c))
    return pl.pallas_call(
        _mixout_kernel,
        grid=(t // tm,),
        in_specs=[_row_spec(tm, GW)] * 5
        + [pl.BlockSpec((None, GW, tm), lambda i: (i // per, 0, i % per)),
           vec(GW), vec(GW), _const_spec((GW, GW)), vec(D_MODEL),
           _const_spec((D_MODEL, D_MODEL)), vec(D_MODEL), _row_spec(tm, D_MODEL)],
        out_specs=_row_spec(tm, D_MODEL),
        out_shape=jax.ShapeDtypeStruct((t, D_MODEL), F32),
        compiler_params=_params("parallel"),
        name="mixout",
    )(ys5, yml, yrw, bonus, g, ymb_t, ln_w, ln_b, bd, gain, w, gpost, x2d)


def _kv_kernel(m_ref, g_ref, w_ref, o_ref):
    h = _rms(m_ref[0], g_ref[...]).astype(BF16)
    o_ref[0] = jnp.dot(h, w_ref[...], preferred_element_type=F32).astype(BF16)


def _kv(mem, g, w):
    bsz, m, _ = mem.shape
    return pl.pallas_call(
        _kv_kernel,
        grid=(bsz,),
        in_specs=[pl.BlockSpec((1, m, D_MODEL), lambda b: (b, 0, 0)), _const_spec((1, D_MODEL)),
                  _const_spec((D_MODEL, 2 * D_MODEL))],
        out_specs=pl.BlockSpec((1, m, 2 * D_MODEL), lambda b: (b, 0, 0)),
        out_shape=jax.ShapeDtypeStruct((bsz, m, 2 * D_MODEL), BF16),
        compiler_params=_params("parallel"),
        name="xa_kv",
    )(mem, g, w)


def _xattn_kernel(x_ref, kv_ref, gpre_ref, wq_ref, wo_ref, gpost_ref, o_ref):
    x = x_ref[...]
    q = jnp.dot(_rms(x, gpre_ref[...]).astype(BF16), wq_ref[...], preferred_element_type=F32)
    acc = None
    for hd in range(XA_HEADS):
        cols = slice(hd * XA_HEAD_DIM, (hd + 1) * XA_HEAD_DIM)
        vcols = slice(D_MODEL + hd * XA_HEAD_DIM, D_MODEL + (hd + 1) * XA_HEAD_DIM)
        s = lax.dot_general(q[:, cols].astype(BF16), kv_ref[0, :, cols], NT_DIMS,
                            preferred_element_type=F32) * (XA_HEAD_DIM ** -0.5)
        p = jnp.exp(s - jnp.max(s, axis=-1, keepdims=True))
        o = jnp.dot(p.astype(BF16), kv_ref[0, :, vcols], preferred_element_type=F32)
        o = o / jnp.sum(p, axis=-1, keepdims=True)
        part = jnp.dot(o.astype(BF16), wo_ref[cols, :], preferred_element_type=F32)
        acc = part if acc is None else acc + part
    o_ref[...] = x + _rms(acc, gpost_ref[...])


def _xattn(x2d, kv, gpre, wq, wo, gpost, seq):
    t = x2d.shape[0]
    m = kv.shape[1]
    tm = min(512, seq)
    per = seq // tm
    vec = _const_spec((1, D_MODEL))
    sq = _const_spec((D_MODEL, D_MODEL))
    return pl.pallas_call(
        _xattn_kernel,
        grid=(t // tm,),
        in_specs=[_row_spec(tm, D_MODEL), pl.BlockSpec((1, m, 2 * D_MODEL), lambda i: (i // per, 0, 0)),
                  vec, sq, sq, vec],
        out_specs=_row_spec(tm, D_MODEL),
        out_shape=jax.ShapeDtypeStruct((t, D_MODEL), F32),
        compiler_params=_params("parallel"),
        name="xattn",
    )(x2d, kv, gpre, wq, wo, gpost)


def _ffn_kernel(x_ref, gpre_ref, wi_ref, wo_ref, gpost_ref, o_ref):
    x = x_ref[...]
    h = _rms(x, gpre_ref[...]).astype(BF16)
    half = D_FF // 2
    acc = None
    for c in range(2):
        gate = jnp.dot(h, wi_ref[:, c * half:(c + 1) * half], preferred_element_type=F32)
        up = jnp.dot(h, wi_ref[:, D_FF + c * half:D_FF + (c + 1) * half], preferred_element_type=F32)
        act = (gate * _sigmoid(gate) * up).astype(BF16)
        part = jnp.dot(act, wo_ref[c * half:(c + 1) * half, :], preferred_element_type=F32)
        acc = part if acc is None else acc + part
    o_ref[...] = x + _rms(acc, gpost_ref[...])


def _ffn(x2d, gpre, wi, wo, gpost):
    t = x2d.shape[0]
    tm = min(512, t)
    vec = _const_spec((1, D_MODEL))
    once = pl.Buffered(1)
    return pl.pallas_call(
        _ffn_kernel,
        grid=(t // tm,),
        in_specs=[_row_spec(tm, D_MODEL), vec,
                  pl.BlockSpec((D_MODEL, 2 * D_FF), lambda i: (0, 0), pipeline_mode=once),
                  pl.BlockSpec((D_FF, D_MODEL), lambda i: (0, 0), pipeline_mode=once), vec],
        out_specs=_row_spec(tm, D_MODEL),
        out_shape=jax.ShapeDtypeStruct((t, D_MODEL), F32),
        compiler_params=_params("parallel"),
        name="ffn",
    )(x2d, gpre, wi, wo, gpost)


def _block_diag(blocks):
    g, a, b = blocks.shape
    eye = jnp.eye(g, dtype=blocks.dtype)
    return jnp.einsum('gab,gk->gakb', blocks, eye).reshape(g * a, g * b)


def kernel(x, mem, rel_bias, norm_pre_mix, norm_post_mix, norm_pre_xa, norm_post_xa, norm_pre_ffn,
           norm_post_ffn, norm_mem, w_in, mix_out_gain, w_out, s5_lam_re, s5_lam_im, s5_log_step,
           s5_b_re, s5_b_im, s5_c_re, s5_c_im, s5_d, s5_w_glu, s5_b_glu, ml_conv, ml_i_bias, ml_f_bias,
           rw_mu, rw_w0, rw_w2, rw_a0, rw_a2, rw_g2, rw_k_k, rw_k_a, rw_r_k, rw_ln_w, rw_ln_b,
           xa_wq, xa_wkv, xa_wo, ffn_w_in, ffn_w_out):
    bsz, seq, _ = x.shape
    depth = w_in.shape[0]
    t = bsz * seq
    row = lambda a: a.reshape(1, -1)

    ml_end = GW + 4 * GW
    rw_lo = ml_end + 2 * N_HEADS
    mb_lo = rw_lo + 4 * GW
    perm = _rwkv_value_perm(bsz)
    w_proj = jnp.concatenate(
        [w_in[:, :, :ml_end], w_in[:, :, rw_lo:mb_lo], w_in[:, :, rw_lo + 2 * GW:rw_lo + 3 * GW][:, :, perm],
         w_in[:, :, mb_lo + GW:mb_lo + 2 * GW], w_in[:, :, ml_end:rw_lo],
         jnp.zeros((depth, D_MODEL, LANES - 2 * N_HEADS), w_in.dtype)],
        axis=2).astype(BF16)
    w_qv_t = jnp.concatenate([w_in[:, :, mb_lo:mb_lo + GW], w_in[:, :, mb_lo + 2 * GW:mb_lo + 3 * GW]],
                             axis=2).transpose(0, 2, 1).astype(BF16)
    rw_rows = 2 * GW + perm
    w_out_b = jnp.concatenate([w_out[:, :2 * GW], w_out[:, rw_rows], w_out[:, 3 * GW:]], axis=1).astype(BF16)
    gain_p = jnp.concatenate([mix_out_gain[:, :2 * GW], mix_out_gain[:, rw_rows], mix_out_gain[:, 3 * GW:]],
                             axis=1)
    mu_p = jnp.concatenate([rw_mu, rw_mu[:, 2 * GW:3 * GW][:, perm]], axis=1)
    wq_b, wkv_b, wo_b = xa_wq.astype(BF16), xa_wkv.astype(BF16), xa_wo.astype(BF16)
    ffn_wi_b, ffn_wo_b = ffn_w_in.astype(BF16), ffn_w_out.astype(BF16)
    w_glu_b = s5_w_glu.astype(BF16)

    ng = depth * S5_GROUPS
    ab_re, ab_im, bb_re, bb_im = _s5_params(
        s5_lam_re.reshape(ng, S5_STATE), s5_lam_im.reshape(ng, S5_STATE), s5_log_step.reshape(ng, 1),
        s5_b_re.transpose(0, 1, 3, 2).reshape(ng, S5_GROUP, S5_STATE),
        s5_b_im.transpose(0, 1, 3, 2).reshape(ng, S5_GROUP, S5_STATE))
    ab_re = ab_re.reshape(depth, 1, S5_P)
    ab_im = ab_im.reshape(depth, 1, S5_P)
    bb_re = bb_re.reshape(depth, S5_GROUPS, S5_GROUP, S5_STATE)
    bb_im = bb_im.reshape(depth, S5_GROUPS, S5_GROUP, S5_STATE)

    bd = _block_diag(jnp.ones((N_HEADS, HEAD_DIM, HEAD_DIM), BF16))
    head_of_lane = jnp.arange(GW) // HEAD_DIM
    sela = (head_of_lane[:, None] == jnp.arange(LANES)[None, :]).astype(BF16)
    selb = (head_of_lane[:, None] + N_HEADS == jnp.arange(LANES)[None, :]).astype(BF16)
    zeros64 = jnp.zeros((HEAD_DIM, GW), F32)
    head_of_perm = jnp.asarray(perm // HEAD_DIM)
    bd_np = (head_of_lane[:, None] == head_of_perm[None, :]).astype(BF16)
    bd_pp = (head_of_perm[:, None] == head_of_perm[None, :]).astype(BF16)

    bias_tiles = _bias_tiles(rel_bias, _bucket_tiles())
    far = rel_bias[REL_BUCKETS - 1, :]

    x2d = x.reshape(t, D_MODEL)
    for l in range(depth):
        u_s5, p_ml, p_rw, gates, k4, qv_t = _proj(x2d, row(norm_pre_mix[l]), w_proj[l], w_qv_t[l], bsz, seq)

        gate_bias = jnp.concatenate([ml_i_bias[l], ml_f_bias[l], jnp.zeros((LANES - 2 * N_HEADS,), F32)])
        y_ml = _mlstm(p_ml.reshape(bsz, seq, 4 * GW), gates.reshape(bsz, seq, LANES), ml_conv[l],
                      row(gate_bias)).reshape(t, GW)

        w2p = jnp.concatenate([rw_w2[l], zeros64], axis=0)
        a2p = jnp.concatenate([zeros64, rw_a2[l]], axis=0)
        kw, v_rw, c12, bonus, g_rw = _rwkv_pre(
            p_rw.reshape(bsz, seq, RW_COLS), row(mu_p[l]), row(rw_w0[l]), w2p, row(rw_a0[l]), a2p,
            rw_g2[l][:, perm], row(rw_k_k[l]), row(rw_k_a[l]), row(rw_r_k[l]), bd, bd_np, sela, selb)
        y_rw = _rwkv_seq(kw, v_rw, c12).reshape(t, GW)

        y_mb_t = _moba(k4, qv_t, bias_tiles, far)

        wb = jnp.concatenate([_block_diag(bb_re[l]), _block_diag(bb_im[l])], axis=1).astype(BF16)
        wc = jnp.concatenate([_block_diag(s5_c_re[l].transpose(0, 2, 1)),
                              -_block_diag(s5_c_im[l].transpose(0, 2, 1))], axis=0).astype(BF16)
        y_s5 = _s5(u_s5.reshape(bsz, seq, GW), wb, ab_re[l], ab_im[l], wc, row(s5_d[l]), w_glu_b[l],
                   row(s5_b_glu[l])).reshape(t, GW)

        x2d = _mixout(y_s5, y_ml, y_rw, bonus.reshape(t, GW), g_rw.reshape(t, GW), y_mb_t,
                      row(rw_ln_w[l][perm]), row(rw_ln_b[l][perm]), bd_pp, row(gain_p[l]), w_out_b[l],
                      row(norm_post_mix[l]), x2d, seq)

        kv = _kv(mem, row(norm_mem[l]), wkv_b[l])
        x2d = _xattn(x2d, kv, row(norm_pre_xa[l]), wq_b[l], wo_b[l], row(norm_post_xa[l]), seq)
        x2d = _ffn(x2d, row(norm_pre_ffn[l]), ffn_wi_b[l], ffn_wo_b[l], row(norm_post_ffn[l]))
    return x2d.reshape(bsz, seq, D_MODEL)
```

```python
import functools
import math

import jax
import jax.numpy as jnp
import numpy as np
from jax import lax
from jax.experimental import pallas as pl
from jax.experimental.pallas import tpu as pltpu

F32 = jnp.float32
BF16 = jnp.bfloat16
HIGHEST = lax.Precision.HIGHEST

D_MODEL = 1024
N_GROUPS = 4
GW = 256
HEAD_DIM = 64
N_HEADS = 4
S5_GROUP = 16
S5_GROUPS = 16
S5_STATE = 64
S5_P = S5_GROUPS * S5_STATE
CONV_K = 4
ML_CHUNK = 256
RW_GN_EPS = 64e-5
RW_VECS = 5
MOBA_BLOCK = 256
MOBA_TOPK = 3
REL_BUCKETS = 32
REL_MAX_DIST = 128
XA_HEADS = 4
XA_HEAD_DIM = 256
D_FF = 2816
RMS_EPS = 1e-6
LOG2E = 1.4426950408889634
LANES = 128
RW_COLS = 5 * GW
PROJ_COLS = 6 * GW + RW_COLS + LANES
VMEM_LIMIT = 56 * 1024 * 1024

NT_DIMS = (((1,), (1,)), ((), ()))


def _rms(x, g):
    return x * lax.rsqrt(jnp.mean(x * x, axis=-1, keepdims=True) + RMS_EPS) * g


def _bdot(a, b):
    return jnp.dot(a.astype(BF16), b.astype(BF16), preferred_element_type=F32)


def _hdot(a, b):
    return jnp.dot(a, b, precision=HIGHEST, preferred_element_type=F32)


def _sum3(x, ones_b):
    hi = x.astype(BF16)
    r1 = x - hi.astype(F32)
    mid = r1.astype(BF16)
    lo = (r1 - mid.astype(F32)).astype(BF16)
    dot = lambda a: jnp.dot(a, ones_b, preferred_element_type=F32)
    return dot(hi) + dot(mid) + dot(lo)


def _sigmoid(x):
    return 1.0 / (1.0 + jnp.exp(-x))


def _softplus(x):
    return jnp.maximum(x, 0.0) + jnp.log1p(jnp.exp(-jnp.abs(x)))


def _params(*sem):
    return pltpu.CompilerParams(dimension_semantics=sem, vmem_limit_bytes=VMEM_LIMIT)


def _row_spec(tm, cols):
    return pl.BlockSpec((tm, cols), lambda i: (i, 0))


def _const_spec(shape):
    return pl.BlockSpec(shape, lambda *_: (0,) * len(shape))


def _proj_kernel(x_ref, g_ref, w_ref, wt_ref, s5_ref, ml_ref, rw_ref, gt_ref, k4_ref, qvt_ref):
    h = _rms(x_ref[...], g_ref[...]).astype(BF16)

    def mm(lo, hi):
        return jnp.dot(h, w_ref[:, lo:hi], preferred_element_type=F32)

    s5_ref[...] = mm(0, GW)
    ml_ref[...] = mm(GW, 5 * GW)
    rw_ref[...] = mm(5 * GW, 5 * GW + RW_COLS)
    kk = mm(5 * GW + RW_COLS, 6 * GW + RW_COLS)
    for hd in range(N_HEADS):
        k4_ref[hd] = kk[:, hd * HEAD_DIM:(hd + 1) * HEAD_DIM]
    gt_ref[...] = mm(6 * GW + RW_COLS, PROJ_COLS)
    qvt_ref[...] = lax.dot_general(wt_ref[...], h, NT_DIMS, preferred_element_type=F32)


def _proj(x2d, g, w, wt, bsz, seq):
    t = x2d.shape[0]
    tm = min(1024, seq)
    per = seq // tm
    widths = (GW, 4 * GW, RW_COLS, LANES)
    return pl.pallas_call(
        _proj_kernel,
        grid=(t // tm,),
        in_specs=[_row_spec(tm, D_MODEL), _const_spec((1, D_MODEL)), _const_spec((D_MODEL, PROJ_COLS)),
                  _const_spec((2 * GW, D_MODEL))],
        out_specs=[_row_spec(tm, c) for c in widths]
        + [pl.BlockSpec((None, N_HEADS, tm, HEAD_DIM), lambda i: (i // per, 0, i % per, 0)),
           pl.BlockSpec((None, 2 * GW, tm), lambda i: (i // per, 0, i % per))],
        out_shape=[jax.ShapeDtypeStruct((t, c), F32) for c in widths]
        + [jax.ShapeDtypeStruct((bsz, N_HEADS, seq, HEAD_DIM), F32),
           jax.ShapeDtypeStruct((bsz, 2 * GW, seq), F32)],
        compiler_params=_params("parallel"),
        name="proj",
    )(x2d, g, w, wt)


def _s5_param_kernel(lr_ref, li_ref, ls_ref, br_ref, bi_ref, abr_ref, abi_ref, bbr_ref, bbi_ref):
    lr, li = lr_ref[...], li_ref[...]
    step = jnp.exp(ls_ref[...])
    mag = jnp.exp(lr * step)
    ang = li * step
    ab_re, ab_im = mag * jnp.cos(ang), mag * jnp.sin(ang)
    den = lr * lr + li * li
    co_re = ((ab_re - 1.0) * lr + ab_im * li) / den
    co_im = (ab_im * lr - (ab_re - 1.0) * li) / den
    abr_ref[...] = ab_re
    abi_ref[...] = ab_im
    br, bi = br_ref[...], bi_ref[...]
    bbr_ref[...] = co_re[:, None, :] * br - co_im[:, None, :] * bi
    bbi_ref[...] = co_re[:, None, :] * bi + co_im[:, None, :] * br


def _s5_params(lam_re, lam_im, log_step, b_re, b_im):
    n = lam_re.shape[0]
    return pl.pallas_call(
        _s5_param_kernel,
        out_shape=[jax.ShapeDtypeStruct((n, S5_STATE), F32)] * 2
        + [jax.ShapeDtypeStruct((n, S5_GROUP, S5_STATE), F32)] * 2,
        name="s5_params",
    )(lam_re, lam_im, log_step, b_re, b_im)


def _s5_kernel(u_ref, wb_ref, are_ref, aim_ref, wc_ref, d_ref, wg_ref, bg_ref, o_ref, xs_ref, st_ref,
               *, tm, nb):
    @pl.when(pl.program_id(0) == 0)
    def _():
        st_ref[...] = jnp.zeros_like(st_ref)

    u = jnp.swapaxes(u_ref[...], 0, 1).reshape(tm * nb, GW)
    xs_ref[...] = _bdot(u, wb_ref[...])
    ch = 256
    for c in range(S5_P // ch):
        re_cols = slice(c * ch, (c + 1) * ch)
        im_cols = slice(S5_P + c * ch, S5_P + (c + 1) * ch)
        a_re = jnp.broadcast_to(are_ref[:, re_cols], (nb, ch))
        a_im = jnp.broadcast_to(aim_ref[:, re_cols], (nb, ch))

        def body(t, carry, re_cols=re_cols, im_cols=im_cols, a_re=a_re, a_im=a_im):
            s_re, s_im = carry
            rows = pl.ds(pl.multiple_of(t * nb, nb), nb)
            n_re = a_re * s_re - a_im * s_im + xs_ref[rows, re_cols]
            n_im = a_re * s_im + a_im * s_re + xs_ref[rows, im_cols]
            xs_ref[rows, re_cols] = n_re
            xs_ref[rows, im_cols] = n_im
            return n_re, n_im

        s_re, s_im = lax.fori_loop(0, tm, body, (st_ref[:, re_cols], st_ref[:, im_cols]))
        st_ref[:, re_cols] = s_re
        st_ref[:, im_cols] = s_im

    y = _bdot(xs_ref[...], wc_ref[...]) + d_ref[...] * u
    z = jax.nn.gelu(y)
    out = z * _sigmoid(_bdot(z, wg_ref[...]) + bg_ref[...])
    o_ref[...] = jnp.swapaxes(out.reshape(tm, nb, GW), 0, 1)


def _s5(u3, wb, a_re, a_im, wc, d, wg, bg):
    nb, seq, _ = u3.shape
    tm = min(64, seq)
    blk = pl.BlockSpec((nb, tm, GW), lambda i: (0, i, 0))
    return pl.pallas_call(
        functools.partial(_s5_kernel, tm=tm, nb=nb),
        grid=(seq // tm,),
        in_specs=[blk, _const_spec((GW, 2 * S5_P)), _const_spec((1, S5_P)),
                  _const_spec((1, S5_P)), _const_spec((2 * S5_P, GW)), _const_spec((1, GW)),
                  _const_spec((GW, GW)), _const_spec((1, GW))],
        out_specs=blk,
        out_shape=jax.ShapeDtypeStruct((nb, seq, GW), F32),
        scratch_shapes=[pltpu.VMEM((nb * tm, 2 * S5_P), F32), pltpu.VMEM((nb, 2 * S5_P), F32)],
        compiler_params=_params("arbitrary"),
        name="s5",
    )(u3, wb, a_re, a_im, wc, d, wg, bg)


def _mlstm_kernel(p_ref, g_ref, cw_ref, gb_ref, o_ref, pad_ref, *, seq):
    cl = ML_CHUNK
    dh = HEAD_DIM
    pad_ref[0:8, :] = jnp.zeros((8, 2 * GW), F32)
    pad_ref[8:8 + seq, :] = p_ref[0, :, 0:2 * GW]
    row = lax.broadcasted_iota(jnp.int32, (cl, cl), 0)
    col = lax.broadcasted_iota(jnp.int32, (cl, cl), 1)
    trif = jnp.where(col <= row, 1.0, 0.0).astype(F32)
    causal_t = row <= col
    cw = cw_ref[...]

    def chunk(c, carry):
        cts, n8s, ms = carry
        r0 = pl.multiple_of(c * cl, cl)
        win = pad_ref[pl.ds(r0, cl + 8), :]
        conv = cw[0:1, :] * win[5:5 + cl, :]
        for j in range(1, CONV_K):
            conv = conv + cw[j:j + 1, :] * win[5 + j:5 + j + cl, :]
        qk = conv * _sigmoid(conv)
        qt = qk[:, 0:GW].T.astype(BF16)
        kb = (qk[:, GW:2 * GW] * (dh ** -0.5)).astype(BF16)
        vt = p_ref[0, pl.ds(r0, cl), 2 * GW:3 * GW].T.astype(BF16)
        ot = p_ref[0, pl.ds(r0, cl), 3 * GW:4 * GW].T
        gc = g_ref[0, pl.ds(r0, cl), :] + gb_ref[...]
        bcum = _hdot(trif, -_softplus(-gc))
        gct = gc.T
        bct = bcum.T
        outs, cts_n, n8s_n, ms_n = [], [], [], []
        for h in range(N_HEADS):
            hs = slice(h * dh, (h + 1) * dh)
            q_t, k_h, v_t = qt[hs, :], kb[:, hs], vt[hs, :]
            colv = bcum[:, N_HEADS + h:N_HEADS + h + 1] - gc[:, h:h + 1]
            br = bct[N_HEADS + h:N_HEADS + h + 1, :]
            lir = gct[h:h + 1, :]
            m_prev = ms[h]
            dmat = jnp.where(causal_t, br - colv, -jnp.inf)
            b_inter = br + m_prev
            m_comb = jnp.maximum(b_inter, jnp.max(dmat, axis=0, keepdims=True))
            w_inter = jnp.exp(b_inter - m_comb)
            s_t = jnp.exp(dmat - m_comb) * jnp.dot(k_h, q_t, preferred_element_type=F32)
            num = (jnp.dot(v_t, s_t.astype(BF16), preferred_element_type=F32)
                   + w_inter * jnp.dot(cts[h].astype(BF16), q_t, preferred_element_type=F32))
            nq = jnp.dot(n8s[h].astype(BF16), q_t, preferred_element_type=F32)[0:1, :]
            den = jnp.sum(s_t, axis=0, keepdims=True) + w_inter * nq
            hh = num / jnp.maximum(jnp.abs(den), jnp.exp(-m_comb))
            outs.append(_sigmoid(ot[hs, :]) * hh)
            b_last = br[:, cl - 1:cl]
            w_st = b_last - br + lir
            m_in = jnp.max(w_st, axis=1, keepdims=True)
            e_row = jnp.exp(w_st - m_in)
            kv_t = jnp.dot((v_t * e_row).astype(BF16), k_h, preferred_element_type=F32)
            ks8 = jnp.dot(jnp.broadcast_to(e_row, (8, cl)).astype(BF16), k_h, preferred_element_type=F32)
            m_new = jnp.maximum(b_last + m_prev, m_in)
            a = jnp.exp(b_last + m_prev - m_new)
            e = jnp.exp(m_in - m_new)
            cts_n.append(a * cts[h] + e * kv_t)
            n8s_n.append(a * n8s[h] + e * ks8)
            ms_n.append(m_new)
        o_ref[0, pl.ds(r0, cl), :] = jnp.concatenate(outs, axis=0).T
        return tuple(cts_n), tuple(n8s_n), tuple(ms_n)

    init = (tuple(jnp.zeros((dh, dh), F32) for _ in range(N_HEADS)),
            tuple(jnp.zeros((8, dh), F32) for _ in range(N_HEADS)),
            tuple(jnp.zeros((1, 1), F32) for _ in range(N_HEADS)))
    lax.fori_loop(0, seq // cl, chunk, init)


def _mlstm(p_ml, gates, conv_w, gate_bias):
    bsz, seq, _ = p_ml.shape
    return pl.pallas_call(
        functools.partial(_mlstm_kernel, seq=seq),
        grid=(bsz,),
        in_specs=[pl.BlockSpec((1, seq, 4 * GW), lambda b: (b, 0, 0)),
                  pl.BlockSpec((1, seq, LANES), lambda b: (b, 0, 0)),
                  _const_spec((CONV_K, 2 * GW)), _const_spec((1, LANES))],
        out_specs=pl.BlockSpec((1, seq, GW), lambda b: (b, 0, 0)),
        out_shape=jax.ShapeDtypeStruct((bsz, seq, GW), F32),
        scratch_shapes=[pltpu.VMEM((seq + 8, 2 * GW), F32)],
        compiler_params=_params("parallel"),
        name="mlstm",
    )(p_ml, gates, conv_w, gate_bias)


def _rwkv_pre_kernel(p_ref, pv_ref, mu_ref, w0_ref, w2_ref, a0_ref, a2_ref, g2_ref, kk_ref, ka_ref, rk_ref,
                     bd_ref, bdp_ref, sela_ref, selb_ref, kw_o, v_o, c_o, bonus_o, g_o):
    x = p_ref[0]
    last = jnp.where(pl.program_id(1) > 0, pv_ref[0][7:8, :], 0.0)
    rowi = lax.broadcasted_iota(jnp.int32, x.shape, 0)
    prev = jnp.where(rowi == 0, last, pltpu.roll(x, 1, 0))
    p = x + (prev - x) * mu_ref[...]
    r, k = p[:, 0:GW], p[:, GW:2 * GW]
    lo = p[:, 3 * GW:3 * GW + LANES]
    g_lo = p[:, 3 * GW + LANES:4 * GW]
    vp = p[:, 4 * GW:5 * GW]
    w = -_softplus(-(w0_ref[...] + _bdot(jnp.tanh(lo), w2_ref[...]))) - 0.5
    decay = jnp.exp(-jnp.exp(w))
    a = _sigmoid(a0_ref[...] + _bdot(lo, a2_ref[...]))
    g_o[0] = _bdot(_sigmoid(g_lo), g2_ref[...])
    kk = k * kk_ref[...]
    kk = kk * lax.rsqrt(_sum3(kk * kk, bd_ref[...]) + 1e-12)
    km = k * (1.0 + (a - 1.0) * ka_ref[...])
    alp = kk * a
    for i, val in enumerate((kk, decay * r, decay, alp, km)):
        kw_o[0, :, i * GW:(i + 1) * GW] = val
    v_o[0] = vp
    c_o[0] = _sum3(alp * r, sela_ref[...]) + _sum3(km * r, selb_ref[...])
    bonus_o[0] = _sum3(r * km * rk_ref[...], bdp_ref[...]) * vp


def _rwkv_pre(p_rw, mu, w0, w2p, a0, a2p, g2p, k_k, k_a, r_k, bd, bdp, sela, selb):
    bsz, seq, _ = p_rw.shape
    tr = min(256, seq)
    row = lambda c: pl.BlockSpec((1, tr, c), lambda b, i: (b, i, 0))
    prev = pl.BlockSpec((1, 8, RW_COLS), lambda b, i: (b, jnp.maximum(i * (tr // 8) - 1, 0), 0))
    vec = lambda c: _const_spec((1, c))
    widths = (RW_VECS * GW, GW, LANES, GW, GW)
    return pl.pallas_call(
        _rwkv_pre_kernel,
        grid=(bsz, seq // tr),
        in_specs=[row(RW_COLS), prev, vec(RW_COLS), vec(GW), _const_spec((LANES, GW)), vec(GW),
                  _const_spec((LANES, GW)), _const_spec((LANES, GW)), vec(GW), vec(GW), vec(GW),
                  _const_spec((GW, GW)), _const_spec((GW, GW)), _const_spec((GW, LANES)),
                  _const_spec((GW, LANES))],
        out_specs=[row(c) for c in widths],
        out_shape=[jax.ShapeDtypeStruct((bsz, seq, c), F32) for c in widths],
        compiler_params=_params("parallel", "parallel"),
        name="rwkv_pre",
    )(p_rw, p_rw, mu, w0, w2p, a0, a2p, g2p, k_k, k_a, r_k, bd, bdp, sela, selb)


def _rwkv_value_perm(nb):
    xg = LANES // nb
    vr = HEAD_DIM // xg
    j = np.arange(GW)
    return (j % N_HEADS) * HEAD_DIM + (j // (vr * N_HEADS)) * vr + (j // N_HEADS) % vr


def _rwkv_seq_kernel(kw_ref, v_ref, c_ref, y_ref, s_ref, kt_ref, ka_ref, kb_ref, vs_ref, cs_ref, ys_ref,
                     *, tb, nb):
    xg = LANES // nb
    vr = HEAD_DIM // xg
    pw = N_HEADS * vr

    @pl.when(pl.program_id(0) == 0)
    def _():
        s_ref[...] = jnp.zeros_like(s_ref)

    def tile_steps(x, width):
        n = LANES // width
        x = x.reshape(tb // n, n, x.shape[1], width)
        return jnp.concatenate([x[:, j] for j in range(n)], axis=2)

    kt_ref[...] = jnp.swapaxes(kw_ref[...], 0, 1)
    vt = jnp.swapaxes(v_ref[...], 0, 1)
    vtile = jnp.concatenate([vt[:, :, x * pw:(x + 1) * pw] for x in range(xg)], axis=1)
    vs_ref[...] = jnp.swapaxes(tile_steps(vtile, pw), 1, 2).reshape(tb, pw, LANES)
    cw = 2 * N_HEADS
    ct = jnp.swapaxes(c_ref[...], 0, 1)[:, :, 0:cw]
    ctile = jnp.concatenate([ct] * xg, axis=1)
    cs_ref[...] = jnp.swapaxes(tile_steps(ctile, cw), 1, 2).reshape(tb, cw, LANES)

    def to_lanes(x):
        return jnp.concatenate([x] * xg, axis=0).T

    def advance(t, cur_ref, nxt_ref):
        nxt_ref[...] = to_lanes(kt_ref[jnp.minimum(t + 1, tb - 1)])

        for h in range(N_HEADS):
            def vec(i, h=h):
                return cur_ref[i * GW + h * HEAD_DIM:i * GW + (h + 1) * HEAD_DIM][None]

            c1 = cs_ref[t, h:h + 1][None]
            c2 = cs_ref[t, N_HEADS + h:N_HEADS + h + 1][None]
            vv = jnp.stack([vs_ref[t, v * N_HEADS + h:v * N_HEADS + h + 1] for v in range(vr)])
            s = s_ref[:, h]
            sa = -jnp.sum(s * vec(0), axis=1, keepdims=True)
            y0 = jnp.sum(s * vec(1), axis=1, keepdims=True)
            y = y0 + sa * c1 + vv * c2
            s_ref[:, h] = s * vec(2) + sa * vec(3) + vv * vec(4)
            for v in range(vr):
                ys_ref[t, v * N_HEADS + h:v * N_HEADS + h + 1] = y[v]

    ka_ref[...] = to_lanes(kt_ref[0])

    def step_pair(i, carry):
        advance(2 * i, ka_ref, kb_ref)
        advance(2 * i + 1, kb_ref, ka_ref)
        return carry

    lax.fori_loop(0, tb // 2, step_pair, 0)
    n = LANES // pw
    yt = jnp.swapaxes(ys_ref[...].reshape(tb // n, LANES, LANES), 1, 2)
    pieces = [jnp.concatenate([yt[:, x * nb:(x + 1) * nb, j * pw:(j + 1) * pw] for x in range(xg)], axis=2)
              for j in range(n)]
    y_ref[...] = jnp.swapaxes(jnp.stack(pieces, axis=1).reshape(tb, nb, GW), 0, 1)


def _rwkv_seq(kw, vp, c12):
    nb, seq, _ = kw.shape
    tb = min(32, seq)
    xg = LANES // nb
    vr = HEAD_DIM // xg
    pw = N_HEADS * vr
    blk = lambda c: pl.BlockSpec((nb, tb, c), lambda i: (0, i, 0))
    return pl.pallas_call(
        functools.partial(_rwkv_seq_kernel, tb=tb, nb=nb),
        grid=(seq // tb,),
        in_specs=[blk(RW_VECS * GW), blk(GW), blk(LANES)],
        out_specs=blk(GW),
        out_shape=jax.ShapeDtypeStruct((nb, seq, GW), F32),
        scratch_shapes=[pltpu.VMEM((vr, N_HEADS, HEAD_DIM, LANES), F32),
                        pltpu.VMEM((tb, nb, RW_VECS * GW), F32),
                        pltpu.VMEM((RW_VECS * GW, LANES), F32), pltpu.VMEM((RW_VECS * GW, LANES), F32),
                        pltpu.VMEM((tb, pw, LANES), F32), pltpu.VMEM((tb, 2 * N_HEADS, LANES), F32),
                        pltpu.VMEM((tb, pw, LANES), F32)],
        compiler_params=_params("arbitrary"),
        name="rwkv_seq",
    )(kw, vp, c12)


def _bias_kernel(rb_ref, bk_ref, o_ref):
    h = pl.program_id(0)
    bk = bk_ref[0]
    out = jnp.full(bk.shape, -jnp.inf, F32)
    for b in range(REL_BUCKETS):
        out = jnp.where(bk == b, rb_ref[b, h], out)
    o_ref[0, 0] = out * LOG2E


def _bias_tiles(rel_bias, buckets):
    bs = MOBA_BLOCK
    return pl.pallas_call(
        _bias_kernel,
        grid=(N_HEADS, 2),
        in_specs=[pl.BlockSpec(memory_space=pltpu.SMEM), pl.BlockSpec((1, bs, bs), lambda h, k: (k, 0, 0))],
        out_specs=pl.BlockSpec((1, 1, bs, bs), lambda h, k: (h, k, 0, 0)),
        out_shape=jax.ShapeDtypeStruct((N_HEADS, 2, bs, bs), F32),
        name="moba_bias",
    )(rel_bias, buckets)


def _moba_kernel(far_ref, k_ref, qt_ref, vt_ref, bias_ref, o_ref, s_ref, p_ref, *, seq, nsel):
    bs = MOBA_BLOCK
    nb = seq // bs
    h = pl.program_id(1)
    k = k_ref[...]
    qt = qt_ref[...] * (HEAD_DIM ** -0.5)
    kmean = jnp.mean(k.reshape(nb, bs, HEAD_DIM), axis=1)
    gate = _hdot(kmean, qt)
    jj = lax.broadcasted_iota(jnp.int32, (nb, seq), 0)
    qblk = lax.broadcasted_iota(jnp.int32, (nb, seq), 1) // bs
    g = jnp.where(jj < qblk, gate, -jnp.inf)
    selb = jnp.full((nb, seq), -jnp.inf, F32)
    for r in range(nsel):
        m = jnp.max(g, axis=0, keepdims=True)
        idx = jnp.min(jnp.where(g == m, jj, nb), axis=0, keepdims=True)
        hit = jj == idx
        selb = jnp.where(jnp.logical_and(hit, qblk > r), 0.0, selb)
        g = jnp.where(hit, -jnp.inf, g)
    selfar = selb + far_ref[h] * LOG2E
    kb = k.astype(BF16)
    qtb = (qt * LOG2E).astype(BF16)
    vtb = vt_ref[...].astype(BF16)
    for i in range(nb):
        qs = slice(i * bs, (i + 1) * bs)
        q_i = qtb[:, qs]
        mx = None
        for j in range(i + 1):
            ks = slice(j * bs, (j + 1) * bs)
            s = jnp.dot(kb[ks, :], q_i, preferred_element_type=F32)
            if j == i:
                s = s + bias_ref[0, 0]
            elif j == i - 1:
                s = s + bias_ref[0, 1] + selb[j:j + 1, qs]
            else:
                s = s + selfar[j:j + 1, qs]
            s_ref[ks, :] = s
            mx = s if mx is None else jnp.maximum(mx, s)
        m = jnp.max(mx, axis=0, keepdims=True)
        lsum = None
        for j in range(i + 1):
            ks = slice(j * bs, (j + 1) * bs)
            p = jnp.exp2(s_ref[ks, :] - m)
            p_ref[ks, :] = p.astype(BF16)
            lsum = p if lsum is None else lsum + p
        acc = jnp.dot(vtb[:, 0:(i + 1) * bs], p_ref[0:(i + 1) * bs, :], preferred_element_type=F32)
        o_ref[:, qs] = acc / jnp.sum(lsum, axis=0, keepdims=True)


def _moba(k4, qv_t, bias_tiles, far):
    bsz, _, seq, _ = k4.shape
    nsel = max(1, min(MOBA_TOPK, seq // MOBA_BLOCK - 1))
    return pl.pallas_call(
        functools.partial(_moba_kernel, seq=seq, nsel=nsel),
        grid=(bsz, N_HEADS),
        in_specs=[pl.BlockSpec(memory_space=pltpu.SMEM),
                  pl.BlockSpec((None, None, seq, HEAD_DIM), lambda b, h: (b, h, 0, 0)),
                  pl.BlockSpec((None, HEAD_DIM, seq), lambda b, h: (b, h, 0)),
                  pl.BlockSpec((None, HEAD_DIM, seq), lambda b, h: (b, N_HEADS + h, 0)),
                  pl.BlockSpec((1, 2, MOBA_BLOCK, MOBA_BLOCK), lambda b, h: (h, 0, 0, 0))],
        out_specs=pl.BlockSpec((None, HEAD_DIM, seq), lambda b, h: (b, h, 0)),
        out_shape=jax.ShapeDtypeStruct((bsz, GW, seq), F32),
        scratch_shapes=[pltpu.VMEM((seq, MOBA_BLOCK), F32), pltpu.VMEM((seq, MOBA_BLOCK), BF16)],
        compiler_params=_params("parallel", "parallel"),
        name="moba",
    )(far, k4, qv_t, qv_t, bias_tiles)


def _t5_bucket(rel):
    n = jnp.maximum(rel, 0)
    max_exact = REL_BUCKETS // 2
    nf = jnp.maximum(n, 1).astype(F32)
    large = max_exact + (jnp.log(nf / max_exact) / math.log(REL_MAX_DIST / max_exact)
                         * (REL_BUCKETS - max_exact)).astype(jnp.int32)
    return jnp.where(n < max_exact, n, jnp.minimum(large, REL_BUCKETS - 1))


def _bucket_tiles():
    kpos = jnp.arange(MOBA_BLOCK)[:, None]
    qpos = jnp.arange(MOBA_BLOCK)[None, :]
    rel = qpos - kpos
    own = jnp.where(rel >= 0, _t5_bucket(rel), -1)
    prev = _t5_bucket(rel + MOBA_BLOCK)
    return jnp.stack([own, prev]).astype(jnp.int32)


def _mixout_kernel(ys5_ref, yml_ref, yrw_ref, bonus_ref, g_ref, ymbt_ref, lnw_ref, lnb_ref, bd_ref,
                   gain_ref, w_ref, gpost_ref, x_ref, o_ref):
    bd = bd_ref[...]
    y = yrw_ref[...]
    mean = _sum3(y, bd) * (1.0 / HEAD_DIM)
    d = y - mean
    var = _sum3(d * d, bd) * (1.0 / HEAD_DIM)
    yrw = (d * lax.rsqrt(var + RW_GN_EPS) * lnw_ref[...] + lnb_ref[...] + bonus_ref[...]) * g_ref[...]
    acc = None
    for i, yg in enumerate((ys5_ref[...], yml_ref[...], yrw, ymbt_ref[...].T)):
        cols = slice(i * GW, (i + 1) * GW)
        part = jnp.dot(_rms(yg, gain_ref[:, cols]).astype(BF16), w_ref[cols, :], preferred_element_type=F32)
        acc = part if acc is None else acc + part
    o_ref[...] = x_ref[...] + _rms(acc, gpost_ref[...])


def _mixout(ys5, yml, yrw, bonus, g, ymb_t, ln_w, ln_b, bd, gain, w, gpost, x2d, seq):
    t = x2d.shape[0]
    tm = min(1024, seq)
    per = seq // tm
    vec = lambda c: _const_spec((1, c))
    return pl.pallas_call(
        _mixout_kernel,
        grid=(t // tm,),
        in_specs=[_row_spec(tm, GW)] * 5
        + [pl.BlockSpec((None, GW, tm), lambda i: (i // per, 0, i % per)),
           vec(GW), vec(GW), _const_spec((GW, GW)), vec(D_MODEL),
           _const_spec((D_MODEL, D_MODEL)), vec(D_MODEL), _row_spec(tm, D_MODEL)],
        out_specs=_row_spec(tm, D_MODEL),
        out_shape=jax.ShapeDtypeStruct((t, D_MODEL), F32),
        compiler_params=_params("parallel"),
        name="mixout",
    )(ys5, yml, yrw, bonus, g, ymb_t, ln_w, ln_b, bd, gain, w, gpost, x2d)


def _kv_kernel(m_ref, g_ref, w_ref, o_ref):
    h = _rms(m_ref[0], g_ref[...]).astype(BF16)
    o_ref[0] = jnp.dot(h, w_ref[...], preferred_element_type=F32).astype(BF16)


def _kv(mem, g, w):
    bsz, m, _ = mem.shape
    return pl.pallas_call(
        _kv_kernel,
        grid=(bsz,),
        in_specs=[pl.BlockSpec((1, m, D_MODEL), lambda b: (b, 0, 0)), _const_spec((1, D_MODEL)),
                  _const_spec((D_MODEL, 2 * D_MODEL))],
        out_specs=pl.BlockSpec((1, m, 2 * D_MODEL), lambda b: (b, 0, 0)),
        out_shape=jax.ShapeDtypeStruct((bsz, m, 2 * D_MODEL), BF16),
        compiler_params=_params("parallel"),
        name="xa_kv",
    )(mem, g, w)


def _xattn_kernel(x_ref, kv_ref, gpre_ref, wq_ref, wo_ref, gpost_ref, o_ref):
    x = x_ref[...]
    q = jnp.dot(_rms(x, gpre_ref[...]).astype(BF16), wq_ref[...], preferred_element_type=F32)
    acc = None
    for hd in range(XA_HEADS):
        cols = slice(hd * XA_HEAD_DIM, (hd + 1) * XA_HEAD_DIM)
        vcols = slice(D_MODEL + hd * XA_HEAD_DIM, D_MODEL + (hd + 1) * XA_HEAD_DIM)
        s = lax.dot_general(q[:, cols].astype(BF16), kv_ref[0, :, cols], NT_DIMS,
                            preferred_element_type=F32) * (XA_HEAD_DIM ** -0.5)
        p = jnp.exp(s - jnp.max(s, axis=-1, keepdims=True))
        o = jnp.dot(p.astype(BF16), kv_ref[0, :, vcols], preferred_element_type=F32)
        o = o / jnp.sum(p, axis=-1, keepdims=True)
        part = jnp.dot(o.astype(BF16), wo_ref[cols, :], preferred_element_type=F32)
        acc = part if acc is None else acc + part
    o_ref[...] = x + _rms(acc, gpost_ref[...])


def _xattn(x2d, kv, gpre, wq, wo, gpost, seq):
    t = x2d.shape[0]
    m = kv.shape[1]
    tm = min(1024, seq)
    per = seq // tm
    vec = _const_spec((1, D_MODEL))
    sq = _const_spec((D_MODEL, D_MODEL))
    return pl.pallas_call(
        _xattn_kernel,
        grid=(t // tm,),
        in_specs=[_row_spec(tm, D_MODEL), pl.BlockSpec((1, m, 2 * D_MODEL), lambda i: (i // per, 0, 0)),
                  vec, sq, sq, vec],
        out_specs=_row_spec(tm, D_MODEL),
        out_shape=jax.ShapeDtypeStruct((t, D_MODEL), F32),
        compiler_params=_params("parallel"),
        name="xattn",
    )(x2d, kv, gpre, wq, wo, gpost)


def _ffn_kernel(x_ref, gpre_ref, wi_ref, wo_ref, gpost_ref, o_ref):
    x = x_ref[...]
    h = _rms(x, gpre_ref[...]).astype(BF16)
    half = D_FF // 2
    acc = None
    for c in range(2):
        gate = jnp.dot(h, wi_ref[:, c * half:(c + 1) * half], preferred_element_type=F32)
        up = jnp.dot(h, wi_ref[:, D_FF + c * half:D_FF + (c + 1) * half], preferred_element_type=F32)
        act = (gate * _sigmoid(gate) * up).astype(BF16)
        part = jnp.dot(act, wo_ref[c * half:(c + 1) * half, :], preferred_element_type=F32)
        acc = part if acc is None else acc + part
    o_ref[...] = x + _rms(acc, gpost_ref[...])


def _ffn(x2d, gpre, wi, wo, gpost):
    t = x2d.shape[0]
    tm = min(512, t)
    vec = _const_spec((1, D_MODEL))
    once = pl.Buffered(1)
    return pl.pallas_call(
        _ffn_kernel,
        grid=(t // tm,),
        in_specs=[_row_spec(tm, D_MODEL), vec,
                  pl.BlockSpec((D_MODEL, 2 * D_FF), lambda i: (0, 0), pipeline_mode=once),
                  pl.BlockSpec((D_FF, D_MODEL), lambda i: (0, 0), pipeline_mode=once), vec],
        out_specs=_row_spec(tm, D_MODEL),
        out_shape=jax.ShapeDtypeStruct((t, D_MODEL), F32),
        compiler_params=_params("parallel"),
        name="ffn",
    )(x2d, gpre, wi, wo, gpost)


def _block_diag(blocks):
    g, a, b = blocks.shape
    eye = jnp.eye(g, dtype=blocks.dtype)
    return jnp.einsum('gab,gk->gakb', blocks, eye).reshape(g * a, g * b)


def kernel(x, mem, rel_bias, norm_pre_mix, norm_post_mix, norm_pre_xa, norm_post_xa, norm_pre_ffn,
           norm_post_ffn, norm_mem, w_in, mix_out_gain, w_out, s5_lam_re, s5_lam_im, s5_log_step,
           s5_b_re, s5_b_im, s5_c_re, s5_c_im, s5_d, s5_w_glu, s5_b_glu, ml_conv, ml_i_bias, ml_f_bias,
           rw_mu, rw_w0, rw_w2, rw_a0, rw_a2, rw_g2, rw_k_k, rw_k_a, rw_r_k, rw_ln_w, rw_ln_b,
           xa_wq, xa_wkv, xa_wo, ffn_w_in, ffn_w_out):
    bsz, seq, _ = x.shape
    depth = w_in.shape[0]
    t = bsz * seq
    row = lambda a: a.reshape(1, -1)

    ml_end = GW + 4 * GW
    rw_lo = ml_end + 2 * N_HEADS
    mb_lo = rw_lo + 4 * GW
    perm = _rwkv_value_perm(bsz)
    w_proj = jnp.concatenate(
        [w_in[:, :, :ml_end], w_in[:, :, rw_lo:mb_lo], w_in[:, :, rw_lo + 2 * GW:rw_lo + 3 * GW][:, :, perm],
         w_in[:, :, mb_lo + GW:mb_lo + 2 * GW], w_in[:, :, ml_end:rw_lo],
         jnp.zeros((depth, D_MODEL, LANES - 2 * N_HEADS), w_in.dtype)],
        axis=2).astype(BF16)
    w_qv_t = jnp.concatenate([w_in[:, :, mb_lo:mb_lo + GW], w_in[:, :, mb_lo + 2 * GW:mb_lo + 3 * GW]],
                             axis=2).transpose(0, 2, 1).astype(BF16)
    rw_rows = 2 * GW + perm
    w_out_b = jnp.concatenate([w_out[:, :2 * GW], w_out[:, rw_rows], w_out[:, 3 * GW:]], axis=1).astype(BF16)
    gain_p = jnp.concatenate([mix_out_gain[:, :2 * GW], mix_out_gain[:, rw_rows], mix_out_gain[:, 3 * GW:]],
                             axis=1)
    mu_p = jnp.concatenate([rw_mu, rw_mu[:, 2 * GW:3 * GW][:, perm]], axis=1)
    wq_b, wkv_b, wo_b = xa_wq.astype(BF16), xa_wkv.astype(BF16), xa_wo.astype(BF16)
    ffn_wi_b, ffn_wo_b = ffn_w_in.astype(BF16), ffn_w_out.astype(BF16)
    w_glu_b = s5_w_glu.astype(BF16)

    ng = depth * S5_GROUPS
    ab_re, ab_im, bb_re, bb_im = _s5_params(
        s5_lam_re.reshape(ng, S5_STATE), s5_lam_im.reshape(ng, S5_STATE), s5_log_step.reshape(ng, 1),
        s5_b_re.transpose(0, 1, 3, 2).reshape(ng, S5_GROUP, S5_STATE),
        s5_b_im.transpose(0, 1, 3, 2).reshape(ng, S5_GROUP, S5_STATE))
    ab_re = ab_re.reshape(depth, 1, S5_P)
    ab_im = ab_im.reshape(depth, 1, S5_P)
    bb_re = bb_re.reshape(depth, S5_GROUPS, S5_GROUP, S5_STATE)
    bb_im = bb_im.reshape(depth, S5_GROUPS, S5_GROUP, S5_STATE)

    bd = _block_diag(jnp.ones((N_HEADS, HEAD_DIM, HEAD_DIM), BF16))
    head_of_lane = jnp.arange(GW) // HEAD_DIM
    sela = (head_of_lane[:, None] == jnp.arange(LANES)[None, :]).astype(BF16)
    selb = (head_of_lane[:, None] + N_HEADS == jnp.arange(LANES)[None, :]).astype(BF16)
    zeros64 = jnp.zeros((HEAD_DIM, GW), F32)
    head_of_perm = jnp.asarray(perm // HEAD_DIM)
    bd_np = (head_of_lane[:, None] == head_of_perm[None, :]).astype(BF16)
    bd_pp = (head_of_perm[:, None] == head_of_perm[None, :]).astype(BF16)

    bias_tiles = _bias_tiles(rel_bias, _bucket_tiles())
    far = rel_bias[REL_BUCKETS - 1, :]

    x2d = x.reshape(t, D_MODEL)
    for l in range(depth):
        u_s5, p_ml, p_rw, gates, k4, qv_t = _proj(x2d, row(norm_pre_mix[l]), w_proj[l], w_qv_t[l], bsz, seq)

        gate_bias = jnp.concatenate([ml_i_bias[l], ml_f_bias[l], jnp.zeros((LANES - 2 * N_HEADS,), F32)])
        y_ml = _mlstm(p_ml.reshape(bsz, seq, 4 * GW), gates.reshape(bsz, seq, LANES), ml_conv[l],
                      row(gate_bias)).reshape(t, GW)

        w2p = jnp.concatenate([rw_w2[l], zeros64], axis=0)
        a2p = jnp.concatenate([zeros64, rw_a2[l]], axis=0)
        kw, v_rw, c12, bonus, g_rw = _rwkv_pre(
            p_rw.reshape(bsz, seq, RW_COLS), row(mu_p[l]), row(rw_w0[l]), w2p, row(rw_a0[l]), a2p,
            rw_g2[l][:, perm], row(rw_k_k[l]), row(rw_k_a[l]), row(rw_r_k[l]), bd, bd_np, sela, selb)
        y_rw = _rwkv_seq(kw, v_rw, c12).reshape(t, GW)

        y_mb_t = _moba(k4, qv_t, bias_tiles, far)

        wb = jnp.concatenate([_block_diag(bb_re[l]), _block_diag(bb_im[l])], axis=1).astype(BF16)
        wc = jnp.concatenate([_block_diag(s5_c_re[l].transpose(0, 2, 1)),
                              -_block_diag(s5_c_im[l].transpose(0, 2, 1))], axis=0).astype(BF16)
        y_s5 = _s5(u_s5.reshape(bsz, seq, GW), wb, ab_re[l], ab_im[l], wc, row(s5_d[l]), w_glu_b[l],
                   row(s5_b_glu[l])).reshape(t, GW)

        x2d = _mixout(y_s5, y_ml, y_rw, bonus.reshape(t, GW), g_rw.reshape(t, GW), y_mb_t,
                      row(rw_ln_w[l][perm]), row(rw_ln_b[l][perm]), bd_pp, row(gain_p[l]), w_out_b[l],
                      row(norm_post_mix[l]), x2d, seq)

        kv = _kv(mem, row(norm_mem[l]), wkv_b[l])
        x2d = _xattn(x2d, kv, row(norm_pre_xa[l]), wq_b[l], wo_b[l], row(norm_post_xa[l]), seq)
        x2d = _ffn(x2d, row(norm_pre_ffn[l]), ffn_wi_b[l], ffn_wo_b[l], row(norm_post_ffn[l]))
    return x2d.reshape(bsz, seq, D_MODEL)
```

```python
import functools
import math

import jax
import jax.numpy as jnp
import numpy as np
from jax import lax
from jax.experimental import pallas as pl
from jax.experimental.pallas import tpu as pltpu

F32 = jnp.float32
BF16 = jnp.bfloat16
HIGHEST = lax.Precision.HIGHEST

D_MODEL = 1024
N_GROUPS = 4
GW = 256
HEAD_DIM = 64
N_HEADS = 4
S5_GROUP = 16
S5_GROUPS = 16
S5_STATE = 64
S5_P = S5_GROUPS * S5_STATE
CONV_K = 4
ML_CHUNK = 256
RW_GN_EPS = 64e-5
RW_VECS = 5
RW_STEP_VECS = 4
RW_WINDOW = 32
MOBA_BLOCK = 256
MOBA_TOPK = 3
REL_BUCKETS = 32
REL_MAX_DIST = 128
XA_HEADS = 4
XA_HEAD_DIM = 256
D_FF = 2816
FFN_CHUNK = 256
RMS_EPS = 1e-6
LOG2E = 1.4426950408889634
LANES = 128
RW_COLS = 5 * GW
PROJ_COLS = 6 * GW + RW_COLS + LANES
VMEM_LIMIT = 56 * 1024 * 1024

NT_DIMS = (((1,), (1,)), ((), ()))


def _rms(x, g):
    return x * lax.rsqrt(jnp.mean(x * x, axis=-1, keepdims=True) + RMS_EPS) * g


def _bdot(a, b):
    return jnp.dot(a.astype(BF16), b.astype(BF16), preferred_element_type=F32)


def _hdot(a, b):
    return jnp.dot(a, b, precision=HIGHEST, preferred_element_type=F32)


def _sum3(x, ones_b):
    hi = x.astype(BF16)
    r1 = x - hi.astype(F32)
    mid = r1.astype(BF16)
    lo = (r1 - mid.astype(F32)).astype(BF16)
    dot = lambda a: jnp.dot(a, ones_b, preferred_element_type=F32)
    return dot(hi) + dot(mid) + dot(lo)


def _sum3_left(ones_b, x):
    hi = x.astype(BF16)
    r1 = x - hi.astype(F32)
    mid = r1.astype(BF16)
    lo = (r1 - mid.astype(F32)).astype(BF16)
    dot = lambda a: jnp.dot(ones_b, a, preferred_element_type=F32)
    return dot(hi) + dot(mid) + dot(lo)


def _sigmoid(x):
    return 1.0 / (1.0 + jnp.exp(-x))


def _softplus(x):
    return jnp.maximum(x, 0.0) + jnp.log1p(jnp.exp(-jnp.abs(x)))


def _params(*sem):
    return pltpu.CompilerParams(dimension_semantics=sem, vmem_limit_bytes=VMEM_LIMIT)


def _row_spec(tm, cols):
    return pl.BlockSpec((tm, cols), lambda i: (i, 0))


def _const_spec(shape):
    return pl.BlockSpec(shape, lambda *_: (0,) * len(shape))


def _proj_kernel(x_ref, g_ref, w_ref, wt_ref, s5_ref, ml_ref, rw_ref, gt_ref, k4_ref, qvt_ref):
    h = _rms(x_ref[...], g_ref[...]).astype(BF16)

    def mm(lo, hi):
        return jnp.dot(h, w_ref[:, lo:hi], preferred_element_type=F32)

    s5_ref[...] = mm(0, GW)
    ml_ref[...] = mm(GW, 5 * GW)
    rw_ref[...] = mm(5 * GW, 5 * GW + RW_COLS)
    kk = mm(5 * GW + RW_COLS, 6 * GW + RW_COLS)
    for hd in range(N_HEADS):
        k4_ref[hd] = kk[:, hd * HEAD_DIM:(hd + 1) * HEAD_DIM]
    gt_ref[...] = mm(6 * GW + RW_COLS, PROJ_COLS)
    qvt_ref[...] = lax.dot_general(wt_ref[...], h, NT_DIMS, preferred_element_type=F32)


def _proj(x2d, g, w, wt, bsz, seq):
    t = x2d.shape[0]
    tm = min(1024, seq)
    per = seq // tm
    widths = (GW, 4 * GW, RW_COLS, LANES)
    return pl.pallas_call(
        _proj_kernel,
        grid=(t // tm,),
        in_specs=[_row_spec(tm, D_MODEL), _const_spec((1, D_MODEL)), _const_spec((D_MODEL, PROJ_COLS)),
                  _const_spec((2 * GW, D_MODEL))],
        out_specs=[_row_spec(tm, c) for c in widths]
        + [pl.BlockSpec((None, N_HEADS, tm, HEAD_DIM), lambda i: (i // per, 0, i % per, 0)),
           pl.BlockSpec((None, 2 * GW, tm), lambda i: (i // per, 0, i % per))],
        out_shape=[jax.ShapeDtypeStruct((t, c), F32) for c in widths]
        + [jax.ShapeDtypeStruct((bsz, N_HEADS, seq, HEAD_DIM), F32),
           jax.ShapeDtypeStruct((bsz, 2 * GW, seq), F32)],
        compiler_params=_params("parallel"),
        name="proj",
    )(x2d, g, w, wt)


def _s5_param_kernel(lr_ref, li_ref, ls_ref, br_ref, bi_ref, abr_ref, abi_ref, bbr_ref, bbi_ref):
    lr, li = lr_ref[...], li_ref[...]
    step = jnp.exp(ls_ref[...])
    mag = jnp.exp(lr * step)
    ang = li * step
    ab_re, ab_im = mag * jnp.cos(ang), mag * jnp.sin(ang)
    den = lr * lr + li * li
    co_re = ((ab_re - 1.0) * lr + ab_im * li) / den
    co_im = (ab_im * lr - (ab_re - 1.0) * li) / den
    abr_ref[...] = ab_re
    abi_ref[...] = ab_im
    br, bi = br_ref[...], bi_ref[...]
    bbr_ref[...] = co_re[:, None, :] * br - co_im[:, None, :] * bi
    bbi_ref[...] = co_re[:, None, :] * bi + co_im[:, None, :] * br


def _s5_params(lam_re, lam_im, log_step, b_re, b_im):
    n = lam_re.shape[0]
    return pl.pallas_call(
        _s5_param_kernel,
        out_shape=[jax.ShapeDtypeStruct((n, S5_STATE), F32)] * 2
        + [jax.ShapeDtypeStruct((n, S5_GROUP, S5_STATE), F32)] * 2,
        name="s5_params",
    )(lam_re, lam_im, log_step, b_re, b_im)


def _s5_kernel(u_ref, wb_ref, are_ref, aim_ref, wc_ref, d_ref, wg_ref, bg_ref, o_ref, xs_ref, st_ref,
               *, tm, nb):
    @pl.when(pl.program_id(0) == 0)
    def _():
        st_ref[...] = jnp.zeros_like(st_ref)

    u = jnp.swapaxes(u_ref[...], 0, 1).reshape(tm * nb, GW)
    xs_ref[...] = _bdot(u, wb_ref[...])
    ch = 256
    for c in range(S5_P // ch):
        re_cols = slice(c * ch, (c + 1) * ch)
        im_cols = slice(S5_P + c * ch, S5_P + (c + 1) * ch)
        a_re = jnp.broadcast_to(are_ref[:, re_cols], (nb, ch))
        a_im = jnp.broadcast_to(aim_ref[:, re_cols], (nb, ch))

        def body(t, carry, re_cols=re_cols, im_cols=im_cols, a_re=a_re, a_im=a_im):
            s_re, s_im = carry
            rows = pl.ds(pl.multiple_of(t * nb, nb), nb)
            n_re = a_re * s_re - a_im * s_im + xs_ref[rows, re_cols]
            n_im = a_re * s_im + a_im * s_re + xs_ref[rows, im_cols]
            xs_ref[rows, re_cols] = n_re
            xs_ref[rows, im_cols] = n_im
            return n_re, n_im

        s_re, s_im = lax.fori_loop(0, tm, body, (st_ref[:, re_cols], st_ref[:, im_cols]))
        st_ref[:, re_cols] = s_re
        st_ref[:, im_cols] = s_im

    y = _bdot(xs_ref[...], wc_ref[...]) + d_ref[...] * u
    z = jax.nn.gelu(y)
    out = z * _sigmoid(_bdot(z, wg_ref[...]) + bg_ref[...])
    o_ref[...] = jnp.swapaxes(out.reshape(tm, nb, GW), 0, 1)


def _s5(u3, wb, a_re, a_im, wc, d, wg, bg):
    nb, seq, _ = u3.shape
    tm = min(64, seq)
    blk = pl.BlockSpec((nb, tm, GW), lambda i: (0, i, 0))
    return pl.pallas_call(
        functools.partial(_s5_kernel, tm=tm, nb=nb),
        grid=(seq // tm,),
        in_specs=[blk, _const_spec((GW, 2 * S5_P)), _const_spec((1, S5_P)),
                  _const_spec((1, S5_P)), _const_spec((2 * S5_P, GW)), _const_spec((1, GW)),
                  _const_spec((GW, GW)), _const_spec((1, GW))],
        out_specs=blk,
        out_shape=jax.ShapeDtypeStruct((nb, seq, GW), F32),
        scratch_shapes=[pltpu.VMEM((nb * tm, 2 * S5_P), F32), pltpu.VMEM((nb, 2 * S5_P), F32)],
        compiler_params=_params("arbitrary"),
        name="s5",
    )(u3, wb, a_re, a_im, wc, d, wg, bg)


def _mlstm_kernel(p_ref, g_ref, cw_ref, gb_ref, o_ref, pad_ref, *, seq):
    cl = ML_CHUNK
    dh = HEAD_DIM
    pad_ref[0:8, :] = jnp.zeros((8, 2 * GW), F32)
    pad_ref[8:8 + seq, :] = p_ref[0, :, 0:2 * GW]
    row = lax.broadcasted_iota(jnp.int32, (cl, cl), 0)
    col = lax.broadcasted_iota(jnp.int32, (cl, cl), 1)
    trif = jnp.where(col <= row, 1.0, 0.0).astype(F32)
    causal_t = row <= col
    cw = cw_ref[...]

    def chunk(c, carry):
        cts, n8s, ms = carry
        r0 = pl.multiple_of(c * cl, cl)
        win = pad_ref[pl.ds(r0, cl + 8), :]
        conv = cw[0:1, :] * win[5:5 + cl, :]
        for j in range(1, CONV_K):
            conv = conv + cw[j:j + 1, :] * win[5 + j:5 + j + cl, :]
        qk = conv * _sigmoid(conv)
        qt = qk[:, 0:GW].T.astype(BF16)
        kb = (qk[:, GW:2 * GW] * (dh ** -0.5)).astype(BF16)
        vt = p_ref[0, pl.ds(r0, cl), 2 * GW:3 * GW].T.astype(BF16)
        ot = p_ref[0, pl.ds(r0, cl), 3 * GW:4 * GW].T
        gc = g_ref[0, pl.ds(r0, cl), :] + gb_ref[...]
        bcum = _hdot(trif, -_softplus(-gc))
        gct = gc.T
        bct = bcum.T
        outs, cts_n, n8s_n, ms_n = [], [], [], []
        for h in range(N_HEADS):
            hs = slice(h * dh, (h + 1) * dh)
            q_t, k_h, v_t = qt[hs, :], kb[:, hs], vt[hs, :]
            colv = bcum[:, N_HEADS + h:N_HEADS + h + 1] - gc[:, h:h + 1]
            br = bct[N_HEADS + h:N_HEADS + h + 1, :]
            lir = gct[h:h + 1, :]
            m_prev = ms[h]
            dmat = jnp.where(causal_t, br - colv, -jnp.inf)
            b_inter = br + m_prev
            m_comb = jnp.maximum(b_inter, jnp.max(dmat, axis=0, keepdims=True))
            w_inter = jnp.exp(b_inter - m_comb)
            s_t = jnp.exp(dmat - m_comb) * jnp.dot(k_h, q_t, preferred_element_type=F32)
            num = (jnp.dot(v_t, s_t.astype(BF16), preferred_element_type=F32)
                   + w_inter * jnp.dot(cts[h].astype(BF16), q_t, preferred_element_type=F32))
            nq = jnp.dot(n8s[h].astype(BF16), q_t, preferred_element_type=F32)[0:1, :]
            den = jnp.sum(s_t, axis=0, keepdims=True) + w_inter * nq
            hh = num / jnp.maximum(jnp.abs(den), jnp.exp(-m_comb))
            outs.append(_sigmoid(ot[hs, :]) * hh)
            b_last = br[:, cl - 1:cl]
            w_st = b_last - br + lir
            m_in = jnp.max(w_st, axis=1, keepdims=True)
            e_row = jnp.exp(w_st - m_in)
            kv_t = jnp.dot((v_t * e_row).astype(BF16), k_h, preferred_element_type=F32)
            ks8 = jnp.dot(jnp.broadcast_to(e_row, (8, cl)).astype(BF16), k_h, preferred_element_type=F32)
            m_new = jnp.maximum(b_last + m_prev, m_in)
            a = jnp.exp(b_last + m_prev - m_new)
            e = jnp.exp(m_in - m_new)
            cts_n.append(a * cts[h] + e * kv_t)
            n8s_n.append(a * n8s[h] + e * ks8)
            ms_n.append(m_new)
        o_ref[0, pl.ds(r0, cl), :] = jnp.concatenate(outs, axis=0).T
        return tuple(cts_n), tuple(n8s_n), tuple(ms_n)

    init = (tuple(jnp.zeros((dh, dh), F32) for _ in range(N_HEADS)),
            tuple(jnp.zeros((8, dh), F32) for _ in range(N_HEADS)),
            tuple(jnp.zeros((1, 1), F32) for _ in range(N_HEADS)))
    lax.fori_loop(0, seq // cl, chunk, init)


def _mlstm(p_ml, gates, conv_w, gate_bias):
    bsz, seq, _ = p_ml.shape
    return pl.pallas_call(
        functools.partial(_mlstm_kernel, seq=seq),
        grid=(bsz,),
        in_specs=[pl.BlockSpec((1, seq, 4 * GW), lambda b: (b, 0, 0)),
                  pl.BlockSpec((1, seq, LANES), lambda b: (b, 0, 0)),
                  _const_spec((CONV_K, 2 * GW)), _const_spec((1, LANES))],
        out_specs=pl.BlockSpec((1, seq, GW), lambda b: (b, 0, 0)),
        out_shape=jax.ShapeDtypeStruct((bsz, seq, GW), F32),
        scratch_shapes=[pltpu.VMEM((seq + 8, 2 * GW), F32)],
        compiler_params=_params("parallel"),
        name="mlstm",
    )(p_ml, gates, conv_w, gate_bias)


def _rwkv_pre_kernel(p_ref, pv_ref, mu_ref, w0_ref, w2_ref, a0_ref, a2_ref, g2_ref, kk_ref, ka_ref, rk_ref,
                     bd_ref, bdp_ref, sela_ref, selb_ref, tril_ref, kw_o, v_o, c_o, bonus_o, g_o):
    x = p_ref[0]
    last = jnp.where(pl.program_id(1) > 0, pv_ref[0][7:8, :], 0.0)
    rowi = lax.broadcasted_iota(jnp.int32, x.shape, 0)
    prev = jnp.where(rowi == 0, last, pltpu.roll(x, 1, 0))
    p = x + (prev - x) * mu_ref[...]
    r, k = p[:, 0:GW], p[:, GW:2 * GW]
    lo = p[:, 3 * GW:3 * GW + LANES]
    g_lo = p[:, 3 * GW + LANES:4 * GW]
    vp = p[:, 4 * GW:5 * GW]
    w = -_softplus(-(w0_ref[...] + _bdot(jnp.tanh(lo), w2_ref[...]))) - 0.5
    logw = -jnp.exp(w)
    a = _sigmoid(a0_ref[...] + _bdot(lo, a2_ref[...]))
    g_o[0] = _bdot(_sigmoid(g_lo), g2_ref[...])
    kk = k * kk_ref[...]
    kk = kk * lax.rsqrt(_sum3(kk * kk, bd_ref[...]) + 1e-12)
    km = k * (1.0 + (a - 1.0) * ka_ref[...])
    alp = kk * a
    cum = _sum3_left(tril_ref[...], logw)
    g_in = jnp.exp(cum)
    g_inv = jnp.exp(-cum)
    for i, val in enumerate((kk * jnp.exp(cum - logw), r * g_in, alp * g_inv, km * g_inv, g_in)):
        kw_o[0, :, i * GW:(i + 1) * GW] = val
    v_o[0] = vp
    c_o[0] = _sum3(alp * r, sela_ref[...]) + _sum3(km * r, selb_ref[...])
    bonus_o[0] = _sum3(r * km * rk_ref[...], bdp_ref[...]) * vp


def _rwkv_pre(p_rw, mu, w0, w2p, a0, a2p, g2p, k_k, k_a, r_k, bd, bdp, sela, selb):
    bsz, seq, _ = p_rw.shape
    tr = min(256, seq)
    win = jnp.arange(tr) // min(RW_WINDOW, seq)
    tril = jnp.logical_and(win[:, None] == win[None, :],
                           jnp.arange(tr)[None, :] <= jnp.arange(tr)[:, None]).astype(BF16)
    row = lambda c: pl.BlockSpec((1, tr, c), lambda b, i: (b, i, 0))
    prev = pl.BlockSpec((1, 8, RW_COLS), lambda b, i: (b, jnp.maximum(i * (tr // 8) - 1, 0), 0))
    vec = lambda c: _const_spec((1, c))
    widths = (RW_VECS * GW, GW, LANES, GW, GW)
    return pl.pallas_call(
        _rwkv_pre_kernel,
        grid=(bsz, seq // tr),
        in_specs=[row(RW_COLS), prev, vec(RW_COLS), vec(GW), _const_spec((LANES, GW)), vec(GW),
                  _const_spec((LANES, GW)), _const_spec((LANES, GW)), vec(GW), vec(GW), vec(GW),
                  _const_spec((GW, GW)), _const_spec((GW, GW)), _const_spec((GW, LANES)),
                  _const_spec((GW, LANES)), _const_spec((tr, tr))],
        out_specs=[row(c) for c in widths],
        out_shape=[jax.ShapeDtypeStruct((bsz, seq, c), F32) for c in widths],
        compiler_params=_params("parallel", "parallel"),
        name="rwkv_pre",
    )(p_rw, p_rw, mu, w0, w2p, a0, a2p, g2p, k_k, k_a, r_k, bd, bdp, sela, selb, tril)


def _rwkv_value_perm(nb):
    xg = LANES // nb
    vr = HEAD_DIM // xg
    j = np.arange(GW)
    return (j % N_HEADS) * HEAD_DIM + (j // (vr * N_HEADS)) * vr + (j // N_HEADS) % vr


def _rwkv_seq_kernel(kw_ref, v_ref, c_ref, y_ref, s_ref, kt_ref, ka_ref, kb_ref, vs_ref, cs_ref, ys_ref,
                     *, tb, nb):
    xg = LANES // nb
    vr = HEAD_DIM // xg
    pw = N_HEADS * vr

    @pl.when(pl.program_id(0) == 0)
    def _():
        s_ref[...] = jnp.zeros_like(s_ref)

    def tile_steps(x, width):
        n = LANES // width
        x = x.reshape(tb // n, n, x.shape[1], width)
        return jnp.concatenate([x[:, j] for j in range(n)], axis=2)

    kt_ref[...] = jnp.swapaxes(kw_ref[...], 0, 1)
    vt = jnp.swapaxes(v_ref[...], 0, 1)
    vtile = jnp.concatenate([vt[:, :, x * pw:(x + 1) * pw] for x in range(xg)], axis=1)
    vs_ref[...] = jnp.swapaxes(tile_steps(vtile, pw), 1, 2).reshape(tb, pw, LANES)
    cw = 2 * N_HEADS
    ct = jnp.swapaxes(c_ref[...], 0, 1)[:, :, 0:cw]
    ctile = jnp.concatenate([ct] * xg, axis=1)
    cs_ref[...] = jnp.swapaxes(tile_steps(ctile, cw), 1, 2).reshape(tb, cw, LANES)

    def to_lanes(x):
        return jnp.concatenate([x] * xg, axis=0).T

    step_cols = RW_STEP_VECS * GW

    def advance(t, cur_ref, nxt_ref):
        nxt_ref[...] = to_lanes(kt_ref[jnp.minimum(t + 1, tb - 1), :, 0:step_cols])

        for h in range(N_HEADS):
            def vec(i, h=h):
                return cur_ref[i * GW + h * HEAD_DIM:i * GW + (h + 1) * HEAD_DIM][None]

            c1 = cs_ref[t, h:h + 1][None]
            c2 = cs_ref[t, N_HEADS + h:N_HEADS + h + 1][None]
            vv = jnp.stack([vs_ref[t, v * N_HEADS + h:v * N_HEADS + h + 1] for v in range(vr)])
            s = s_ref[:, h]
            sa = -jnp.sum(s * vec(0), axis=1, keepdims=True)
            y0 = jnp.sum(s * vec(1), axis=1, keepdims=True)
            y = y0 + sa * c1 + vv * c2
            s_ref[:, h] = s + sa * vec(2) + vv * vec(3)
            for v in range(vr):
                ys_ref[t, v * N_HEADS + h:v * N_HEADS + h + 1] = y[v]

    ka_ref[...] = to_lanes(kt_ref[0, :, 0:step_cols])

    def step_pair(i, carry):
        advance(2 * i, ka_ref, kb_ref)
        advance(2 * i + 1, kb_ref, ka_ref)
        return carry

    lax.fori_loop(0, tb // 2, step_pair, 0)
    gamma = to_lanes(kt_ref[tb - 1, :, step_cols:RW_VECS * GW])
    s_ref[...] = s_ref[...] * gamma.reshape(N_HEADS, HEAD_DIM, LANES)[None]
    n = LANES // pw
    yt = jnp.swapaxes(ys_ref[...].reshape(tb // n, LANES, LANES), 1, 2)
    pieces = [jnp.concatenate([yt[:, x * nb:(x + 1) * nb, j * pw:(j + 1) * pw] for x in range(xg)], axis=2)
              for j in range(n)]
    y_ref[...] = jnp.swapaxes(jnp.stack(pieces, axis=1).reshape(tb, nb, GW), 0, 1)


def _rwkv_seq(kw, vp, c12):
    nb, seq, _ = kw.shape
    tb = min(RW_WINDOW, seq)
    xg = LANES // nb
    vr = HEAD_DIM // xg
    pw = N_HEADS * vr
    blk = lambda c: pl.BlockSpec((nb, tb, c), lambda i: (0, i, 0))
    return pl.pallas_call(
        functools.partial(_rwkv_seq_kernel, tb=tb, nb=nb),
        grid=(seq // tb,),
        in_specs=[blk(RW_VECS * GW), blk(GW), blk(LANES)],
        out_specs=blk(GW),
        out_shape=jax.ShapeDtypeStruct((nb, seq, GW), F32),
        scratch_shapes=[pltpu.VMEM((vr, N_HEADS, HEAD_DIM, LANES), F32),
                        pltpu.VMEM((tb, nb, RW_VECS * GW), F32),
                        pltpu.VMEM((RW_STEP_VECS * GW, LANES), F32),
                        pltpu.VMEM((RW_STEP_VECS * GW, LANES), F32),
                        pltpu.VMEM((tb, pw, LANES), F32), pltpu.VMEM((tb, 2 * N_HEADS, LANES), F32),
                        pltpu.VMEM((tb, pw, LANES), F32)],
        compiler_params=_params("arbitrary"),
        name="rwkv_seq",
    )(kw, vp, c12)


def _bias_kernel(rb_ref, bk_ref, o_ref):
    h = pl.program_id(0)
    bk = bk_ref[0]
    out = jnp.full(bk.shape, -jnp.inf, F32)
    for b in range(REL_BUCKETS):
        out = jnp.where(bk == b, rb_ref[b, h], out)
    o_ref[0, 0] = out * LOG2E


def _bias_tiles(rel_bias, buckets):
    bs = MOBA_BLOCK
    return pl.pallas_call(
        _bias_kernel,
        grid=(N_HEADS, 2),
        in_specs=[pl.BlockSpec(memory_space=pltpu.SMEM), pl.BlockSpec((1, bs, bs), lambda h, k: (k, 0, 0))],
        out_specs=pl.BlockSpec((1, 1, bs, bs), lambda h, k: (h, k, 0, 0)),
        out_shape=jax.ShapeDtypeStruct((N_HEADS, 2, bs, bs), F32),
        name="moba_bias",
    )(rel_bias, buckets)


def _moba_kernel(far_ref, k_ref, qt_ref, vt_ref, bias_ref, o_ref, s_ref, p_ref, *, seq, nsel):
    bs = MOBA_BLOCK
    nb = seq // bs
    h = pl.program_id(1)
    k = k_ref[...]
    qt = qt_ref[...] * (HEAD_DIM ** -0.5)
    kmean = jnp.mean(k.reshape(nb, bs, HEAD_DIM), axis=1)
    gate = _hdot(kmean, qt)
    jj = lax.broadcasted_iota(jnp.int32, (nb, seq), 0)
    qblk = lax.broadcasted_iota(jnp.int32, (nb, seq), 1) // bs
    g = jnp.where(jj < qblk, gate, -jnp.inf)
    selb = jnp.full((nb, seq), -jnp.inf, F32)
    for r in range(nsel):
        m = jnp.max(g, axis=0, keepdims=True)
        idx = jnp.min(jnp.where(g == m, jj, nb), axis=0, keepdims=True)
        hit = jj == idx
        selb = jnp.where(jnp.logical_and(hit, qblk > r), 0.0, selb)
        g = jnp.where(hit, -jnp.inf, g)
    selfar = selb + far_ref[h] * LOG2E
    kb = k.astype(BF16)
    qtb = (qt * LOG2E).astype(BF16)
    vtb = vt_ref[...].astype(BF16)
    for i in range(nb):
        qs = slice(i * bs, (i + 1) * bs)
        q_i = qtb[:, qs]
        mx = None
        for j in range(i + 1):
            ks = slice(j * bs, (j + 1) * bs)
            s = jnp.dot(kb[ks, :], q_i, preferred_element_type=F32)
            if j == i:
                s = s + bias_ref[0, 0]
            elif j == i - 1:
                s = s + bias_ref[0, 1] + selb[j:j + 1, qs]
            else:
                s = s + selfar[j:j + 1, qs]
            s_ref[ks, :] = s
            mx = s if mx is None else jnp.maximum(mx, s)
        m = jnp.max(mx, axis=0, keepdims=True)
        lsum = None
        for j in range(i + 1):
            ks = slice(j * bs, (j + 1) * bs)
            p = jnp.exp2(s_ref[ks, :] - m)
            p_ref[ks, :] = p.astype(BF16)
            lsum = p if lsum is None else lsum + p
        acc = jnp.dot(vtb[:, 0:(i + 1) * bs], p_ref[0:(i + 1) * bs, :], preferred_element_type=F32)
        o_ref[:, qs] = acc / jnp.sum(lsum, axis=0, keepdims=True)


def _moba(k4, qv_t, bias_tiles, far):
    bsz, _, seq, _ = k4.shape
    nsel = max(1, min(MOBA_TOPK, seq // MOBA_BLOCK - 1))
    return pl.pallas_call(
        functools.partial(_moba_kernel, seq=seq, nsel=nsel),
        grid=(bsz, N_HEADS),
        in_specs=[pl.BlockSpec(memory_space=pltpu.SMEM),
                  pl.BlockSpec((None, None, seq, HEAD_DIM), lambda b, h: (b, h, 0, 0)),
                  pl.BlockSpec((None, HEAD_DIM, seq), lambda b, h: (b, h, 0)),
                  pl.BlockSpec((None, HEAD_DIM, seq), lambda b, h: (b, N_HEADS + h, 0)),
                  pl.BlockSpec((1, 2, MOBA_BLOCK, MOBA_BLOCK), lambda b, h: (h, 0, 0, 0))],
        out_specs=pl.BlockSpec((None, HEAD_DIM, seq), lambda b, h: (b, h, 0)),
        out_shape=jax.ShapeDtypeStruct((bsz, GW, seq), F32),
        scratch_shapes=[pltpu.VMEM((seq, MOBA_BLOCK), F32), pltpu.VMEM((seq, MOBA_BLOCK), BF16)],
        compiler_params=_params("parallel", "parallel"),
        name="moba",
    )(far, k4, qv_t, qv_t, bias_tiles)


def _t5_bucket(rel):
    n = jnp.maximum(rel, 0)
    max_exact = REL_BUCKETS // 2
    nf = jnp.maximum(n, 1).astype(F32)
    large = max_exact + (jnp.log(nf / max_exact) / math.log(REL_MAX_DIST / max_exact)
                         * (REL_BUCKETS - max_exact)).astype(jnp.int32)
    return jnp.where(n < max_exact, n, jnp.minimum(large, REL_BUCKETS - 1))


def _bucket_tiles():
    kpos = jnp.arange(MOBA_BLOCK)[:, None]
    qpos = jnp.arange(MOBA_BLOCK)[None, :]
    rel = qpos - kpos
    own = jnp.where(rel >= 0, _t5_bucket(rel), -1)
    prev = _t5_bucket(rel + MOBA_BLOCK)
    return jnp.stack([own, prev]).astype(jnp.int32)


def _mixout_kernel(ys5_ref, yml_ref, yrw_ref, bonus_ref, g_ref, ymbt_ref, lnw_ref, lnb_ref, bd_ref,
                   gain_ref, w_ref, gpost_ref, x_ref, o_ref):
    bd = bd_ref[...]
    y = yrw_ref[...]
    mean = _sum3(y, bd) * (1.0 / HEAD_DIM)
    d = y - mean
    var = _sum3(d * d, bd) * (1.0 / HEAD_DIM)
    yrw = (d * lax.rsqrt(var + RW_GN_EPS) * lnw_ref[...] + lnb_ref[...] + bonus_ref[...]) * g_ref[...]
    acc = None
    for i, yg in enumerate((ys5_ref[...], yml_ref[...], yrw, ymbt_ref[...].T)):
        cols = slice(i * GW, (i + 1) * GW)
        part = jnp.dot(_rms(yg, gain_ref[:, cols]).astype(BF16), w_ref[cols, :], preferred_element_type=F32)
        acc = part if acc is None else acc + part
    o_ref[...] = x_ref[...] + _rms(acc, gpost_ref[...])


def _mixout(ys5, yml, yrw, bonus, g, ymb_t, ln_w, ln_b, bd, gain, w, gpost, x2d, seq):
    t = x2d.shape[0]
    tm = min(1024, seq)
    per = seq // tm
    vec = lambda c: _const_spec((1, c))
    return pl.pallas_call(
        _mixout_kernel,
        grid=(t // tm,),
        in_specs=[_row_spec(tm, GW)] * 5
        + [pl.BlockSpec((None, GW, tm), lambda i: (i // per, 0, i % per)),
           vec(GW), vec(GW), _const_spec((GW, GW)), vec(D_MODEL),
           _const_spec((D_MODEL, D_MODEL)), vec(D_MODEL), _row_spec(tm, D_MODEL)],
        out_specs=_row_spec(tm, D_MODEL),
        out_shape=jax.ShapeDtypeStruct((t, D_MODEL), F32),
        compiler_params=_params("parallel"),
        name="mixout",
    )(ys5, yml, yrw, bonus, g, ymb_t, ln_w, ln_b, bd, gain, w, gpost, x2d)


def _kv_kernel(m_ref, g_ref, w_ref, o_ref):
    h = _rms(m_ref[0], g_ref[...]).astype(BF16)
    o_ref[0] = jnp.dot(h, w_ref[...], preferred_element_type=F32).astype(BF16)


def _kv(mem, g, w):
    bsz, m, _ = mem.shape
    return pl.pallas_call(
        _kv_kernel,
        grid=(bsz,),
        in_specs=[pl.BlockSpec((1, m, D_MODEL), lambda b: (b, 0, 0)), _const_spec((1, D_MODEL)),
                  _const_spec((D_MODEL, 2 * D_MODEL))],
        out_specs=pl.BlockSpec((1, m, 2 * D_MODEL), lambda b: (b, 0, 0)),
        out_shape=jax.ShapeDtypeStruct((bsz, m, 2 * D_MODEL), BF16),
        compiler_params=_params("parallel"),
        name="xa_kv",
    )(mem, g, w)


def _xattn_kernel(x_ref, kv_ref, gpre_ref, wq_ref, wo_ref, gpost_ref, o_ref):
    x = x_ref[...]
    q = jnp.dot(_rms(x, gpre_ref[...]).astype(BF16), wq_ref[...], preferred_element_type=F32)
    acc = None
    for hd in range(XA_HEADS):
        cols = slice(hd * XA_HEAD_DIM, (hd + 1) * XA_HEAD_DIM)
        vcols = slice(D_MODEL + hd * XA_HEAD_DIM, D_MODEL + (hd + 1) * XA_HEAD_DIM)
        s = lax.dot_general(q[:, cols].astype(BF16), kv_ref[0, :, cols], NT_DIMS,
                            preferred_element_type=F32) * (XA_HEAD_DIM ** -0.5)
        p = jnp.exp(s - jnp.max(s, axis=-1, keepdims=True))
        o = jnp.dot(p.astype(BF16), kv_ref[0, :, vcols], preferred_element_type=F32)
        o = o / jnp.sum(p, axis=-1, keepdims=True)
        part = jnp.dot(o.astype(BF16), wo_ref[cols, :], preferred_element_type=F32)
        acc = part if acc is None else acc + part
    o_ref[...] = x + _rms(acc, gpost_ref[...])


def _xattn(x2d, kv, gpre, wq, wo, gpost, seq):
    t = x2d.shape[0]
    m = kv.shape[1]
    tm = min(1024, seq)
    per = seq // tm
    vec = _const_spec((1, D_MODEL))
    sq = _const_spec((D_MODEL, D_MODEL))
    return pl.pallas_call(
        _xattn_kernel,
        grid=(t // tm,),
        in_specs=[_row_spec(tm, D_MODEL), pl.BlockSpec((1, m, 2 * D_MODEL), lambda i: (i // per, 0, 0)),
                  vec, sq, sq, vec],
        out_specs=_row_spec(tm, D_MODEL),
        out_shape=jax.ShapeDtypeStruct((t, D_MODEL), F32),
        compiler_params=_params("parallel"),
        name="xattn",
    )(x2d, kv, gpre, wq, wo, gpost)


def _ffn_kernel(x_ref, gpre_ref, wi_ref, wo_ref, gpost_ref, o_ref):
    x = x_ref[...]
    h = _rms(x, gpre_ref[...]).astype(BF16)
    step = FFN_CHUNK
    acc = None
    for c in range(D_FF // step):
        gate = jnp.dot(h, wi_ref[:, c * step:(c + 1) * step], preferred_element_type=F32)
        up = jnp.dot(h, wi_ref[:, D_FF + c * step:D_FF + (c + 1) * step], preferred_element_type=F32)
        act = (gate * _sigmoid(gate) * up).astype(BF16)
        part = jnp.dot(act, wo_ref[c * step:(c + 1) * step, :], preferred_element_type=F32)
        acc = part if acc is None else acc + part
    o_ref[...] = x + _rms(acc, gpost_ref[...])


def _ffn(x2d, gpre, wi, wo, gpost):
    t = x2d.shape[0]
    tm = min(512, t)
    vec = _const_spec((1, D_MODEL))
    once = pl.Buffered(1)
    return pl.pallas_call(
        _ffn_kernel,
        grid=(t // tm,),
        in_specs=[_row_spec(tm, D_MODEL), vec,
                  pl.BlockSpec((D_MODEL, 2 * D_FF), lambda i: (0, 0), pipeline_mode=once),
                  pl.BlockSpec((D_FF, D_MODEL), lambda i: (0, 0), pipeline_mode=once), vec],
        out_specs=_row_spec(tm, D_MODEL),
        out_shape=jax.ShapeDtypeStruct((t, D_MODEL), F32),
        compiler_params=_params("parallel"),
        name="ffn",
    )(x2d, gpre, wi, wo, gpost)


def _block_diag(blocks):
    g, a, b = blocks.shape
    eye = jnp.eye(g, dtype=blocks.dtype)
    return jnp.einsum('gab,gk->gakb', blocks, eye).reshape(g * a, g * b)


def kernel(x, mem, rel_bias, norm_pre_mix, norm_post_mix, norm_pre_xa, norm_post_xa, norm_pre_ffn,
           norm_post_ffn, norm_mem, w_in, mix_out_gain, w_out, s5_lam_re, s5_lam_im, s5_log_step,
           s5_b_re, s5_b_im, s5_c_re, s5_c_im, s5_d, s5_w_glu, s5_b_glu, ml_conv, ml_i_bias, ml_f_bias,
           rw_mu, rw_w0, rw_w2, rw_a0, rw_a2, rw_g2, rw_k_k, rw_k_a, rw_r_k, rw_ln_w, rw_ln_b,
           xa_wq, xa_wkv, xa_wo, ffn_w_in, ffn_w_out):
    bsz, seq, _ = x.shape
    depth = w_in.shape[0]
    t = bsz * seq
    row = lambda a: a.reshape(1, -1)

    ml_end = GW + 4 * GW
    rw_lo = ml_end + 2 * N_HEADS
    mb_lo = rw_lo + 4 * GW
    perm = _rwkv_value_perm(bsz)
    w_proj = jnp.concatenate(
        [w_in[:, :, :ml_end], w_in[:, :, rw_lo:mb_lo], w_in[:, :, rw_lo + 2 * GW:rw_lo + 3 * GW][:, :, perm],
         w_in[:, :, mb_lo + GW:mb_lo + 2 * GW], w_in[:, :, ml_end:rw_lo],
         jnp.zeros((depth, D_MODEL, LANES - 2 * N_HEADS), w_in.dtype)],
        axis=2).astype(BF16)
    w_qv_t = jnp.concatenate([w_in[:, :, mb_lo:mb_lo + GW], w_in[:, :, mb_lo + 2 * GW:mb_lo + 3 * GW]],
                             axis=2).transpose(0, 2, 1).astype(BF16)
    rw_rows = 2 * GW + perm
    w_out_b = jnp.concatenate([w_out[:, :2 * GW], w_out[:, rw_rows], w_out[:, 3 * GW:]], axis=1).astype(BF16)
    gain_p = jnp.concatenate([mix_out_gain[:, :2 * GW], mix_out_gain[:, rw_rows], mix_out_gain[:, 3 * GW:]],
                             axis=1)
    mu_p = jnp.concatenate([rw_mu, rw_mu[:, 2 * GW:3 * GW][:, perm]], axis=1)
    wq_b, wkv_b, wo_b = xa_wq.astype(BF16), xa_wkv.astype(BF16), xa_wo.astype(BF16)
    ffn_wi_b, ffn_wo_b = ffn_w_in.astype(BF16), ffn_w_out.astype(BF16)
    w_glu_b = s5_w_glu.astype(BF16)

    ng = depth * S5_GROUPS
    ab_re, ab_im, bb_re, bb_im = _s5_params(
        s5_lam_re.reshape(ng, S5_STATE), s5_lam_im.reshape(ng, S5_STATE), s5_log_step.reshape(ng, 1),
        s5_b_re.transpose(0, 1, 3, 2).reshape(ng, S5_GROUP, S5_STATE),
        s5_b_im.transpose(0, 1, 3, 2).reshape(ng, S5_GROUP, S5_STATE))
    ab_re = ab_re.reshape(depth, 1, S5_P)
    ab_im = ab_im.reshape(depth, 1, S5_P)
    bb_re = bb_re.reshape(depth, S5_GROUPS, S5_GROUP, S5_STATE)
    bb_im = bb_im.reshape(depth, S5_GROUPS, S5_GROUP, S5_STATE)

    bd = _block_diag(jnp.ones((N_HEADS, HEAD_DIM, HEAD_DIM), BF16))
    head_of_lane = jnp.arange(GW) // HEAD_DIM
    sela = (head_of_lane[:, None] == jnp.arange(LANES)[None, :]).astype(BF16)
    selb = (head_of_lane[:, None] + N_HEADS == jnp.arange(LANES)[None, :]).astype(BF16)
    zeros64 = jnp.zeros((HEAD_DIM, GW), F32)
    head_of_perm = jnp.asarray(perm // HEAD_DIM)
    bd_np = (head_of_lane[:, None] == head_of_perm[None, :]).astype(BF16)
    bd_pp = (head_of_perm[:, None] == head_of_perm[None, :]).astype(BF16)

    bias_tiles = _bias_tiles(rel_bias, _bucket_tiles())
    far = rel_bias[REL_BUCKETS - 1, :]

    x2d = x.reshape(t, D_MODEL)
    for l in range(depth):
        u_s5, p_ml, p_rw, gates, k4, qv_t = _proj(x2d, row(norm_pre_mix[l]), w_proj[l], w_qv_t[l], bsz, seq)

        gate_bias = jnp.concatenate([ml_i_bias[l], ml_f_bias[l], jnp.zeros((LANES - 2 * N_HEADS,), F32)])
        y_ml = _mlstm(p_ml.reshape(bsz, seq, 4 * GW), gates.reshape(bsz, seq, LANES), ml_conv[l],
                      row(gate_bias)).reshape(t, GW)

        w2p = jnp.concatenate([rw_w2[l], zeros64], axis=0)
        a2p = jnp.concatenate([zeros64, rw_a2[l]], axis=0)
        kw, v_rw, c12, bonus, g_rw = _rwkv_pre(
            p_rw.reshape(bsz, seq, RW_COLS), row(mu_p[l]), row(rw_w0[l]), w2p, row(rw_a0[l]), a2p,
            rw_g2[l][:, perm], row(rw_k_k[l]), row(rw_k_a[l]), row(rw_r_k[l]), bd, bd_np, sela, selb)
        y_rw = _rwkv_seq(kw, v_rw, c12).reshape(t, GW)

        y_mb_t = _moba(k4, qv_t, bias_tiles, far)

        wb = jnp.concatenate([_block_diag(bb_re[l]), _block_diag(bb_im[l])], axis=1).astype(BF16)
        wc = jnp.concatenate([_block_diag(s5_c_re[l].transpose(0, 2, 1)),
                              -_block_diag(s5_c_im[l].transpose(0, 2, 1))], axis=0).astype(BF16)
        y_s5 = _s5(u_s5.reshape(bsz, seq, GW), wb, ab_re[l], ab_im[l], wc, row(s5_d[l]), w_glu_b[l],
                   row(s5_b_glu[l])).reshape(t, GW)

        x2d = _mixout(y_s5, y_ml, y_rw, bonus.reshape(t, GW), g_rw.reshape(t, GW), y_mb_t,
                      row(rw_ln_w[l][perm]), row(rw_ln_b[l][perm]), bd_pp, row(gain_p[l]), w_out_b[l],
                      row(norm_post_mix[l]), x2d, seq)

        kv = _kv(mem, row(norm_mem[l]), wkv_b[l])
        x2d = _xattn(x2d, kv, row(norm_pre_xa[l]), wq_b[l], wo_b[l], row(norm_post_xa[l]), seq)
        x2d = _ffn(x2d, row(norm_pre_ffn[l]), ffn_wi_b[l], ffn_wo_b[l], row(norm_post_ffn[l]))
    return x2d.reshape(bsz, seq, D_MODEL)
```

```python
import functools
import math

import jax
import jax.numpy as jnp
import numpy as np
from jax import lax
from jax.experimental import pallas as pl
from jax.experimental.pallas import tpu as pltpu

F32 = jnp.float32
BF16 = jnp.bfloat16
HIGHEST = lax.Precision.HIGHEST

D_MODEL = 1024
N_GROUPS = 4
GW = 256
HEAD_DIM = 64
N_HEADS = 4
S5_GROUP = 16
S5_GROUPS = 16
S5_STATE = 64
S5_P = S5_GROUPS * S5_STATE
CONV_K = 4
ML_CHUNK = 256
RW_GN_EPS = 64e-5
RW_VECS = 5
RW_STEP_VECS = 4
RW_WINDOW = 32
MOBA_BLOCK = 256
MOBA_TOPK = 3
REL_BUCKETS = 32
REL_MAX_DIST = 128
XA_HEADS = 4
XA_HEAD_DIM = 256
D_FF = 2816
FFN_CHUNK = 256
RMS_EPS = 1e-6
LOG2E = 1.4426950408889634
LANES = 128
SUBLANES = 8
RW_COLS = 5 * GW
PROJ_COLS = 6 * GW + RW_COLS + LANES
VMEM_LIMIT = 56 * 1024 * 1024
ROW_TILE = 1024
FFN_ROW_TILE = 512
S5_TIME_TILE = 64
RW_PRE_ROWS = 256

NT_DIMS = (((1,), (1,)), ((), ()))


def _rms(x, g):
    return x * lax.rsqrt(jnp.mean(x * x, axis=-1, keepdims=True) + RMS_EPS) * g


def _bdot(a, b):
    return jnp.dot(a.astype(BF16), b.astype(BF16), preferred_element_type=F32)


def _hdot(a, b):
    return jnp.dot(a, b, precision=HIGHEST, preferred_element_type=F32)


def _sum3(x, ones_b):
    hi = x.astype(BF16)
    r1 = x - hi.astype(F32)
    mid = r1.astype(BF16)
    lo = (r1 - mid.astype(F32)).astype(BF16)
    dot = lambda a: jnp.dot(a, ones_b, preferred_element_type=F32)
    return dot(hi) + dot(mid) + dot(lo)


def _sum3_left(ones_b, x):
    hi = x.astype(BF16)
    r1 = x - hi.astype(F32)
    mid = r1.astype(BF16)
    lo = (r1 - mid.astype(F32)).astype(BF16)
    dot = lambda a: jnp.dot(ones_b, a, preferred_element_type=F32)
    return dot(hi) + dot(mid) + dot(lo)


def _sigmoid(x):
    return 1.0 / (1.0 + jnp.exp(-x))


def _softplus(x):
    return jnp.maximum(x, 0.0) + jnp.log1p(jnp.exp(-jnp.abs(x)))


def _params(*sem):
    return pltpu.CompilerParams(dimension_semantics=sem, vmem_limit_bytes=VMEM_LIMIT)


def _row_spec(tm, cols):
    return pl.BlockSpec((tm, cols), lambda i: (i, 0))


def _const_spec(shape):
    return pl.BlockSpec(shape, lambda *_: (0,) * len(shape))


def _proj_kernel(x_ref, g_ref, w_ref, wt_ref, s5_ref, ml_ref, rw_ref, gt_ref, k4_ref, qvt_ref):
    h = _rms(x_ref[...], g_ref[...]).astype(BF16)

    def mm(lo, hi):
        return jnp.dot(h, w_ref[:, lo:hi], preferred_element_type=F32)

    s5_ref[...] = mm(0, GW)
    ml_ref[...] = mm(GW, 5 * GW)
    rw_ref[...] = mm(5 * GW, 5 * GW + RW_COLS)
    kk = mm(5 * GW + RW_COLS, 6 * GW + RW_COLS)
    for hd in range(N_HEADS):
        k4_ref[hd] = kk[:, hd * HEAD_DIM:(hd + 1) * HEAD_DIM]
    gt_ref[...] = mm(6 * GW + RW_COLS, PROJ_COLS)
    qvt_ref[...] = lax.dot_general(wt_ref[...], h, NT_DIMS, preferred_element_type=F32)


def _proj(x2d, g, w, wt, bsz, seq):
    t = x2d.shape[0]
    tm = min(ROW_TILE, seq)
    per = seq // tm
    widths = (GW, 4 * GW, RW_COLS, LANES)
    return pl.pallas_call(
        _proj_kernel,
        grid=(t // tm,),
        in_specs=[_row_spec(tm, D_MODEL), _const_spec((1, D_MODEL)), _const_spec((D_MODEL, PROJ_COLS)),
                  _const_spec((2 * GW, D_MODEL))],
        out_specs=[_row_spec(tm, c) for c in widths]
        + [pl.BlockSpec((None, N_HEADS, tm, HEAD_DIM), lambda i: (i // per, 0, i % per, 0)),
           pl.BlockSpec((None, 2 * GW, tm), lambda i: (i // per, 0, i % per))],
        out_shape=[jax.ShapeDtypeStruct((t, c), F32) for c in widths]
        + [jax.ShapeDtypeStruct((bsz, N_HEADS, seq, HEAD_DIM), F32),
           jax.ShapeDtypeStruct((bsz, 2 * GW, seq), F32)],
        compiler_params=_params("parallel"),
        name="proj",
    )(x2d, g, w, wt)


def _s5_param_kernel(lr_ref, li_ref, ls_ref, br_ref, bi_ref, abr_ref, abi_ref, bbr_ref, bbi_ref):
    lr, li = lr_ref[...], li_ref[...]
    step = jnp.exp(ls_ref[...])
    mag = jnp.exp(lr * step)
    ang = li * step
    ab_re, ab_im = mag * jnp.cos(ang), mag * jnp.sin(ang)
    den = lr * lr + li * li
    co_re = ((ab_re - 1.0) * lr + ab_im * li) / den
    co_im = (ab_im * lr - (ab_re - 1.0) * li) / den
    abr_ref[...] = ab_re
    abi_ref[...] = ab_im
    br, bi = br_ref[...], bi_ref[...]
    bbr_ref[...] = co_re[:, None, :] * br - co_im[:, None, :] * bi
    bbi_ref[...] = co_re[:, None, :] * bi + co_im[:, None, :] * br


def _s5_params(lam_re, lam_im, log_step, b_re, b_im):
    n = lam_re.shape[0]
    return pl.pallas_call(
        _s5_param_kernel,
        out_shape=[jax.ShapeDtypeStruct((n, S5_STATE), F32)] * 2
        + [jax.ShapeDtypeStruct((n, S5_GROUP, S5_STATE), F32)] * 2,
        name="s5_params",
    )(lam_re, lam_im, log_step, b_re, b_im)


def _s5_kernel(u_ref, wb_ref, are_ref, aim_ref, wc_ref, d_ref, wg_ref, bg_ref, o_ref, xs_ref, st_ref,
               *, tm, nb):
    @pl.when(pl.program_id(0) == 0)
    def _():
        st_ref[...] = jnp.zeros_like(st_ref)

    u = jnp.swapaxes(u_ref[...], 0, 1).reshape(tm * nb, GW)
    xs_ref[...] = _bdot(u, wb_ref[...])
    ch = 256
    for c in range(S5_P // ch):
        re_cols = slice(c * ch, (c + 1) * ch)
        im_cols = slice(S5_P + c * ch, S5_P + (c + 1) * ch)
        a_re = jnp.broadcast_to(are_ref[:, re_cols], (nb, ch))
        a_im = jnp.broadcast_to(aim_ref[:, re_cols], (nb, ch))

        def body(t, carry, re_cols=re_cols, im_cols=im_cols, a_re=a_re, a_im=a_im):
            s_re, s_im = carry
            rows = pl.ds(pl.multiple_of(t * nb, nb), nb)
            n_re = a_re * s_re - a_im * s_im + xs_ref[rows, re_cols]
            n_im = a_re * s_im + a_im * s_re + xs_ref[rows, im_cols]
            xs_ref[rows, re_cols] = n_re
            xs_ref[rows, im_cols] = n_im
            return n_re, n_im

        s_re, s_im = lax.fori_loop(0, tm, body, (st_ref[:, re_cols], st_ref[:, im_cols]))
        st_ref[:, re_cols] = s_re
        st_ref[:, im_cols] = s_im

    y = _bdot(xs_ref[...], wc_ref[...]) + d_ref[...] * u
    z = jax.nn.gelu(y)
    out = z * _sigmoid(_bdot(z, wg_ref[...]) + bg_ref[...])
    o_ref[...] = jnp.swapaxes(out.reshape(tm, nb, GW), 0, 1)


def _s5(u3, wb, a_re, a_im, wc, d, wg, bg):
    nb, seq, _ = u3.shape
    tm = min(S5_TIME_TILE, seq)
    blk = pl.BlockSpec((nb, tm, GW), lambda i: (0, i, 0))
    return pl.pallas_call(
        functools.partial(_s5_kernel, tm=tm, nb=nb),
        grid=(seq // tm,),
        in_specs=[blk, _const_spec((GW, 2 * S5_P)), _const_spec((1, S5_P)),
                  _const_spec((1, S5_P)), _const_spec((2 * S5_P, GW)), _const_spec((1, GW)),
                  _const_spec((GW, GW)), _const_spec((1, GW))],
        out_specs=blk,
        out_shape=jax.ShapeDtypeStruct((nb, seq, GW), F32),
        scratch_shapes=[pltpu.VMEM((nb * tm, 2 * S5_P), F32), pltpu.VMEM((nb, 2 * S5_P), F32)],
        compiler_params=_params("arbitrary"),
        name="s5",
    )(u3, wb, a_re, a_im, wc, d, wg, bg)


def _mlstm_kernel(p_ref, g_ref, cw_ref, gb_ref, o_ref, pad_ref, *, seq):
    cl = ML_CHUNK
    dh = HEAD_DIM
    pad = SUBLANES
    pad_ref[0:pad, :] = jnp.zeros((pad, 2 * GW), F32)
    pad_ref[pad:pad + seq, :] = p_ref[0, :, 0:2 * GW]
    row = lax.broadcasted_iota(jnp.int32, (cl, cl), 0)
    col = lax.broadcasted_iota(jnp.int32, (cl, cl), 1)
    trif = jnp.where(col <= row, 1.0, 0.0).astype(F32)
    causal_t = row <= col
    cw = cw_ref[...]

    def chunk(c, carry):
        cts, n8s, ms = carry
        r0 = pl.multiple_of(c * cl, cl)
        win = pad_ref[pl.ds(r0, cl + pad), :]
        first = pad - (CONV_K - 1)
        conv = cw[0:1, :] * win[first:first + cl, :]
        for j in range(1, CONV_K):
            conv = conv + cw[j:j + 1, :] * win[first + j:first + j + cl, :]
        qk = conv * _sigmoid(conv)
        qt = qk[:, 0:GW].T.astype(BF16)
        kb = (qk[:, GW:2 * GW] * (dh ** -0.5)).astype(BF16)
        vt = p_ref[0, pl.ds(r0, cl), 2 * GW:3 * GW].T.astype(BF16)
        ot = p_ref[0, pl.ds(r0, cl), 3 * GW:4 * GW].T
        gc = g_ref[0, pl.ds(r0, cl), :] + gb_ref[...]
        bcum = _hdot(trif, -_softplus(-gc))
        gct = gc.T
        bct = bcum.T
        outs, cts_n, n8s_n, ms_n = [], [], [], []
        for h in range(N_HEADS):
            hs = slice(h * dh, (h + 1) * dh)
            q_t, k_h, v_t = qt[hs, :], kb[:, hs], vt[hs, :]
            colv = bcum[:, N_HEADS + h:N_HEADS + h + 1] - gc[:, h:h + 1]
            br = bct[N_HEADS + h:N_HEADS + h + 1, :]
            lir = gct[h:h + 1, :]
            m_prev = ms[h]
            dmat = jnp.where(causal_t, br - colv, -jnp.inf)
            b_inter = br + m_prev
            m_comb = jnp.maximum(b_inter, jnp.max(dmat, axis=0, keepdims=True))
            w_inter = jnp.exp(b_inter - m_comb)
            s_t = jnp.exp(dmat - m_comb) * jnp.dot(k_h, q_t, preferred_element_type=F32)
            num = (jnp.dot(v_t, s_t.astype(BF16), preferred_element_type=F32)
                   + w_inter * jnp.dot(cts[h].astype(BF16), q_t, preferred_element_type=F32))
            nq = jnp.dot(n8s[h].astype(BF16), q_t, preferred_element_type=F32)[0:1, :]
            den = jnp.sum(s_t, axis=0, keepdims=True) + w_inter * nq
            hh = num / jnp.maximum(jnp.abs(den), jnp.exp(-m_comb))
            outs.append(_sigmoid(ot[hs, :]) * hh)
            b_last = br[:, cl - 1:cl]
            w_st = b_last - br + lir
            m_in = jnp.max(w_st, axis=1, keepdims=True)
            e_row = jnp.exp(w_st - m_in)
            kv_t = jnp.dot((v_t * e_row).astype(BF16), k_h, preferred_element_type=F32)
            ks8 = jnp.dot(jnp.broadcast_to(e_row, (SUBLANES, cl)).astype(BF16), k_h,
                          preferred_element_type=F32)
            m_new = jnp.maximum(b_last + m_prev, m_in)
            a = jnp.exp(b_last + m_prev - m_new)
            e = jnp.exp(m_in - m_new)
            cts_n.append(a * cts[h] + e * kv_t)
            n8s_n.append(a * n8s[h] + e * ks8)
            ms_n.append(m_new)
        o_ref[0, pl.ds(r0, cl), :] = jnp.concatenate(outs, axis=0).T
        return tuple(cts_n), tuple(n8s_n), tuple(ms_n)

    init = (tuple(jnp.zeros((dh, dh), F32) for _ in range(N_HEADS)),
            tuple(jnp.zeros((SUBLANES, dh), F32) for _ in range(N_HEADS)),
            tuple(jnp.zeros((1, 1), F32) for _ in range(N_HEADS)))
    lax.fori_loop(0, seq // cl, chunk, init)


def _mlstm(p_ml, gates, conv_w, gate_bias):
    bsz, seq, _ = p_ml.shape
    return pl.pallas_call(
        functools.partial(_mlstm_kernel, seq=seq),
        grid=(bsz,),
        in_specs=[pl.BlockSpec((1, seq, 4 * GW), lambda b: (b, 0, 0)),
                  pl.BlockSpec((1, seq, LANES), lambda b: (b, 0, 0)),
                  _const_spec((CONV_K, 2 * GW)), _const_spec((1, LANES))],
        out_specs=pl.BlockSpec((1, seq, GW), lambda b: (b, 0, 0)),
        out_shape=jax.ShapeDtypeStruct((bsz, seq, GW), F32),
        scratch_shapes=[pltpu.VMEM((seq + SUBLANES, 2 * GW), F32)],
        compiler_params=_params("parallel"),
        name="mlstm",
    )(p_ml, gates, conv_w, gate_bias)


def _rwkv_pre_kernel(p_ref, pv_ref, mu_ref, w0_ref, w2_ref, a0_ref, a2_ref, g2_ref, kk_ref, ka_ref, rk_ref,
                     bd_ref, bdp_ref, sela_ref, selb_ref, tril_ref, kw_o, v_o, c_o, bonus_o, g_o):
    x = p_ref[0]
    last = jnp.where(pl.program_id(1) > 0, pv_ref[0][SUBLANES - 1:SUBLANES, :], 0.0)
    rowi = lax.broadcasted_iota(jnp.int32, x.shape, 0)
    prev = jnp.where(rowi == 0, last, pltpu.roll(x, 1, 0))
    p = x + (prev - x) * mu_ref[...]
    r, k = p[:, 0:GW], p[:, GW:2 * GW]
    lo = p[:, 3 * GW:3 * GW + LANES]
    g_lo = p[:, 3 * GW + LANES:4 * GW]
    vp = p[:, 4 * GW:5 * GW]
    w = -_softplus(-(w0_ref[...] + _bdot(jnp.tanh(lo), w2_ref[...]))) - 0.5
    logw = -jnp.exp(w)
    a = _sigmoid(a0_ref[...] + _bdot(lo, a2_ref[...]))
    g_o[0] = _bdot(_sigmoid(g_lo), g2_ref[...])
    kk = k * kk_ref[...]
    kk = kk * lax.rsqrt(_sum3(kk * kk, bd_ref[...]) + 1e-12)
    km = k * (1.0 + (a - 1.0) * ka_ref[...])
    alp = kk * a
    cum = _sum3_left(tril_ref[...], logw)
    g_in = jnp.exp(cum)
    g_inv = jnp.exp(-cum)
    for i, val in enumerate((kk * jnp.exp(cum - logw), r * g_in, alp * g_inv, km * g_inv, g_in)):
        kw_o[0, :, i * GW:(i + 1) * GW] = val
    v_o[0] = vp
    c_o[0] = _sum3(alp * r, sela_ref[...]) + _sum3(km * r, selb_ref[...])
    bonus_o[0] = _sum3(r * km * rk_ref[...], bdp_ref[...]) * vp


def _rwkv_pre(p_rw, mu, w0, w2p, a0, a2p, g2p, k_k, k_a, r_k, bd, bdp, sela, selb):
    bsz, seq, _ = p_rw.shape
    tr = min(RW_PRE_ROWS, seq)
    win = jnp.arange(tr) // min(RW_WINDOW, seq)
    tril = jnp.logical_and(win[:, None] == win[None, :],
                           jnp.arange(tr)[None, :] <= jnp.arange(tr)[:, None]).astype(BF16)
    row = lambda c: pl.BlockSpec((1, tr, c), lambda b, i: (b, i, 0))
    prev = pl.BlockSpec((1, SUBLANES, RW_COLS),
                        lambda b, i: (b, jnp.maximum(i * (tr // SUBLANES) - 1, 0), 0))
    vec = lambda c: _const_spec((1, c))
    widths = (RW_VECS * GW, GW, LANES, GW, GW)
    return pl.pallas_call(
        _rwkv_pre_kernel,
        grid=(bsz, seq // tr),
        in_specs=[row(RW_COLS), prev, vec(RW_COLS), vec(GW), _const_spec((LANES, GW)), vec(GW),
                  _const_spec((LANES, GW)), _const_spec((LANES, GW)), vec(GW), vec(GW), vec(GW),
                  _const_spec((GW, GW)), _const_spec((GW, GW)), _const_spec((GW, LANES)),
                  _const_spec((GW, LANES)), _const_spec((tr, tr))],
        out_specs=[row(c) for c in widths],
        out_shape=[jax.ShapeDtypeStruct((bsz, seq, c), F32) for c in widths],
        compiler_params=_params("parallel", "parallel"),
        name="rwkv_pre",
    )(p_rw, p_rw, mu, w0, w2p, a0, a2p, g2p, k_k, k_a, r_k, bd, bdp, sela, selb, tril)


def _rwkv_value_perm(nb):
    xg = LANES // nb
    vr = HEAD_DIM // xg
    j = np.arange(GW)
    return (j % N_HEADS) * HEAD_DIM + (j // (vr * N_HEADS)) * vr + (j // N_HEADS) % vr


def _rwkv_seq_kernel(kw_ref, v_ref, c_ref, y_ref, s_ref, kt_ref, ka_ref, kb_ref, vs_ref, cs_ref, ys_ref,
                     *, tb, nb):
    xg = LANES // nb
    vr = HEAD_DIM // xg
    pw = N_HEADS * vr

    @pl.when(pl.program_id(0) == 0)
    def _():
        s_ref[...] = jnp.zeros_like(s_ref)

    def tile_steps(x, width):
        n = LANES // width
        x = x.reshape(tb // n, n, x.shape[1], width)
        return jnp.concatenate([x[:, j] for j in range(n)], axis=2)

    kt_ref[...] = jnp.swapaxes(kw_ref[...], 0, 1)
    vt = jnp.swapaxes(v_ref[...], 0, 1)
    vtile = jnp.concatenate([vt[:, :, x * pw:(x + 1) * pw] for x in range(xg)], axis=1)
    vs_ref[...] = jnp.swapaxes(tile_steps(vtile, pw), 1, 2).reshape(tb, pw, LANES)
    cw = 2 * N_HEADS
    ct = jnp.swapaxes(c_ref[...], 0, 1)[:, :, 0:cw]
    ctile = jnp.concatenate([ct] * xg, axis=1)
    cs_ref[...] = jnp.swapaxes(tile_steps(ctile, cw), 1, 2).reshape(tb, cw, LANES)

    def to_lanes(x):
        return jnp.concatenate([x] * xg, axis=0).T

    step_cols = RW_STEP_VECS * GW

    def advance(t, cur_ref, nxt_ref):
        nxt_ref[...] = to_lanes(kt_ref[jnp.minimum(t + 1, tb - 1), :, 0:step_cols])

        for h in range(N_HEADS):
            def vec(i, h=h):
                return cur_ref[i * GW + h * HEAD_DIM:i * GW + (h + 1) * HEAD_DIM][None]

            c1 = cs_ref[t, h:h + 1][None]
            c2 = cs_ref[t, N_HEADS + h:N_HEADS + h + 1][None]
            vv = jnp.stack([vs_ref[t, v * N_HEADS + h:v * N_HEADS + h + 1] for v in range(vr)])
            s = s_ref[:, h]
            sa = -jnp.sum(s * vec(0), axis=1, keepdims=True)
            y0 = jnp.sum(s * vec(1), axis=1, keepdims=True)
            y = y0 + sa * c1 + vv * c2
            s_ref[:, h] = s + sa * vec(2) + vv * vec(3)
            for v in range(vr):
                ys_ref[t, v * N_HEADS + h:v * N_HEADS + h + 1] = y[v]

    ka_ref[...] = to_lanes(kt_ref[0, :, 0:step_cols])

    def step_pair(i, carry):
        advance(2 * i, ka_ref, kb_ref)
        advance(2 * i + 1, kb_ref, ka_ref)
        return carry

    lax.fori_loop(0, tb // 2, step_pair, 0)
    gamma = to_lanes(kt_ref[tb - 1, :, step_cols:RW_VECS * GW])
    s_ref[...] = s_ref[...] * gamma.reshape(N_HEADS, HEAD_DIM, LANES)[None]
    n = LANES // pw
    yt = jnp.swapaxes(ys_ref[...].reshape(tb // n, LANES, LANES), 1, 2)
    pieces = [jnp.concatenate([yt[:, x * nb:(x + 1) * nb, j * pw:(j + 1) * pw] for x in range(xg)], axis=2)
              for j in range(n)]
    y_ref[...] = jnp.swapaxes(jnp.stack(pieces, axis=1).reshape(tb, nb, GW), 0, 1)


def _rwkv_seq(kw, vp, c12):
    nb, seq, _ = kw.shape
    tb = min(RW_WINDOW, seq)
    xg = LANES // nb
    vr = HEAD_DIM // xg
    pw = N_HEADS * vr
    blk = lambda c: pl.BlockSpec((nb, tb, c), lambda i: (0, i, 0))
    return pl.pallas_call(
        functools.partial(_rwkv_seq_kernel, tb=tb, nb=nb),
        grid=(seq // tb,),
        in_specs=[blk(RW_VECS * GW), blk(GW), blk(LANES)],
        out_specs=blk(GW),
        out_shape=jax.ShapeDtypeStruct((nb, seq, GW), F32),
        scratch_shapes=[pltpu.VMEM((vr, N_HEADS, HEAD_DIM, LANES), F32),
                        pltpu.VMEM((tb, nb, RW_VECS * GW), F32),
                        pltpu.VMEM((RW_STEP_VECS * GW, LANES), F32),
                        pltpu.VMEM((RW_STEP_VECS * GW, LANES), F32),
                        pltpu.VMEM((tb, pw, LANES), F32), pltpu.VMEM((tb, 2 * N_HEADS, LANES), F32),
                        pltpu.VMEM((tb, pw, LANES), F32)],
        compiler_params=_params("arbitrary"),
        name="rwkv_seq",
    )(kw, vp, c12)


def _bias_kernel(rb_ref, bk_ref, o_ref):
    h = pl.program_id(0)
    bk = bk_ref[0]
    out = jnp.full(bk.shape, -jnp.inf, F32)
    for b in range(REL_BUCKETS):
        out = jnp.where(bk == b, rb_ref[b, h], out)
    o_ref[0, 0] = out * LOG2E


def _bias_tiles(rel_bias, buckets):
    bs = MOBA_BLOCK
    return pl.pallas_call(
        _bias_kernel,
        grid=(N_HEADS, 2),
        in_specs=[pl.BlockSpec(memory_space=pltpu.SMEM), pl.BlockSpec((1, bs, bs), lambda h, k: (k, 0, 0))],
        out_specs=pl.BlockSpec((1, 1, bs, bs), lambda h, k: (h, k, 0, 0)),
        out_shape=jax.ShapeDtypeStruct((N_HEADS, 2, bs, bs), F32),
        name="moba_bias",
    )(rel_bias, buckets)


def _moba_kernel(far_ref, k_ref, qt_ref, vt_ref, bias_ref, o_ref, s_ref, p_ref, *, seq, nsel):
    bs = MOBA_BLOCK
    nb = seq // bs
    h = pl.program_id(1)
    k = k_ref[...]
    qt = qt_ref[...] * (HEAD_DIM ** -0.5)
    kmean = jnp.mean(k.reshape(nb, bs, HEAD_DIM), axis=1)
    gate = _hdot(kmean, qt)
    jj = lax.broadcasted_iota(jnp.int32, (nb, seq), 0)
    qblk = lax.broadcasted_iota(jnp.int32, (nb, seq), 1) // bs
    g = jnp.where(jj < qblk, gate, -jnp.inf)
    selb = jnp.full((nb, seq), -jnp.inf, F32)
    for r in range(nsel):
        m = jnp.max(g, axis=0, keepdims=True)
        idx = jnp.min(jnp.where(g == m, jj, nb), axis=0, keepdims=True)
        hit = jj == idx
        selb = jnp.where(jnp.logical_and(hit, qblk > r), 0.0, selb)
        g = jnp.where(hit, -jnp.inf, g)
    selfar = selb + far_ref[h] * LOG2E
    kb = k.astype(BF16)
    qtb = (qt * LOG2E).astype(BF16)
    vtb = jnp.concatenate([vt_ref[...].astype(BF16), jnp.ones((SUBLANES, seq), BF16)], axis=0)
    for i in range(nb):
        qs = slice(i * bs, (i + 1) * bs)
        q_i = qtb[:, qs]
        mx = None
        for j in range(i + 1):
            ks = slice(j * bs, (j + 1) * bs)
            s = jnp.dot(kb[ks, :], q_i, preferred_element_type=F32)
            if j == i:
                s = s + bias_ref[0, 0]
            elif j == i - 1:
                s = s + bias_ref[0, 1] + selb[j:j + 1, qs]
            else:
                s = s + selfar[j:j + 1, qs]
            s_ref[ks, :] = s
            mx = s if mx is None else jnp.maximum(mx, s)
        m = jnp.max(mx, axis=0, keepdims=True)
        for j in range(i + 1):
            ks = slice(j * bs, (j + 1) * bs)
            p_ref[ks, :] = jnp.exp2(s_ref[ks, :] - m).astype(BF16)
        acc = jnp.dot(vtb[:, 0:(i + 1) * bs], p_ref[0:(i + 1) * bs, :], preferred_element_type=F32)
        o_ref[:, qs] = acc[0:HEAD_DIM] / acc[HEAD_DIM:HEAD_DIM + 1]


def _moba(k4, qv_t, bias_tiles, far):
    bsz, _, seq, _ = k4.shape
    nsel = max(1, min(MOBA_TOPK, seq // MOBA_BLOCK - 1))
    return pl.pallas_call(
        functools.partial(_moba_kernel, seq=seq, nsel=nsel),
        grid=(bsz, N_HEADS),
        in_specs=[pl.BlockSpec(memory_space=pltpu.SMEM),
                  pl.BlockSpec((None, None, seq, HEAD_DIM), lambda b, h: (b, h, 0, 0)),
                  pl.BlockSpec((None, HEAD_DIM, seq), lambda b, h: (b, h, 0)),
                  pl.BlockSpec((None, HEAD_DIM, seq), lambda b, h: (b, N_HEADS + h, 0)),
                  pl.BlockSpec((1, 2, MOBA_BLOCK, MOBA_BLOCK), lambda b, h: (h, 0, 0, 0))],
        out_specs=pl.BlockSpec((None, HEAD_DIM, seq), lambda b, h: (b, h, 0)),
        out_shape=jax.ShapeDtypeStruct((bsz, GW, seq), F32),
        scratch_shapes=[pltpu.VMEM((seq, MOBA_BLOCK), F32), pltpu.VMEM((seq, MOBA_BLOCK), BF16)],
        compiler_params=_params("parallel", "parallel"),
        name="moba",
    )(far, k4, qv_t, qv_t, bias_tiles)


def _t5_bucket(rel):
    n = jnp.maximum(rel, 0)
    max_exact = REL_BUCKETS // 2
    nf = jnp.maximum(n, 1).astype(F32)
    large = max_exact + (jnp.log(nf / max_exact) / math.log(REL_MAX_DIST / max_exact)
                         * (REL_BUCKETS - max_exact)).astype(jnp.int32)
    return jnp.where(n < max_exact, n, jnp.minimum(large, REL_BUCKETS - 1))


def _bucket_tiles():
    kpos = jnp.arange(MOBA_BLOCK)[:, None]
    qpos = jnp.arange(MOBA_BLOCK)[None, :]
    rel = qpos - kpos
    own = jnp.where(rel >= 0, _t5_bucket(rel), -1)
    prev = _t5_bucket(rel + MOBA_BLOCK)
    return jnp.stack([own, prev]).astype(jnp.int32)


def _mixout_kernel(ys5_ref, yml_ref, yrw_ref, bonus_ref, g_ref, ymbt_ref, lnw_ref, lnb_ref, bd_ref,
                   gain_ref, w_ref, gpost_ref, x_ref, o_ref):
    bd = bd_ref[...]
    y = yrw_ref[...]
    mean = _sum3(y, bd) * (1.0 / HEAD_DIM)
    d = y - mean
    var = _sum3(d * d, bd) * (1.0 / HEAD_DIM)
    yrw = (d * lax.rsqrt(var + RW_GN_EPS) * lnw_ref[...] + lnb_ref[...] + bonus_ref[...]) * g_ref[...]
    acc = None
    for i, yg in enumerate((ys5_ref[...], yml_ref[...], yrw, ymbt_ref[...].T)):
        cols = slice(i * GW, (i + 1) * GW)
        part = jnp.dot(_rms(yg, gain_ref[:, cols]).astype(BF16), w_ref[cols, :], preferred_element_type=F32)
        acc = part if acc is None else acc + part
    o_ref[...] = x_ref[...] + _rms(acc, gpost_ref[...])


def _mixout(ys5, yml, yrw, bonus, g, ymb_t, ln_w, ln_b, bd, gain, w, gpost, x2d, seq):
    t = x2d.shape[0]
    tm = min(ROW_TILE, seq)
    per = seq // tm
    vec = lambda c: _const_spec((1, c))
    return pl.pallas_call(
        _mixout_kernel,
        grid=(t // tm,),
        in_specs=[_row_spec(tm, GW)] * 5
        + [pl.BlockSpec((None, GW, tm), lambda i: (i // per, 0, i % per)),
           vec(GW), vec(GW), _const_spec((GW, GW)), vec(D_MODEL),
           _const_spec((D_MODEL, D_MODEL)), vec(D_MODEL), _row_spec(tm, D_MODEL)],
        out_specs=_row_spec(tm, D_MODEL),
        out_shape=jax.ShapeDtypeStruct((t, D_MODEL), F32),
        compiler_params=_params("parallel"),
        name="mixout",
    )(ys5, yml, yrw, bonus, g, ymb_t, ln_w, ln_b, bd, gain, w, gpost, x2d)


def _kv_kernel(m_ref, g_ref, w_ref, o_ref):
    h = _rms(m_ref[0], g_ref[...]).astype(BF16)
    o_ref[0] = jnp.dot(h, w_ref[...], preferred_element_type=F32).astype(BF16)


def _kv(mem, g, w):
    bsz, m, _ = mem.shape
    return pl.pallas_call(
        _kv_kernel,
        grid=(bsz,),
        in_specs=[pl.BlockSpec((1, m, D_MODEL), lambda b: (b, 0, 0)), _const_spec((1, D_MODEL)),
                  _const_spec((D_MODEL, 2 * D_MODEL))],
        out_specs=pl.BlockSpec((1, m, 2 * D_MODEL), lambda b: (b, 0, 0)),
        out_shape=jax.ShapeDtypeStruct((bsz, m, 2 * D_MODEL), BF16),
        compiler_params=_params("parallel"),
        name="xa_kv",
    )(mem, g, w)


def _xattn_kernel(x_ref, kv_ref, gpre_ref, wq_ref, wo_ref, gpost_ref, o_ref):
    x = x_ref[...]
    q = jnp.dot(_rms(x, gpre_ref[...]).astype(BF16), wq_ref[...], preferred_element_type=F32)
    acc = None
    for hd in range(XA_HEADS):
        cols = slice(hd * XA_HEAD_DIM, (hd + 1) * XA_HEAD_DIM)
        vcols = slice(D_MODEL + hd * XA_HEAD_DIM, D_MODEL + (hd + 1) * XA_HEAD_DIM)
        s = lax.dot_general(q[:, cols].astype(BF16), kv_ref[0, :, cols], NT_DIMS,
                            preferred_element_type=F32) * (XA_HEAD_DIM ** -0.5)
        p = jnp.exp(s - jnp.max(s, axis=-1, keepdims=True))
        o = jnp.dot(p.astype(BF16), kv_ref[0, :, vcols], preferred_element_type=F32)
        o = o / jnp.sum(p, axis=-1, keepdims=True)
        part = jnp.dot(o.astype(BF16), wo_ref[cols, :], preferred_element_type=F32)
        acc = part if acc is None else acc + part
    o_ref[...] = x + _rms(acc, gpost_ref[...])


def _xattn(x2d, kv, gpre, wq, wo, gpost, seq):
    t = x2d.shape[0]
    m = kv.shape[1]
    tm = min(ROW_TILE, seq)
    per = seq // tm
    vec = _const_spec((1, D_MODEL))
    sq = _const_spec((D_MODEL, D_MODEL))
    return pl.pallas_call(
        _xattn_kernel,
        grid=(t // tm,),
        in_specs=[_row_spec(tm, D_MODEL), pl.BlockSpec((1, m, 2 * D_MODEL), lambda i: (i // per, 0, 0)),
                  vec, sq, sq, vec],
        out_specs=_row_spec(tm, D_MODEL),
        out_shape=jax.ShapeDtypeStruct((t, D_MODEL), F32),
        compiler_params=_params("parallel"),
        name="xattn",
    )(x2d, kv, gpre, wq, wo, gpost)


def _ffn_kernel(x_ref, gpre_ref, wi_ref, wo_ref, gpost_ref, o_ref):
    x = x_ref[...]
    h = _rms(x, gpre_ref[...]).astype(BF16)
    step = FFN_CHUNK
    acc = None
    for c in range(D_FF // step):
        gate = jnp.dot(h, wi_ref[:, c * step:(c + 1) * step], preferred_element_type=F32)
        up = jnp.dot(h, wi_ref[:, D_FF + c * step:D_FF + (c + 1) * step], preferred_element_type=F32)
        act = (gate * _sigmoid(gate) * up).astype(BF16)
        part = jnp.dot(act, wo_ref[c * step:(c + 1) * step, :], preferred_element_type=F32)
        acc = part if acc is None else acc + part
    o_ref[...] = x + _rms(acc, gpost_ref[...])


def _ffn(x2d, gpre, wi, wo, gpost):
    t = x2d.shape[0]
    tm = min(FFN_ROW_TILE, t)
    vec = _const_spec((1, D_MODEL))
    once = pl.Buffered(1)
    return pl.pallas_call(
        _ffn_kernel,
        grid=(t // tm,),
        in_specs=[_row_spec(tm, D_MODEL), vec,
                  pl.BlockSpec((D_MODEL, 2 * D_FF), lambda i: (0, 0), pipeline_mode=once),
                  pl.BlockSpec((D_FF, D_MODEL), lambda i: (0, 0), pipeline_mode=once), vec],
        out_specs=_row_spec(tm, D_MODEL),
        out_shape=jax.ShapeDtypeStruct((t, D_MODEL), F32),
        compiler_params=_params("parallel"),
        name="ffn",
    )(x2d, gpre, wi, wo, gpost)


def _block_diag(blocks):
    g, a, b = blocks.shape
    eye = jnp.eye(g, dtype=blocks.dtype)
    return jnp.einsum('gab,gk->gakb', blocks, eye).reshape(g * a, g * b)


def kernel(x, mem, rel_bias, norm_pre_mix, norm_post_mix, norm_pre_xa, norm_post_xa, norm_pre_ffn,
           norm_post_ffn, norm_mem, w_in, mix_out_gain, w_out, s5_lam_re, s5_lam_im, s5_log_step,
           s5_b_re, s5_b_im, s5_c_re, s5_c_im, s5_d, s5_w_glu, s5_b_glu, ml_conv, ml_i_bias, ml_f_bias,
           rw_mu, rw_w0, rw_w2, rw_a0, rw_a2, rw_g2, rw_k_k, rw_k_a, rw_r_k, rw_ln_w, rw_ln_b,
           xa_wq, xa_wkv, xa_wo, ffn_w_in, ffn_w_out):
    bsz, seq, _ = x.shape
    depth = w_in.shape[0]
    t = bsz * seq
    row = lambda a: a.reshape(1, -1)

    ml_end = GW + 4 * GW
    rw_lo = ml_end + 2 * N_HEADS
    mb_lo = rw_lo + 4 * GW
    perm = _rwkv_value_perm(bsz)
    w_proj = jnp.concatenate(
        [w_in[:, :, :ml_end], w_in[:, :, rw_lo:mb_lo], w_in[:, :, rw_lo + 2 * GW:rw_lo + 3 * GW][:, :, perm],
         w_in[:, :, mb_lo + GW:mb_lo + 2 * GW], w_in[:, :, ml_end:rw_lo],
         jnp.zeros((depth, D_MODEL, LANES - 2 * N_HEADS), w_in.dtype)],
        axis=2).astype(BF16)
    w_qv_t = jnp.concatenate([w_in[:, :, mb_lo:mb_lo + GW], w_in[:, :, mb_lo + 2 * GW:mb_lo + 3 * GW]],
                             axis=2).transpose(0, 2, 1).astype(BF16)
    rw_rows = 2 * GW + perm
    w_out_b = jnp.concatenate([w_out[:, :2 * GW], w_out[:, rw_rows], w_out[:, 3 * GW:]], axis=1).astype(BF16)
    gain_p = jnp.concatenate([mix_out_gain[:, :2 * GW], mix_out_gain[:, rw_rows], mix_out_gain[:, 3 * GW:]],
                             axis=1)
    mu_p = jnp.concatenate([rw_mu, rw_mu[:, 2 * GW:3 * GW][:, perm]], axis=1)
    wq_b, wkv_b, wo_b = xa_wq.astype(BF16), xa_wkv.astype(BF16), xa_wo.astype(BF16)
    ffn_wi_b, ffn_wo_b = ffn_w_in.astype(BF16), ffn_w_out.astype(BF16)
    w_glu_b = s5_w_glu.astype(BF16)

    ng = depth * S5_GROUPS
    ab_re, ab_im, bb_re, bb_im = _s5_params(
        s5_lam_re.reshape(ng, S5_STATE), s5_lam_im.reshape(ng, S5_STATE), s5_log_step.reshape(ng, 1),
        s5_b_re.transpose(0, 1, 3, 2).reshape(ng, S5_GROUP, S5_STATE),
        s5_b_im.transpose(0, 1, 3, 2).reshape(ng, S5_GROUP, S5_STATE))
    ab_re = ab_re.reshape(depth, 1, S5_P)
    ab_im = ab_im.reshape(depth, 1, S5_P)
    bb_re = bb_re.reshape(depth, S5_GROUPS, S5_GROUP, S5_STATE)
    bb_im = bb_im.reshape(depth, S5_GROUPS, S5_GROUP, S5_STATE)

    bd = _block_diag(jnp.ones((N_HEADS, HEAD_DIM, HEAD_DIM), BF16))
    head_of_lane = jnp.arange(GW) // HEAD_DIM
    sela = (head_of_lane[:, None] == jnp.arange(LANES)[None, :]).astype(BF16)
    selb = (head_of_lane[:, None] + N_HEADS == jnp.arange(LANES)[None, :]).astype(BF16)
    zeros64 = jnp.zeros((HEAD_DIM, GW), F32)
    head_of_perm = jnp.asarray(perm // HEAD_DIM)
    bd_np = (head_of_lane[:, None] == head_of_perm[None, :]).astype(BF16)
    bd_pp = (head_of_perm[:, None] == head_of_perm[None, :]).astype(BF16)

    bias_tiles = _bias_tiles(rel_bias, _bucket_tiles())
    far = rel_bias[REL_BUCKETS - 1, :]

    x2d = x.reshape(t, D_MODEL)
    for l in range(depth):
        u_s5, p_ml, p_rw, gates, k4, qv_t = _proj(x2d, row(norm_pre_mix[l]), w_proj[l], w_qv_t[l], bsz, seq)

        gate_bias = jnp.concatenate([ml_i_bias[l], ml_f_bias[l], jnp.zeros((LANES - 2 * N_HEADS,), F32)])
        y_ml = _mlstm(p_ml.reshape(bsz, seq, 4 * GW), gates.reshape(bsz, seq, LANES), ml_conv[l],
                      row(gate_bias)).reshape(t, GW)

        w2p = jnp.concatenate([rw_w2[l], zeros64], axis=0)
        a2p = jnp.concatenate([zeros64, rw_a2[l]], axis=0)
        kw, v_rw, c12, bonus, g_rw = _rwkv_pre(
            p_rw.reshape(bsz, seq, RW_COLS), row(mu_p[l]), row(rw_w0[l]), w2p, row(rw_a0[l]), a2p,
            rw_g2[l][:, perm], row(rw_k_k[l]), row(rw_k_a[l]), row(rw_r_k[l]), bd, bd_np, sela, selb)
        y_rw = _rwkv_seq(kw, v_rw, c12).reshape(t, GW)

        y_mb_t = _moba(k4, qv_t, bias_tiles, far)

        wb = jnp.concatenate([_block_diag(bb_re[l]), _block_diag(bb_im[l])], axis=1).astype(BF16)
        wc = jnp.concatenate([_block_diag(s5_c_re[l].transpose(0, 2, 1)),
                              -_block_diag(s5_c_im[l].transpose(0, 2, 1))], axis=0).astype(BF16)
        y_s5 = _s5(u_s5.reshape(bsz, seq, GW), wb, ab_re[l], ab_im[l], wc, row(s5_d[l]), w_glu_b[l],
                   row(s5_b_glu[l])).reshape(t, GW)

        x2d = _mixout(y_s5, y_ml, y_rw, bonus.reshape(t, GW), g_rw.reshape(t, GW), y_mb_t,
                      row(rw_ln_w[l][perm]), row(rw_ln_b[l][perm]), bd_pp, row(gain_p[l]), w_out_b[l],
                      row(norm_post_mix[l]), x2d, seq)

        kv = _kv(mem, row(norm_mem[l]), wkv_b[l])
        x2d = _xattn(x2d, kv, row(norm_pre_xa[l]), wq_b[l], wo_b[l], row(norm_post_xa[l]), seq)
        x2d = _ffn(x2d, row(norm_pre_ffn[l]), ffn_wi_b[l], ffn_wo_b[l], row(norm_post_ffn[l]))
    return x2d.reshape(bsz, seq, D_MODEL)
```

```python
import functools
import math

import jax
import jax.numpy as jnp
import numpy as np
from jax import lax
from jax.experimental import pallas as pl
from jax.experimental.pallas import tpu as pltpu

F32 = jnp.float32
BF16 = jnp.bfloat16
HIGHEST = lax.Precision.HIGHEST

D_MODEL = 1024
N_GROUPS = 4
GW = 256
HEAD_DIM = 64
N_HEADS = 4
S5_GROUP = 16
S5_GROUPS = 16
S5_STATE = 64
S5_P = S5_GROUPS * S5_STATE
CONV_K = 4
ML_CHUNK = 256
RW_GN_EPS = 64e-5
RW_VECS = 5
RW_STEP_VECS = 4
RW_WINDOW = 64
MOBA_BLOCK = 256
MOBA_TOPK = 3
REL_BUCKETS = 32
REL_MAX_DIST = 128
XA_HEADS = 4
XA_HEAD_DIM = 256
D_FF = 2816
FFN_CHUNK = 256
RMS_EPS = 1e-6
LOG2E = 1.4426950408889634
LANES = 128
SUBLANES = 8
RW_COLS = 5 * GW
PROJ_COLS = 6 * GW + RW_COLS + LANES
VMEM_LIMIT = 56 * 1024 * 1024
ROW_TILE = 1024
FFN_ROW_TILE = 512
S5_TIME_TILE = 64
RW_PRE_ROWS = 256

NT_DIMS = (((1,), (1,)), ((), ()))


def _rms(x, g):
    return x * lax.rsqrt(jnp.mean(x * x, axis=-1, keepdims=True) + RMS_EPS) * g


def _bdot(a, b):
    return jnp.dot(a.astype(BF16), b.astype(BF16), preferred_element_type=F32)


def _hdot(a, b):
    return jnp.dot(a, b, precision=HIGHEST, preferred_element_type=F32)


def _sum3(x, ones_b):
    hi = x.astype(BF16)
    r1 = x - hi.astype(F32)
    mid = r1.astype(BF16)
    lo = (r1 - mid.astype(F32)).astype(BF16)
    dot = lambda a: jnp.dot(a, ones_b, preferred_element_type=F32)
    return dot(hi) + dot(mid) + dot(lo)


def _sum3_left(ones_b, x):
    hi = x.astype(BF16)
    r1 = x - hi.astype(F32)
    mid = r1.astype(BF16)
    lo = (r1 - mid.astype(F32)).astype(BF16)
    dot = lambda a: jnp.dot(ones_b, a, preferred_element_type=F32)
    return dot(hi) + dot(mid) + dot(lo)


def _sigmoid(x):
    return 1.0 / (1.0 + jnp.exp(-x))


def _softplus(x):
    return jnp.maximum(x, 0.0) + jnp.log1p(jnp.exp(-jnp.abs(x)))


def _params(*sem):
    return pltpu.CompilerParams(dimension_semantics=sem, vmem_limit_bytes=VMEM_LIMIT)


def _row_spec(tm, cols):
    return pl.BlockSpec((tm, cols), lambda i: (i, 0))


def _const_spec(shape):
    return pl.BlockSpec(shape, lambda *_: (0,) * len(shape))


def _proj_kernel(x_ref, g_ref, w_ref, wt_ref, s5_ref, ml_ref, rw_ref, gt_ref, k4_ref, qvt_ref):
    h = _rms(x_ref[...], g_ref[...]).astype(BF16)

    def mm(lo, hi):
        return jnp.dot(h, w_ref[:, lo:hi], preferred_element_type=F32)

    s5_ref[...] = mm(0, GW)
    ml_ref[...] = mm(GW, 5 * GW)
    rw_ref[...] = mm(5 * GW, 5 * GW + RW_COLS)
    kk = mm(5 * GW + RW_COLS, 6 * GW + RW_COLS)
    for hd in range(N_HEADS):
        k4_ref[hd] = kk[:, hd * HEAD_DIM:(hd + 1) * HEAD_DIM]
    gt_ref[...] = mm(6 * GW + RW_COLS, PROJ_COLS)
    qvt_ref[...] = lax.dot_general(wt_ref[...], h, NT_DIMS, preferred_element_type=F32)


def _proj(x2d, g, w, wt, bsz, seq):
    t = x2d.shape[0]
    tm = min(ROW_TILE, seq)
    per = seq // tm
    widths = (GW, 4 * GW, RW_COLS, LANES)
    return pl.pallas_call(
        _proj_kernel,
        grid=(t // tm,),
        in_specs=[_row_spec(tm, D_MODEL), _const_spec((1, D_MODEL)), _const_spec((D_MODEL, PROJ_COLS)),
                  _const_spec((2 * GW, D_MODEL))],
        out_specs=[_row_spec(tm, c) for c in widths]
        + [pl.BlockSpec((None, N_HEADS, tm, HEAD_DIM), lambda i: (i // per, 0, i % per, 0)),
           pl.BlockSpec((None, 2 * GW, tm), lambda i: (i // per, 0, i % per))],
        out_shape=[jax.ShapeDtypeStruct((t, c), F32) for c in widths]
        + [jax.ShapeDtypeStruct((bsz, N_HEADS, seq, HEAD_DIM), F32),
           jax.ShapeDtypeStruct((bsz, 2 * GW, seq), F32)],
        compiler_params=_params("parallel"),
        name="proj",
    )(x2d, g, w, wt)


def _s5_param_kernel(lr_ref, li_ref, ls_ref, br_ref, bi_ref, abr_ref, abi_ref, bbr_ref, bbi_ref):
    lr, li = lr_ref[...], li_ref[...]
    step = jnp.exp(ls_ref[...])
    mag = jnp.exp(lr * step)
    ang = li * step
    ab_re, ab_im = mag * jnp.cos(ang), mag * jnp.sin(ang)
    den = lr * lr + li * li
    co_re = ((ab_re - 1.0) * lr + ab_im * li) / den
    co_im = (ab_im * lr - (ab_re - 1.0) * li) / den
    abr_ref[...] = ab_re
    abi_ref[...] = ab_im
    br, bi = br_ref[...], bi_ref[...]
    bbr_ref[...] = co_re[:, None, :] * br - co_im[:, None, :] * bi
    bbi_ref[...] = co_re[:, None, :] * bi + co_im[:, None, :] * br


def _s5_params(lam_re, lam_im, log_step, b_re, b_im):
    n = lam_re.shape[0]
    return pl.pallas_call(
        _s5_param_kernel,
        out_shape=[jax.ShapeDtypeStruct((n, S5_STATE), F32)] * 2
        + [jax.ShapeDtypeStruct((n, S5_GROUP, S5_STATE), F32)] * 2,
        name="s5_params",
    )(lam_re, lam_im, log_step, b_re, b_im)


def _s5_kernel(u_ref, wb_ref, are_ref, aim_ref, wc_ref, d_ref, wg_ref, bg_ref, o_ref, xs_ref, st_ref,
               *, tm, nb):
    @pl.when(pl.program_id(0) == 0)
    def _():
        st_ref[...] = jnp.zeros_like(st_ref)

    u = jnp.swapaxes(u_ref[...], 0, 1).reshape(tm * nb, GW)
    xs_ref[...] = _bdot(u, wb_ref[...])
    ch = 512
    for c in range(S5_P // ch):
        re_cols = slice(c * ch, (c + 1) * ch)
        im_cols = slice(S5_P + c * ch, S5_P + (c + 1) * ch)
        a_re = jnp.broadcast_to(are_ref[:, re_cols], (nb, ch))
        a_im = jnp.broadcast_to(aim_ref[:, re_cols], (nb, ch))

        def body(t, carry, re_cols=re_cols, im_cols=im_cols, a_re=a_re, a_im=a_im):
            s_re, s_im = carry
            rows = pl.ds(pl.multiple_of(t * nb, nb), nb)
            n_re = a_re * s_re - a_im * s_im + xs_ref[rows, re_cols]
            n_im = a_re * s_im + a_im * s_re + xs_ref[rows, im_cols]
            xs_ref[rows, re_cols] = n_re
            xs_ref[rows, im_cols] = n_im
            return n_re, n_im

        s_re, s_im = lax.fori_loop(0, tm, body, (st_ref[:, re_cols], st_ref[:, im_cols]))
        st_ref[:, re_cols] = s_re
        st_ref[:, im_cols] = s_im

    y = _bdot(xs_ref[...], wc_ref[...]) + d_ref[...] * u
    z = jax.nn.gelu(y)
    out = z * _sigmoid(_bdot(z, wg_ref[...]) + bg_ref[...])
    o_ref[...] = jnp.swapaxes(out.reshape(tm, nb, GW), 0, 1)


def _s5(u3, wb, a_re, a_im, wc, d, wg, bg):
    nb, seq, _ = u3.shape
    tm = min(S5_TIME_TILE, seq)
    blk = pl.BlockSpec((nb, tm, GW), lambda i: (0, i, 0))
    return pl.pallas_call(
        functools.partial(_s5_kernel, tm=tm, nb=nb),
        grid=(seq // tm,),
        in_specs=[blk, _const_spec((GW, 2 * S5_P)), _const_spec((1, S5_P)),
                  _const_spec((1, S5_P)), _const_spec((2 * S5_P, GW)), _const_spec((1, GW)),
                  _const_spec((GW, GW)), _const_spec((1, GW))],
        out_specs=blk,
        out_shape=jax.ShapeDtypeStruct((nb, seq, GW), F32),
        scratch_shapes=[pltpu.VMEM((nb * tm, 2 * S5_P), F32), pltpu.VMEM((nb, 2 * S5_P), F32)],
        compiler_params=_params("arbitrary"),
        name="s5",
    )(u3, wb, a_re, a_im, wc, d, wg, bg)


def _mlstm_kernel(p_ref, g_ref, cw_ref, gb_ref, o_ref, pad_ref, *, seq):
    cl = ML_CHUNK
    dh = HEAD_DIM
    pad = SUBLANES
    pad_ref[0:pad, :] = jnp.zeros((pad, 2 * GW), F32)
    pad_ref[pad:pad + seq, :] = p_ref[0, :, 0:2 * GW]
    row = lax.broadcasted_iota(jnp.int32, (cl, cl), 0)
    col = lax.broadcasted_iota(jnp.int32, (cl, cl), 1)
    trif = jnp.where(col <= row, 1.0, 0.0).astype(BF16)
    causal_t = row <= col
    cw = cw_ref[...]

    def chunk(c, carry):
        cts, n8s, ms = carry
        r0 = pl.multiple_of(c * cl, cl)
        win = pad_ref[pl.ds(r0, cl + pad), :]
        first = pad - (CONV_K - 1)
        conv = cw[0:1, :] * win[first:first + cl, :]
        for j in range(1, CONV_K):
            conv = conv + cw[j:j + 1, :] * win[first + j:first + j + cl, :]
        qk = conv * _sigmoid(conv)
        qt = qk[:, 0:GW].T.astype(BF16)
        kb = (qk[:, GW:2 * GW] * (dh ** -0.5)).astype(BF16)
        vt = p_ref[0, pl.ds(r0, cl), 2 * GW:3 * GW].T.astype(BF16)
        ot = p_ref[0, pl.ds(r0, cl), 3 * GW:4 * GW].T
        gc = g_ref[0, pl.ds(r0, cl), :] + gb_ref[...]
        bcum = _sum3_left(trif, -_softplus(-gc))
        gct = gc.T
        bct = bcum.T
        outs, cts_n, n8s_n, ms_n = [], [], [], []
        for h in range(N_HEADS):
            hs = slice(h * dh, (h + 1) * dh)
            q_t, k_h, v_t = qt[hs, :], kb[:, hs], vt[hs, :]
            colv = bcum[:, N_HEADS + h:N_HEADS + h + 1] - gc[:, h:h + 1]
            br = bct[N_HEADS + h:N_HEADS + h + 1, :]
            lir = gct[h:h + 1, :]
            m_prev = ms[h]
            dmat = jnp.where(causal_t, br - colv, -jnp.inf)
            b_inter = br + m_prev
            m_comb = jnp.maximum(b_inter, jnp.max(dmat, axis=0, keepdims=True))
            w_inter = jnp.exp(b_inter - m_comb)
            s_t = jnp.exp(dmat - m_comb) * jnp.dot(k_h, q_t, preferred_element_type=F32)
            num = (jnp.dot(v_t, s_t.astype(BF16), preferred_element_type=F32)
                   + w_inter * jnp.dot(cts[h].astype(BF16), q_t, preferred_element_type=F32))
            nq = jnp.dot(n8s[h].astype(BF16), q_t, preferred_element_type=F32)[0:1, :]
            den = jnp.sum(s_t, axis=0, keepdims=True) + w_inter * nq
            hh = num / jnp.maximum(jnp.abs(den), jnp.exp(-m_comb))
            outs.append(_sigmoid(ot[hs, :]) * hh)
            b_last = br[:, cl - 1:cl]
            w_st = b_last - br + lir
            m_in = jnp.max(w_st, axis=1, keepdims=True)
            e_row = jnp.exp(w_st - m_in)
            kv_t = jnp.dot((v_t * e_row).astype(BF16), k_h, preferred_element_type=F32)
            ks8 = jnp.dot(jnp.broadcast_to(e_row, (SUBLANES, cl)).astype(BF16), k_h,
                          preferred_element_type=F32)
            m_new = jnp.maximum(b_last + m_prev, m_in)
            a = jnp.exp(b_last + m_prev - m_new)
            e = jnp.exp(m_in - m_new)
            cts_n.append(a * cts[h] + e * kv_t)
            n8s_n.append(a * n8s[h] + e * ks8)
            ms_n.append(m_new)
        o_ref[0, pl.ds(r0, cl), :] = jnp.concatenate(outs, axis=0).T
        return tuple(cts_n), tuple(n8s_n), tuple(ms_n)

    init = (tuple(jnp.zeros((dh, dh), F32) for _ in range(N_HEADS)),
            tuple(jnp.zeros((SUBLANES, dh), F32) for _ in range(N_HEADS)),
            tuple(jnp.zeros((1, 1), F32) for _ in range(N_HEADS)))
    lax.fori_loop(0, seq // cl, chunk, init)


def _mlstm(p_ml, gates, conv_w, gate_bias):
    bsz, seq, _ = p_ml.shape
    return pl.pallas_call(
        functools.partial(_mlstm_kernel, seq=seq),
        grid=(bsz,),
        in_specs=[pl.BlockSpec((1, seq, 4 * GW), lambda b: (b, 0, 0)),
                  pl.BlockSpec((1, seq, LANES), lambda b: (b, 0, 0)),
                  _const_spec((CONV_K, 2 * GW)), _const_spec((1, LANES))],
        out_specs=pl.BlockSpec((1, seq, GW), lambda b: (b, 0, 0)),
        out_shape=jax.ShapeDtypeStruct((bsz, seq, GW), F32),
        scratch_shapes=[pltpu.VMEM((seq + SUBLANES, 2 * GW), F32)],
        compiler_params=_params("parallel"),
        name="mlstm",
    )(p_ml, gates, conv_w, gate_bias)


def _rwkv_pre_kernel(p_ref, pv_ref, mu_ref, w0_ref, w2_ref, a0_ref, a2_ref, g2_ref, kk_ref, ka_ref, rk_ref,
                     bd_ref, bdp_ref, sela_ref, selb_ref, tril_ref, kw_o, v_o, c_o, bonus_o, g_o):
    x = p_ref[0]
    last = jnp.where(pl.program_id(1) > 0, pv_ref[0][SUBLANES - 1:SUBLANES, :], 0.0)
    rowi = lax.broadcasted_iota(jnp.int32, x.shape, 0)
    prev = jnp.where(rowi == 0, last, pltpu.roll(x, 1, 0))
    p = x + (prev - x) * mu_ref[...]
    r, k = p[:, 0:GW], p[:, GW:2 * GW]
    lo = p[:, 3 * GW:3 * GW + LANES]
    g_lo = p[:, 3 * GW + LANES:4 * GW]
    vp = p[:, 4 * GW:5 * GW]
    w = -_softplus(-(w0_ref[...] + _bdot(jnp.tanh(lo), w2_ref[...]))) - 0.5
    logw = -jnp.exp(w)
    a = _sigmoid(a0_ref[...] + _bdot(lo, a2_ref[...]))
    g_o[0] = _bdot(_sigmoid(g_lo), g2_ref[...])
    kk = k * kk_ref[...]
    kk = kk * lax.rsqrt(_sum3(kk * kk, bd_ref[...]) + 1e-12)
    km = k * (1.0 + (a - 1.0) * ka_ref[...])
    alp = kk * a
    cum = _sum3_left(tril_ref[...], logw)
    g_in = jnp.exp(cum)
    g_inv = jnp.exp(-cum)
    for i, val in enumerate((kk * jnp.exp(cum - logw), r * g_in, alp * g_inv, km * g_inv, g_in)):
        kw_o[0, :, i * GW:(i + 1) * GW] = val
    v_o[0] = vp
    c_o[0] = _sum3(alp * r, sela_ref[...]) + _sum3(km * r, selb_ref[...])
    bonus_o[0] = _sum3(r * km * rk_ref[...], bdp_ref[...]) * vp


def _rwkv_pre(p_rw, mu, w0, w2p, a0, a2p, g2p, k_k, k_a, r_k, bd, bdp, sela, selb):
    bsz, seq, _ = p_rw.shape
    tr = min(RW_PRE_ROWS, seq)
    win = jnp.arange(tr) // min(RW_WINDOW, seq)
    tril = jnp.logical_and(win[:, None] == win[None, :],
                           jnp.arange(tr)[None, :] <= jnp.arange(tr)[:, None]).astype(BF16)
    row = lambda c: pl.BlockSpec((1, tr, c), lambda b, i: (b, i, 0))
    prev = pl.BlockSpec((1, SUBLANES, RW_COLS),
                        lambda b, i: (b, jnp.maximum(i * (tr // SUBLANES) - 1, 0), 0))
    vec = lambda c: _const_spec((1, c))
    widths = (RW_VECS * GW, GW, LANES, GW, GW)
    return pl.pallas_call(
        _rwkv_pre_kernel,
        grid=(bsz, seq // tr),
        in_specs=[row(RW_COLS), prev, vec(RW_COLS), vec(GW), _const_spec((LANES, GW)), vec(GW),
                  _const_spec((LANES, GW)), _const_spec((LANES, GW)), vec(GW), vec(GW), vec(GW),
                  _const_spec((GW, GW)), _const_spec((GW, GW)), _const_spec((GW, LANES)),
                  _const_spec((GW, LANES)), _const_spec((tr, tr))],
        out_specs=[row(c) for c in widths],
        out_shape=[jax.ShapeDtypeStruct((bsz, seq, c), F32) for c in widths],
        compiler_params=_params("parallel", "parallel"),
        name="rwkv_pre",
    )(p_rw, p_rw, mu, w0, w2p, a0, a2p, g2p, k_k, k_a, r_k, bd, bdp, sela, selb, tril)


def _rwkv_value_perm(nb):
    xg = LANES // nb
    vr = HEAD_DIM // xg
    j = np.arange(GW)
    return (j % N_HEADS) * HEAD_DIM + (j // (vr * N_HEADS)) * vr + (j // N_HEADS) % vr


def _rwkv_seq_kernel(kw_ref, v_ref, c_ref, y_ref, s_ref, kt_ref, ka_ref, kb_ref, vs_ref, cs_ref, ys_ref,
                     *, tb, nb):
    xg = LANES // nb
    vr = HEAD_DIM // xg
    pw = N_HEADS * vr

    @pl.when(pl.program_id(0) == 0)
    def _():
        s_ref[...] = jnp.zeros_like(s_ref)

    def tile_steps(x, width):
        n = LANES // width
        x = x.reshape(tb // n, n, x.shape[1], width)
        return jnp.concatenate([x[:, j] for j in range(n)], axis=2)

    kt_ref[...] = jnp.swapaxes(kw_ref[...], 0, 1)
    vt = jnp.swapaxes(v_ref[...], 0, 1)
    vtile = jnp.concatenate([vt[:, :, x * pw:(x + 1) * pw] for x in range(xg)], axis=1)
    vs_ref[...] = jnp.swapaxes(tile_steps(vtile, pw), 1, 2).reshape(tb, pw, LANES)
    cw = 2 * N_HEADS
    ct = jnp.swapaxes(c_ref[...], 0, 1)[:, :, 0:cw]
    ctile = jnp.concatenate([ct] * xg, axis=1)
    cs_ref[...] = jnp.swapaxes(tile_steps(ctile, cw), 1, 2).reshape(tb, cw, LANES)

    def to_lanes(x):
        return jnp.concatenate([x] * xg, axis=0).T

    step_cols = RW_STEP_VECS * GW

    def advance(t, cur_ref, nxt_ref):
        nxt_ref[...] = to_lanes(kt_ref[jnp.minimum(t + 1, tb - 1), :, 0:step_cols])

        for h in range(N_HEADS):
            def vec(i, h=h):
                return cur_ref[i * GW + h * HEAD_DIM:i * GW + (h + 1) * HEAD_DIM][None]

            c1 = cs_ref[t, h:h + 1][None]
            c2 = cs_ref[t, N_HEADS + h:N_HEADS + h + 1][None]
            vv = jnp.stack([vs_ref[t, v * N_HEADS + h:v * N_HEADS + h + 1] for v in range(vr)])
            s = s_ref[:, h]
            sa = -jnp.sum(s * vec(0), axis=1, keepdims=True)
            y0 = jnp.sum(s * vec(1), axis=1, keepdims=True)
            y = y0 + sa * c1 + vv * c2
            s_ref[:, h] = s + sa * vec(2) + vv * vec(3)
            for v in range(vr):
                ys_ref[t, v * N_HEADS + h:v * N_HEADS + h + 1] = y[v]

    ka_ref[...] = to_lanes(kt_ref[0, :, 0:step_cols])

    def step_pair(i, carry):
        advance(2 * i, ka_ref, kb_ref)
        advance(2 * i + 1, kb_ref, ka_ref)
        return carry

    lax.fori_loop(0, tb // 2, step_pair, 0)
    gamma = to_lanes(kt_ref[tb - 1, :, step_cols:RW_VECS * GW])
    s_ref[...] = s_ref[...] * gamma.reshape(N_HEADS, HEAD_DIM, LANES)[None]
    n = LANES // pw
    yt = jnp.swapaxes(ys_ref[...].reshape(tb // n, LANES, LANES), 1, 2)
    pieces = [jnp.concatenate([yt[:, x * nb:(x + 1) * nb, j * pw:(j + 1) * pw] for x in range(xg)], axis=2)
              for j in range(n)]
    y_ref[...] = jnp.swapaxes(jnp.stack(pieces, axis=1).reshape(tb, nb, GW), 0, 1)


def _rwkv_seq(kw, vp, c12):
    nb, seq, _ = kw.shape
    tb = min(RW_WINDOW, seq)
    xg = LANES // nb
    vr = HEAD_DIM // xg
    pw = N_HEADS * vr
    blk = lambda c: pl.BlockSpec((nb, tb, c), lambda i: (0, i, 0))
    return pl.pallas_call(
        functools.partial(_rwkv_seq_kernel, tb=tb, nb=nb),
        grid=(seq // tb,),
        in_specs=[blk(RW_VECS * GW), blk(GW), blk(LANES)],
        out_specs=blk(GW),
        out_shape=jax.ShapeDtypeStruct((nb, seq, GW), F32),
        scratch_shapes=[pltpu.VMEM((vr, N_HEADS, HEAD_DIM, LANES), F32),
                        pltpu.VMEM((tb, nb, RW_VECS * GW), F32),
                        pltpu.VMEM((RW_STEP_VECS * GW, LANES), F32),
                        pltpu.VMEM((RW_STEP_VECS * GW, LANES), F32),
                        pltpu.VMEM((tb, pw, LANES), F32), pltpu.VMEM((tb, 2 * N_HEADS, LANES), F32),
                        pltpu.VMEM((tb, pw, LANES), F32)],
        compiler_params=_params("arbitrary"),
        name="rwkv_seq",
    )(kw, vp, c12)


def _bias_kernel(rb_ref, bk_ref, o_ref):
    h = pl.program_id(0)
    bk = bk_ref[0]
    out = jnp.full(bk.shape, -jnp.inf, F32)
    for b in range(REL_BUCKETS):
        out = jnp.where(bk == b, rb_ref[b, h], out)
    o_ref[0, 0] = out * LOG2E


def _bias_tiles(rel_bias, buckets):
    bs = MOBA_BLOCK
    return pl.pallas_call(
        _bias_kernel,
        grid=(N_HEADS, 2),
        in_specs=[pl.BlockSpec(memory_space=pltpu.SMEM), pl.BlockSpec((1, bs, bs), lambda h, k: (k, 0, 0))],
        out_specs=pl.BlockSpec((1, 1, bs, bs), lambda h, k: (h, k, 0, 0)),
        out_shape=jax.ShapeDtypeStruct((N_HEADS, 2, bs, bs), F32),
        name="moba_bias",
    )(rel_bias, buckets)


def _moba_kernel(far_ref, k_ref, qt_ref, vt_ref, bias_ref, o_ref, s_ref, p_ref, *, seq, nsel):
    bs = MOBA_BLOCK
    nb = seq // bs
    h = pl.program_id(1)
    k = k_ref[...]
    qt = qt_ref[...] * (HEAD_DIM ** -0.5)
    kmean = jnp.mean(k.reshape(nb, bs, HEAD_DIM), axis=1)
    gate = _hdot(kmean, qt)
    jj = lax.broadcasted_iota(jnp.int32, (nb, seq), 0)
    qblk = lax.broadcasted_iota(jnp.int32, (nb, seq), 1) // bs
    g = jnp.where(jj < qblk, gate, -jnp.inf)
    selb = jnp.full((nb, seq), -jnp.inf, F32)
    for r in range(nsel):
        m = jnp.max(g, axis=0, keepdims=True)
        idx = jnp.min(jnp.where(g == m, jj, nb), axis=0, keepdims=True)
        hit = jj == idx
        selb = jnp.where(jnp.logical_and(hit, qblk > r), 0.0, selb)
        g = jnp.where(hit, -jnp.inf, g)
    selfar = selb + far_ref[h] * LOG2E
    kb = k.astype(BF16)
    qtb = (qt * LOG2E).astype(BF16)
    vtb = jnp.concatenate([vt_ref[...].astype(BF16), jnp.ones((SUBLANES, seq), BF16)], axis=0)
    for i in range(nb):
        qs = slice(i * bs, (i + 1) * bs)
        q_i = qtb[:, qs]
        mx = None
        for j in range(i + 1):
            ks = slice(j * bs, (j + 1) * bs)
            s = jnp.dot(kb[ks, :], q_i, preferred_element_type=F32)
            if j == i:
                s = s + bias_ref[0, 0]
            elif j == i - 1:
                s = s + bias_ref[0, 1] + selb[j:j + 1, qs]
            else:
                s = s + selfar[j:j + 1, qs]
            s_ref[ks, :] = s
            mx = s if mx is None else jnp.maximum(mx, s)
        m = jnp.max(mx, axis=0, keepdims=True)
        for j in range(i + 1):
            ks = slice(j * bs, (j + 1) * bs)
            p_ref[ks, :] = jnp.exp2(s_ref[ks, :] - m).astype(BF16)
        acc = jnp.dot(vtb[:, 0:(i + 1) * bs], p_ref[0:(i + 1) * bs, :], preferred_element_type=F32)
        o_ref[:, qs] = acc[0:HEAD_DIM] / acc[HEAD_DIM:HEAD_DIM + 1]


def _moba(k4, qv_t, bias_tiles, far):
    bsz, _, seq, _ = k4.shape
    nsel = max(1, min(MOBA_TOPK, seq // MOBA_BLOCK - 1))
    return pl.pallas_call(
        functools.partial(_moba_kernel, seq=seq, nsel=nsel),
        grid=(bsz, N_HEADS),
        in_specs=[pl.BlockSpec(memory_space=pltpu.SMEM),
                  pl.BlockSpec((None, None, seq, HEAD_DIM), lambda b, h: (b, h, 0, 0)),
                  pl.BlockSpec((None, HEAD_DIM, seq), lambda b, h: (b, h, 0)),
                  pl.BlockSpec((None, HEAD_DIM, seq), lambda b, h: (b, N_HEADS + h, 0)),
                  pl.BlockSpec((1, 2, MOBA_BLOCK, MOBA_BLOCK), lambda b, h: (h, 0, 0, 0))],
        out_specs=pl.BlockSpec((None, HEAD_DIM, seq), lambda b, h: (b, h, 0)),
        out_shape=jax.ShapeDtypeStruct((bsz, GW, seq), F32),
        scratch_shapes=[pltpu.VMEM((seq, MOBA_BLOCK), F32), pltpu.VMEM((seq, MOBA_BLOCK), BF16)],
        compiler_params=_params("parallel", "parallel"),
        name="moba",
    )(far, k4, qv_t, qv_t, bias_tiles)


def _t5_bucket(rel):
    n = jnp.maximum(rel, 0)
    max_exact = REL_BUCKETS // 2
    nf = jnp.maximum(n, 1).astype(F32)
    large = max_exact + (jnp.log(nf / max_exact) / math.log(REL_MAX_DIST / max_exact)
                         * (REL_BUCKETS - max_exact)).astype(jnp.int32)
    return jnp.where(n < max_exact, n, jnp.minimum(large, REL_BUCKETS - 1))


def _bucket_tiles():
    kpos = jnp.arange(MOBA_BLOCK)[:, None]
    qpos = jnp.arange(MOBA_BLOCK)[None, :]
    rel = qpos - kpos
    own = jnp.where(rel >= 0, _t5_bucket(rel), -1)
    prev = _t5_bucket(rel + MOBA_BLOCK)
    return jnp.stack([own, prev]).astype(jnp.int32)


def _mixout_kernel(ys5_ref, yml_ref, yrw_ref, bonus_ref, g_ref, ymbt_ref, lnw_ref, lnb_ref, bd_ref,
                   gain_ref, w_ref, gpost_ref, x_ref, o_ref):
    bd = bd_ref[...]
    y = yrw_ref[...]
    mean = _sum3(y, bd) * (1.0 / HEAD_DIM)
    d = y - mean
    var = _sum3(d * d, bd) * (1.0 / HEAD_DIM)
    yrw = (d * lax.rsqrt(var + RW_GN_EPS) * lnw_ref[...] + lnb_ref[...] + bonus_ref[...]) * g_ref[...]
    acc = None
    for i, yg in enumerate((ys5_ref[...], yml_ref[...], yrw, ymbt_ref[...].T)):
        cols = slice(i * GW, (i + 1) * GW)
        part = jnp.dot(_rms(yg, gain_ref[:, cols]).astype(BF16), w_ref[cols, :], preferred_element_type=F32)
        acc = part if acc is None else acc + part
    o_ref[...] = x_ref[...] + _rms(acc, gpost_ref[...])


def _mixout(ys5, yml, yrw, bonus, g, ymb_t, ln_w, ln_b, bd, gain, w, gpost, x2d, seq):
    t = x2d.shape[0]
    tm = min(ROW_TILE, seq)
    per = seq // tm
    vec = lambda c: _const_spec((1, c))
    return pl.pallas_call(
        _mixout_kernel,
        grid=(t // tm,),
        in_specs=[_row_spec(tm, GW)] * 5
        + [pl.BlockSpec((None, GW, tm), lambda i: (i // per, 0, i % per)),
           vec(GW), vec(GW), _const_spec((GW, GW)), vec(D_MODEL),
           _const_spec((D_MODEL, D_MODEL)), vec(D_MODEL), _row_spec(tm, D_MODEL)],
        out_specs=_row_spec(tm, D_MODEL),
        out_shape=jax.ShapeDtypeStruct((t, D_MODEL), F32),
        compiler_params=_params("parallel"),
        name="mixout",
    )(ys5, yml, yrw, bonus, g, ymb_t, ln_w, ln_b, bd, gain, w, gpost, x2d)


def _kv_kernel(m_ref, g_ref, w_ref, o_ref):
    h = _rms(m_ref[0], g_ref[...]).astype(BF16)
    o_ref[0] = jnp.dot(h, w_ref[...], preferred_element_type=F32).astype(BF16)


def _kv(mem, g, w):
    bsz, m, _ = mem.shape
    return pl.pallas_call(
        _kv_kernel,
        grid=(bsz,),
        in_specs=[pl.BlockSpec((1, m, D_MODEL), lambda b: (b, 0, 0)), _const_spec((1, D_MODEL)),
                  _const_spec((D_MODEL, 2 * D_MODEL))],
        out_specs=pl.BlockSpec((1, m, 2 * D_MODEL), lambda b: (b, 0, 0)),
        out_shape=jax.ShapeDtypeStruct((bsz, m, 2 * D_MODEL), BF16),
        compiler_params=_params("parallel"),
        name="xa_kv",
    )(mem, g, w)


def _xattn_kernel(x_ref, kv_ref, gpre_ref, wq_ref, wo_ref, gpost_ref, o_ref):
    x = x_ref[...]
    q = jnp.dot(_rms(x, gpre_ref[...]).astype(BF16), wq_ref[...], preferred_element_type=F32)
    acc = None
    for hd in range(XA_HEADS):
        cols = slice(hd * XA_HEAD_DIM, (hd + 1) * XA_HEAD_DIM)
        vcols = slice(D_MODEL + hd * XA_HEAD_DIM, D_MODEL + (hd + 1) * XA_HEAD_DIM)
        s = lax.dot_general(q[:, cols].astype(BF16), kv_ref[0, :, cols], NT_DIMS,
                            preferred_element_type=F32) * (XA_HEAD_DIM ** -0.5)
        p = jnp.exp(s - jnp.max(s, axis=-1, keepdims=True))
        o = jnp.dot(p.astype(BF16), kv_ref[0, :, vcols], preferred_element_type=F32)
        o = o / jnp.sum(p, axis=-1, keepdims=True)
        part = jnp.dot(o.astype(BF16), wo_ref[cols, :], preferred_element_type=F32)
        acc = part if acc is None else acc + part
    o_ref[...] = x + _rms(acc, gpost_ref[...])


def _xattn(x2d, kv, gpre, wq, wo, gpost, seq):
    t = x2d.shape[0]
    m = kv.shape[1]
    tm = min(ROW_TILE, seq)
    per = seq // tm
    vec = _const_spec((1, D_MODEL))
    sq = _const_spec((D_MODEL, D_MODEL))
    return pl.pallas_call(
        _xattn_kernel,
        grid=(t // tm,),
        in_specs=[_row_spec(tm, D_MODEL), pl.BlockSpec((1, m, 2 * D_MODEL), lambda i: (i // per, 0, 0)),
                  vec, sq, sq, vec],
        out_specs=_row_spec(tm, D_MODEL),
        out_shape=jax.ShapeDtypeStruct((t, D_MODEL), F32),
        compiler_params=_params("parallel"),
        name="xattn",
    )(x2d, kv, gpre, wq, wo, gpost)


def _ffn_kernel(x_ref, gpre_ref, wi_ref, wo_ref, gpost_ref, o_ref):
    x = x_ref[...]
    h = _rms(x, gpre_ref[...]).astype(BF16)
    step = FFN_CHUNK
    acc = None
    for c in range(D_FF // step):
        gate = jnp.dot(h, wi_ref[:, c * step:(c + 1) * step], preferred_element_type=F32)
        up = jnp.dot(h, wi_ref[:, D_FF + c * step:D_FF + (c + 1) * step], preferred_element_type=F32)
        act = (gate * _sigmoid(gate) * up).astype(BF16)
        part = jnp.dot(act, wo_ref[c * step:(c + 1) * step, :], preferred_element_type=F32)
        acc = part if acc is None else acc + part
    o_ref[...] = x + _rms(acc, gpost_ref[...])


def _ffn(x2d, gpre, wi, wo, gpost):
    t = x2d.shape[0]
    tm = min(FFN_ROW_TILE, t)
    vec = _const_spec((1, D_MODEL))
    once = pl.Buffered(1)
    return pl.pallas_call(
        _ffn_kernel,
        grid=(t // tm,),
        in_specs=[_row_spec(tm, D_MODEL), vec,
                  pl.BlockSpec((D_MODEL, 2 * D_FF), lambda i: (0, 0), pipeline_mode=once),
                  pl.BlockSpec((D_FF, D_MODEL), lambda i: (0, 0), pipeline_mode=once), vec],
        out_specs=_row_spec(tm, D_MODEL),
        out_shape=jax.ShapeDtypeStruct((t, D_MODEL), F32),
        compiler_params=_params("parallel"),
        name="ffn",
    )(x2d, gpre, wi, wo, gpost)


def _block_diag(blocks):
    g, a, b = blocks.shape
    eye = jnp.eye(g, dtype=blocks.dtype)
    return jnp.einsum('gab,gk->gakb', blocks, eye).reshape(g * a, g * b)


def kernel(x, mem, rel_bias, norm_pre_mix, norm_post_mix, norm_pre_xa, norm_post_xa, norm_pre_ffn,
           norm_post_ffn, norm_mem, w_in, mix_out_gain, w_out, s5_lam_re, s5_lam_im, s5_log_step,
           s5_b_re, s5_b_im, s5_c_re, s5_c_im, s5_d, s5_w_glu, s5_b_glu, ml_conv, ml_i_bias, ml_f_bias,
           rw_mu, rw_w0, rw_w2, rw_a0, rw_a2, rw_g2, rw_k_k, rw_k_a, rw_r_k, rw_ln_w, rw_ln_b,
           xa_wq, xa_wkv, xa_wo, ffn_w_in, ffn_w_out):
    bsz, seq, _ = x.shape
    depth = w_in.shape[0]
    t = bsz * seq
    row = lambda a: a.reshape(1, -1)

    ml_end = GW + 4 * GW
    rw_lo = ml_end + 2 * N_HEADS
    mb_lo = rw_lo + 4 * GW
    perm = _rwkv_value_perm(bsz)
    w_proj = jnp.concatenate(
        [w_in[:, :, :ml_end], w_in[:, :, rw_lo:mb_lo], w_in[:, :, rw_lo + 2 * GW:rw_lo + 3 * GW][:, :, perm],
         w_in[:, :, mb_lo + GW:mb_lo + 2 * GW], w_in[:, :, ml_end:rw_lo],
         jnp.zeros((depth, D_MODEL, LANES - 2 * N_HEADS), w_in.dtype)],
        axis=2).astype(BF16)
    w_qv_t = jnp.concatenate([w_in[:, :, mb_lo:mb_lo + GW], w_in[:, :, mb_lo + 2 * GW:mb_lo + 3 * GW]],
                             axis=2).transpose(0, 2, 1).astype(BF16)
    rw_rows = 2 * GW + perm
    w_out_b = jnp.concatenate([w_out[:, :2 * GW], w_out[:, rw_rows], w_out[:, 3 * GW:]], axis=1).astype(BF16)
    gain_p = jnp.concatenate([mix_out_gain[:, :2 * GW], mix_out_gain[:, rw_rows], mix_out_gain[:, 3 * GW:]],
                             axis=1)
    mu_p = jnp.concatenate([rw_mu, rw_mu[:, 2 * GW:3 * GW][:, perm]], axis=1)
    wq_b, wkv_b, wo_b = xa_wq.astype(BF16), xa_wkv.astype(BF16), xa_wo.astype(BF16)
    ffn_wi_b, ffn_wo_b = ffn_w_in.astype(BF16), ffn_w_out.astype(BF16)
    w_glu_b = s5_w_glu.astype(BF16)

    ng = depth * S5_GROUPS
    ab_re, ab_im, bb_re, bb_im = _s5_params(
        s5_lam_re.reshape(ng, S5_STATE), s5_lam_im.reshape(ng, S5_STATE), s5_log_step.reshape(ng, 1),
        s5_b_re.transpose(0, 1, 3, 2).reshape(ng, S5_GROUP, S5_STATE),
        s5_b_im.transpose(0, 1, 3, 2).reshape(ng, S5_GROUP, S5_STATE))
    ab_re = ab_re.reshape(depth, 1, S5_P)
    ab_im = ab_im.reshape(depth, 1, S5_P)
    bb_re = bb_re.reshape(depth, S5_GROUPS, S5_GROUP, S5_STATE)
    bb_im = bb_im.reshape(depth, S5_GROUPS, S5_GROUP, S5_STATE)

    bd = _block_diag(jnp.ones((N_HEADS, HEAD_DIM, HEAD_DIM), BF16))
    head_of_lane = jnp.arange(GW) // HEAD_DIM
    sela = (head_of_lane[:, None] == jnp.arange(LANES)[None, :]).astype(BF16)
    selb = (head_of_lane[:, None] + N_HEADS == jnp.arange(LANES)[None, :]).astype(BF16)
    zeros64 = jnp.zeros((HEAD_DIM, GW), F32)
    head_of_perm = jnp.asarray(perm // HEAD_DIM)
    bd_np = (head_of_lane[:, None] == head_of_perm[None, :]).astype(BF16)
    bd_pp = (head_of_perm[:, None] == head_of_perm[None, :]).astype(BF16)

    bias_tiles = _bias_tiles(rel_bias, _bucket_tiles())
    far = rel_bias[REL_BUCKETS - 1, :]

    x2d = x.reshape(t, D_MODEL)
    for l in range(depth):
        u_s5, p_ml, p_rw, gates, k4, qv_t = _proj(x2d, row(norm_pre_mix[l]), w_proj[l], w_qv_t[l], bsz, seq)

        gate_bias = jnp.concatenate([ml_i_bias[l], ml_f_bias[l], jnp.zeros((LANES - 2 * N_HEADS,), F32)])
        y_ml = _mlstm(p_ml.reshape(bsz, seq, 4 * GW), gates.reshape(bsz, seq, LANES), ml_conv[l],
                      row(gate_bias)).reshape(t, GW)

        w2p = jnp.concatenate([rw_w2[l], zeros64], axis=0)
        a2p = jnp.concatenate([zeros64, rw_a2[l]], axis=0)
        kw, v_rw, c12, bonus, g_rw = _rwkv_pre(
            p_rw.reshape(bsz, seq, RW_COLS), row(mu_p[l]), row(rw_w0[l]), w2p, row(rw_a0[l]), a2p,
            rw_g2[l][:, perm], row(rw_k_k[l]), row(rw_k_a[l]), row(rw_r_k[l]), bd, bd_np, sela, selb)
        y_rw = _rwkv_seq(kw, v_rw, c12).reshape(t, GW)

        y_mb_t = _moba(k4, qv_t, bias_tiles, far)

        wb = jnp.concatenate([_block_diag(bb_re[l]), _block_diag(bb_im[l])], axis=1).astype(BF16)
        wc = jnp.concatenate([_block_diag(s5_c_re[l].transpose(0, 2, 1)),
                              -_block_diag(s5_c_im[l].transpose(0, 2, 1))], axis=0).astype(BF16)
        y_s5 = _s5(u_s5.reshape(bsz, seq, GW), wb, ab_re[l], ab_im[l], wc, row(s5_d[l]), w_glu_b[l],
                   row(s5_b_glu[l])).reshape(t, GW)

        x2d = _mixout(y_s5, y_ml, y_rw, bonus.reshape(t, GW), g_rw.reshape(t, GW), y_mb_t,
                      row(rw_ln_w[l][perm]), row(rw_ln_b[l][perm]), bd_pp, row(gain_p[l]), w_out_b[l],
                      row(norm_post_mix[l]), x2d, seq)

        kv = _kv(mem, row(norm_mem[l]), wkv_b[l])
        x2d = _xattn(x2d, kv, row(norm_pre_xa[l]), wq_b[l], wo_b[l], row(norm_post_xa[l]), seq)
        x2d = _ffn(x2d, row(norm_pre_ffn[l]), ffn_wi_b[l], ffn_wo_b[l], row(norm_post_ffn[l]))
    return x2d.reshape(bsz, seq, D_MODEL)
```

```python
import functools
import math

import jax
import jax.numpy as jnp
import numpy as np
from jax import lax
from jax.experimental import pallas as pl
from jax.experimental.pallas import tpu as pltpu

F32 = jnp.float32
BF16 = jnp.bfloat16
HIGHEST = lax.Precision.HIGHEST

D_MODEL = 1024
GW = 256
HEAD_DIM = 64
N_HEADS = 4
S5_GROUP = 16
S5_GROUPS = 16
S5_STATE = 64
S5_P = S5_GROUPS * S5_STATE
CONV_K = 4
ML_CHUNK = 256
RW_GN_EPS = 64e-5
RW_VECS = 5
RW_STEP_VECS = 4
RW_WINDOW = 64
MOBA_BLOCK = 256
MOBA_TOPK = 3
REL_BUCKETS = 32
REL_MAX_DIST = 128
XA_HEADS = 4
XA_HEAD_DIM = 256
D_FF = 2816
FFN_CHUNK = 256
RMS_EPS = 1e-6
LOG2E = 1.4426950408889634
LANES = 128
SUBLANES = 8
RW_COLS = 5 * GW
PROJ_COLS = 6 * GW + RW_COLS + LANES
VMEM_LIMIT = 56 * 1024 * 1024
ROW_TILE = 1024
FFN_ROW_TILE = 512
S5_TIME_TILE = 64
RW_PRE_ROWS = 256

NT_DIMS = (((1,), (1,)), ((), ()))


def _rms(x, g):
    return x * lax.rsqrt(jnp.mean(x * x, axis=-1, keepdims=True) + RMS_EPS) * g


def _bdot(a, b):
    return jnp.dot(a.astype(BF16), b.astype(BF16), preferred_element_type=F32)


def _hdot(a, b):
    return jnp.dot(a, b, precision=HIGHEST, preferred_element_type=F32)


def _sum3(x, ones_b):
    hi = x.astype(BF16)
    r1 = x - hi.astype(F32)
    mid = r1.astype(BF16)
    lo = (r1 - mid.astype(F32)).astype(BF16)
    dot = lambda a: jnp.dot(a, ones_b, preferred_element_type=F32)
    return dot(hi) + dot(mid) + dot(lo)


def _sum2(x, ones_b):
    hi = x.astype(BF16)
    mid = (x - hi.astype(F32)).astype(BF16)
    dot = lambda a: jnp.dot(a, ones_b, preferred_element_type=F32)
    return dot(hi) + dot(mid)


def _sum3_left(ones_b, x):
    hi = x.astype(BF16)
    r1 = x - hi.astype(F32)
    mid = r1.astype(BF16)
    lo = (r1 - mid.astype(F32)).astype(BF16)
    dot = lambda a: jnp.dot(ones_b, a, preferred_element_type=F32)
    return dot(hi) + dot(mid) + dot(lo)


def _sigmoid(x):
    return 1.0 / (1.0 + jnp.exp(-x))


def _softplus(x):
    return jnp.maximum(x, 0.0) + jnp.log1p(jnp.exp(-jnp.abs(x)))


def _params(*sem):
    return pltpu.CompilerParams(dimension_semantics=sem, vmem_limit_bytes=VMEM_LIMIT)


def _row_spec(tm, cols):
    return pl.BlockSpec((tm, cols), lambda i: (i, 0))


def _const_spec(shape):
    return pl.BlockSpec(shape, lambda *_: (0,) * len(shape))


def _proj_kernel(x_ref, g_ref, w_ref, wt_ref, s5_ref, ml_ref, rw_ref, gt_ref, k4_ref, qvt_ref):
    h = _rms(x_ref[...], g_ref[...]).astype(BF16)

    def mm(lo, hi):
        return jnp.dot(h, w_ref[:, lo:hi], preferred_element_type=F32)

    s5_ref[...] = mm(0, GW)
    ml_ref[...] = mm(GW, 5 * GW)
    rw_ref[...] = mm(5 * GW, 5 * GW + RW_COLS)
    kk = mm(5 * GW + RW_COLS, 6 * GW + RW_COLS)
    for hd in range(N_HEADS):
        k4_ref[hd] = kk[:, hd * HEAD_DIM:(hd + 1) * HEAD_DIM]
    gt_ref[...] = mm(6 * GW + RW_COLS, PROJ_COLS)
    qvt_ref[...] = lax.dot_general(wt_ref[...], h, NT_DIMS, preferred_element_type=F32)


def _proj(x2d, g, w, wt, bsz, seq):
    t = x2d.shape[0]
    tm = min(ROW_TILE, seq)
    per = seq // tm
    widths = (GW, 4 * GW, RW_COLS, LANES)
    return pl.pallas_call(
        _proj_kernel,
        grid=(t // tm,),
        in_specs=[_row_spec(tm, D_MODEL), _const_spec((1, D_MODEL)), _const_spec((D_MODEL, PROJ_COLS)),
                  _const_spec((2 * GW, D_MODEL))],
        out_specs=[_row_spec(tm, c) for c in widths]
        + [pl.BlockSpec((None, N_HEADS, tm, HEAD_DIM), lambda i: (i // per, 0, i % per, 0)),
           pl.BlockSpec((None, 2 * GW, tm), lambda i: (i // per, 0, i % per))],
        out_shape=[jax.ShapeDtypeStruct((t, c), F32) for c in widths]
        + [jax.ShapeDtypeStruct((bsz, N_HEADS, seq, HEAD_DIM), F32),
           jax.ShapeDtypeStruct((bsz, 2 * GW, seq), F32)],
        compiler_params=_params("parallel"),
        name="proj",
    )(x2d, g, w, wt)


def _s5_param_kernel(lr_ref, li_ref, ls_ref, br_ref, bi_ref, abr_ref, abi_ref, bbr_ref, bbi_ref):
    lr, li = lr_ref[...], li_ref[...]
    step = jnp.exp(ls_ref[...])
    mag = jnp.exp(lr * step)
    ang = li * step
    ab_re, ab_im = mag * jnp.cos(ang), mag * jnp.sin(ang)
    den = lr * lr + li * li
    co_re = ((ab_re - 1.0) * lr + ab_im * li) / den
    co_im = (ab_im * lr - (ab_re - 1.0) * li) / den
    abr_ref[...] = ab_re
    abi_ref[...] = ab_im
    br, bi = br_ref[...], bi_ref[...]
    bbr_ref[...] = co_re[:, None, :] * br - co_im[:, None, :] * bi
    bbi_ref[...] = co_re[:, None, :] * bi + co_im[:, None, :] * br


def _s5_params(lam_re, lam_im, log_step, b_re, b_im):
    n = lam_re.shape[0]
    return pl.pallas_call(
        _s5_param_kernel,
        out_shape=[jax.ShapeDtypeStruct((n, S5_STATE), F32)] * 2
        + [jax.ShapeDtypeStruct((n, S5_GROUP, S5_STATE), F32)] * 2,
        name="s5_params",
    )(lam_re, lam_im, log_step, b_re, b_im)


def _s5_kernel(u_ref, wb_ref, are_ref, aim_ref, wc_ref, d_ref, wg_ref, bg_ref, o_ref, xs_ref, st_ref,
               *, tm, nb):
    @pl.when(pl.program_id(0) == 0)
    def _():
        st_ref[...] = jnp.zeros_like(st_ref)

    u = jnp.swapaxes(u_ref[...], 0, 1).reshape(tm * nb, GW)
    xs_ref[...] = _bdot(u, wb_ref[...])
    ch = 512
    for c in range(S5_P // ch):
        re_cols = slice(c * ch, (c + 1) * ch)
        im_cols = slice(S5_P + c * ch, S5_P + (c + 1) * ch)
        a_re = jnp.broadcast_to(are_ref[:, re_cols], (nb, ch))
        a_im = jnp.broadcast_to(aim_ref[:, re_cols], (nb, ch))

        def body(t, carry, re_cols=re_cols, im_cols=im_cols, a_re=a_re, a_im=a_im):
            s_re, s_im = carry
            rows = pl.ds(pl.multiple_of(t * nb, nb), nb)
            n_re = a_re * s_re - a_im * s_im + xs_ref[rows, re_cols]
            n_im = a_re * s_im + a_im * s_re + xs_ref[rows, im_cols]
            xs_ref[rows, re_cols] = n_re
            xs_ref[rows, im_cols] = n_im
            return n_re, n_im

        s_re, s_im = lax.fori_loop(0, tm, body, (st_ref[:, re_cols], st_ref[:, im_cols]))
        st_ref[:, re_cols] = s_re
        st_ref[:, im_cols] = s_im

    y = _bdot(xs_ref[...], wc_ref[...]) + d_ref[...] * u
    z = jax.nn.gelu(y)
    out = z * _sigmoid(_bdot(z, wg_ref[...]) + bg_ref[...])
    o_ref[...] = jnp.swapaxes(out.reshape(tm, nb, GW), 0, 1)


def _s5(u3, wb, a_re, a_im, wc, d, wg, bg):
    nb, seq, _ = u3.shape
    tm = min(S5_TIME_TILE, seq)
    blk = pl.BlockSpec((nb, tm, GW), lambda i: (0, i, 0))
    return pl.pallas_call(
        functools.partial(_s5_kernel, tm=tm, nb=nb),
        grid=(seq // tm,),
        in_specs=[blk, _const_spec((GW, 2 * S5_P)), _const_spec((1, S5_P)),
                  _const_spec((1, S5_P)), _const_spec((2 * S5_P, GW)), _const_spec((1, GW)),
                  _const_spec((GW, GW)), _const_spec((1, GW))],
        out_specs=blk,
        out_shape=jax.ShapeDtypeStruct((nb, seq, GW), F32),
        scratch_shapes=[pltpu.VMEM((nb * tm, 2 * S5_P), F32), pltpu.VMEM((nb, 2 * S5_P), F32)],
        compiler_params=_params("arbitrary"),
        name="s5",
    )(u3, wb, a_re, a_im, wc, d, wg, bg)


def _mlstm_kernel(p_ref, g_ref, cw_ref, gb_ref, o_ref, pad_ref, *, seq):
    cl = ML_CHUNK
    dh = HEAD_DIM
    pad = SUBLANES
    pad_ref[0:pad, :] = jnp.zeros((pad, 2 * GW), F32)
    pad_ref[pad:pad + seq, :] = p_ref[0, :, 0:2 * GW]
    row = lax.broadcasted_iota(jnp.int32, (cl, cl), 0)
    col = lax.broadcasted_iota(jnp.int32, (cl, cl), 1)
    trif = jnp.where(col <= row, 1.0, 0.0).astype(BF16)
    causal_t = row <= col
    cw = cw_ref[...]

    def chunk(c, carry):
        cts, n8s, ms = carry
        r0 = pl.multiple_of(c * cl, cl)
        win = pad_ref[pl.ds(r0, cl + pad), :]
        conv = cw[CONV_K - 1:CONV_K, :] * win[pad:pad + cl, :]
        for j in range(CONV_K - 1):
            back = pltpu.roll(win, CONV_K - 1 - j, 0)[pad:pad + cl, :]
            conv = conv + cw[j:j + 1, :] * back
        qk = conv * _sigmoid(conv)
        qt = qk[:, 0:GW].T.astype(BF16)
        kb = (qk[:, GW:2 * GW] * (dh ** -0.5)).astype(BF16)
        vt = p_ref[0, pl.ds(r0, cl), 2 * GW:3 * GW].T.astype(BF16)
        ot = p_ref[0, pl.ds(r0, cl), 3 * GW:4 * GW].T
        gc = g_ref[0, pl.ds(r0, cl), :] + gb_ref[...]
        bcum = _sum3_left(trif, -_softplus(-gc))
        gct = gc.T
        bct = bcum.T
        outs, cts_n, n8s_n, ms_n = [], [], [], []
        for h in range(N_HEADS):
            hs = slice(h * dh, (h + 1) * dh)
            q_t, k_h, v_t = qt[hs, :], kb[:, hs], vt[hs, :]
            colv = bcum[:, N_HEADS + h:N_HEADS + h + 1] - gc[:, h:h + 1]
            br = bct[N_HEADS + h:N_HEADS + h + 1, :]
            lir = gct[h:h + 1, :]
            m_prev = ms[h]
            dmat = jnp.where(causal_t, br - colv, -jnp.inf)
            b_inter = br + m_prev
            m_comb = jnp.maximum(b_inter, jnp.max(dmat, axis=0, keepdims=True))
            w_inter = jnp.exp(b_inter - m_comb)
            s_t = jnp.exp(dmat - m_comb) * jnp.dot(k_h, q_t, preferred_element_type=F32)
            num = (jnp.dot(v_t, s_t.astype(BF16), preferred_element_type=F32)
                   + w_inter * jnp.dot(cts[h].astype(BF16), q_t, preferred_element_type=F32))
            nq = jnp.dot(n8s[h].astype(BF16), q_t, preferred_element_type=F32)[0:1, :]
            den = jnp.sum(s_t, axis=0, keepdims=True) + w_inter * nq
            hh = num / jnp.maximum(jnp.abs(den), jnp.exp(-m_comb))
            outs.append(_sigmoid(ot[hs, :]) * hh)
            b_last = br[:, cl - 1:cl]
            w_st = b_last - br + lir
            m_in = jnp.max(w_st, axis=1, keepdims=True)
            e_row = jnp.exp(w_st - m_in)
            kv_t = jnp.dot((v_t * e_row).astype(BF16), k_h, preferred_element_type=F32)
            ks8 = jnp.dot(jnp.broadcast_to(e_row, (SUBLANES, cl)).astype(BF16), k_h,
                          preferred_element_type=F32)
            m_new = jnp.maximum(b_last + m_prev, m_in)
            a = jnp.exp(b_last + m_prev - m_new)
            e = jnp.exp(m_in - m_new)
            cts_n.append(a * cts[h] + e * kv_t)
            n8s_n.append(a * n8s[h] + e * ks8)
            ms_n.append(m_new)
        o_ref[0, pl.ds(r0, cl), :] = jnp.concatenate(outs, axis=0).T
        return tuple(cts_n), tuple(n8s_n), tuple(ms_n)

    init = (tuple(jnp.zeros((dh, dh), F32) for _ in range(N_HEADS)),
            tuple(jnp.zeros((SUBLANES, dh), F32) for _ in range(N_HEADS)),
            tuple(jnp.zeros((1, 1), F32) for _ in range(N_HEADS)))
    lax.fori_loop(0, seq // cl, chunk, init)


def _mlstm(p_ml, gates, conv_w, gate_bias):
    bsz, seq, _ = p_ml.shape
    return pl.pallas_call(
        functools.partial(_mlstm_kernel, seq=seq),
        grid=(bsz,),
        in_specs=[pl.BlockSpec((1, seq, 4 * GW), lambda b: (b, 0, 0)),
                  pl.BlockSpec((1, seq, LANES), lambda b: (b, 0, 0)),
                  _const_spec((CONV_K, 2 * GW)), _const_spec((1, LANES))],
        out_specs=pl.BlockSpec((1, seq, GW), lambda b: (b, 0, 0)),
        out_shape=jax.ShapeDtypeStruct((bsz, seq, GW), F32),
        scratch_shapes=[pltpu.VMEM((seq + SUBLANES, 2 * GW), F32)],
        compiler_params=_params("parallel"),
        name="mlstm",
    )(p_ml, gates, conv_w, gate_bias)


def _rwkv_pre_kernel(p_ref, pv_ref, mu_ref, w0_ref, w2_ref, a0_ref, a2_ref, g2_ref, kk_ref, ka_ref, rk_ref,
                     bd_ref, bdp_ref, sela_ref, selb_ref, tril_ref, kw_o, v_o, c_o, bonus_o, g_o):
    x = p_ref[0]
    last = jnp.where(pl.program_id(1) > 0, pv_ref[0][SUBLANES - 1:SUBLANES, :], 0.0)
    rowi = lax.broadcasted_iota(jnp.int32, x.shape, 0)
    prev = jnp.where(rowi == 0, last, pltpu.roll(x, 1, 0))
    p = x + (prev - x) * mu_ref[...]
    r, k = p[:, 0:GW], p[:, GW:2 * GW]
    lo = p[:, 3 * GW:3 * GW + LANES]
    g_lo = p[:, 3 * GW + LANES:4 * GW]
    vp = p[:, 4 * GW:5 * GW]
    w = -_softplus(-(w0_ref[...] + _bdot(jnp.tanh(lo), w2_ref[...]))) - 0.5
    logw = -jnp.exp(w)
    a = _sigmoid(a0_ref[...] + _bdot(lo, a2_ref[...]))
    g_o[0] = _bdot(_sigmoid(g_lo), g2_ref[...])
    kk = k * kk_ref[...]
    kk = kk * lax.rsqrt(_sum3(kk * kk, bd_ref[...]) + 1e-12)
    km = k * (1.0 + (a - 1.0) * ka_ref[...])
    alp = kk * a
    cum = _sum3_left(tril_ref[...], logw)
    g_in = jnp.exp(cum)
    g_inv = jnp.exp(-cum)
    for i, val in enumerate((kk * jnp.exp(cum - logw), r * g_in, alp * g_inv, km * g_inv, g_in)):
        kw_o[0, :, i * GW:(i + 1) * GW] = val
    v_o[0] = vp
    c_o[0] = _bdot(alp * r, sela_ref[...]) + _bdot(km * r, selb_ref[...])
    bonus_o[0] = _sum3(r * km * rk_ref[...], bdp_ref[...]) * vp


def _rwkv_pre(p_rw, mu, w0, w2p, a0, a2p, g2p, k_k, k_a, r_k, bd, bdp, sela, selb):
    bsz, seq, _ = p_rw.shape
    tr = min(RW_PRE_ROWS, seq)
    win = jnp.arange(tr) // min(RW_WINDOW, seq)
    tril = jnp.logical_and(win[:, None] == win[None, :],
                           jnp.arange(tr)[None, :] <= jnp.arange(tr)[:, None]).astype(BF16)
    row = lambda c: pl.BlockSpec((1, tr, c), lambda b, i: (b, i, 0))
    prev = pl.BlockSpec((1, SUBLANES, RW_COLS),
                        lambda b, i: (b, jnp.maximum(i * (tr // SUBLANES) - 1, 0), 0))
    vec = lambda c: _const_spec((1, c))
    widths = (RW_VECS * GW, GW, LANES, GW, GW)
    return pl.pallas_call(
        _rwkv_pre_kernel,
        grid=(bsz, seq // tr),
        in_specs=[row(RW_COLS), prev, vec(RW_COLS), vec(GW), _const_spec((LANES, GW)), vec(GW),
                  _const_spec((LANES, GW)), _const_spec((LANES, GW)), vec(GW), vec(GW), vec(GW),
                  _const_spec((GW, GW)), _const_spec((GW, GW)), _const_spec((GW, LANES)),
                  _const_spec((GW, LANES)), _const_spec((tr, tr))],
        out_specs=[row(c) for c in widths],
        out_shape=[jax.ShapeDtypeStruct((bsz, seq, c), F32) for c in widths],
        compiler_params=_params("parallel", "parallel"),
        name="rwkv_pre",
    )(p_rw, p_rw, mu, w0, w2p, a0, a2p, g2p, k_k, k_a, r_k, bd, bdp, sela, selb, tril)


def _rwkv_value_perm(nb):
    xg = LANES // nb
    vr = HEAD_DIM // xg
    j = np.arange(GW)
    return (j % N_HEADS) * HEAD_DIM + (j // (vr * N_HEADS)) * vr + (j // N_HEADS) % vr


def _rwkv_seq_kernel(kw_ref, v_ref, c_ref, y_ref, s_ref, kt_ref, ka_ref, kb_ref, vs_ref, cs_ref, ys_ref,
                     *, tb, nb):
    xg = LANES // nb
    vr = HEAD_DIM // xg
    pw = N_HEADS * vr

    @pl.when(pl.program_id(0) == 0)
    def _():
        s_ref[...] = jnp.zeros_like(s_ref)

    def tile_steps(x, width):
        n = LANES // width
        x = x.reshape(tb // n, n, x.shape[1], width)
        return jnp.concatenate([x[:, j] for j in range(n)], axis=2)

    kt_ref[...] = jnp.swapaxes(kw_ref[...], 0, 1)
    vt = jnp.swapaxes(v_ref[...], 0, 1)
    vtile = jnp.concatenate([vt[:, :, x * pw:(x + 1) * pw] for x in range(xg)], axis=1)
    vs_ref[...] = jnp.swapaxes(tile_steps(vtile, pw), 1, 2).reshape(tb, pw, LANES)
    cw = 2 * N_HEADS
    ct = jnp.swapaxes(c_ref[...], 0, 1)[:, :, 0:cw]
    ctile = jnp.concatenate([ct] * xg, axis=1)
    cs_ref[...] = jnp.swapaxes(tile_steps(ctile, cw), 1, 2).reshape(tb, cw, LANES)

    def to_lanes(x):
        return jnp.concatenate([x] * xg, axis=0).T

    step_cols = RW_STEP_VECS * GW

    def advance(t, cur_ref, nxt_ref):
        nxt_ref[...] = to_lanes(kt_ref[jnp.minimum(t + 1, tb - 1), :, 0:step_cols])

        for h in range(N_HEADS):
            def vec(i, h=h):
                return cur_ref[i * GW + h * HEAD_DIM:i * GW + (h + 1) * HEAD_DIM][None]

            c1 = cs_ref[t, h:h + 1][None]
            c2 = cs_ref[t, N_HEADS + h:N_HEADS + h + 1][None]
            vv = jnp.stack([vs_ref[t, v * N_HEADS + h:v * N_HEADS + h + 1] for v in range(vr)])
            s = s_ref[:, h]
            sa = -jnp.sum(s * vec(0), axis=1, keepdims=True)
            y0 = jnp.sum(s * vec(1), axis=1, keepdims=True)
            y = y0 + sa * c1 + vv * c2
            s_ref[:, h] = s + sa * vec(2) + vv * vec(3)
            for v in range(vr):
                ys_ref[t, v * N_HEADS + h:v * N_HEADS + h + 1] = y[v]

    ka_ref[...] = to_lanes(kt_ref[0, :, 0:step_cols])

    def step_pair(i, carry):
        advance(2 * i, ka_ref, kb_ref)
        advance(2 * i + 1, kb_ref, ka_ref)
        return carry

    lax.fori_loop(0, tb // 2, step_pair, 0)
    gamma = to_lanes(kt_ref[tb - 1, :, step_cols:RW_VECS * GW])
    s_ref[...] = s_ref[...] * gamma.reshape(N_HEADS, HEAD_DIM, LANES)[None]
    n = LANES // pw
    yt = jnp.swapaxes(ys_ref[...].reshape(tb // n, LANES, LANES), 1, 2)
    pieces = [jnp.concatenate([yt[:, x * nb:(x + 1) * nb, j * pw:(j + 1) * pw] for x in range(xg)], axis=2)
              for j in range(n)]
    y_ref[...] = jnp.swapaxes(jnp.stack(pieces, axis=1).reshape(tb, nb, GW), 0, 1)


def _rwkv_seq(kw, vp, c12):
    nb, seq, _ = kw.shape
    tb = min(RW_WINDOW, seq)
    xg = LANES // nb
    vr = HEAD_DIM // xg
    pw = N_HEADS * vr
    blk = lambda c: pl.BlockSpec((nb, tb, c), lambda i: (0, i, 0))
    return pl.pallas_call(
        functools.partial(_rwkv_seq_kernel, tb=tb, nb=nb),
        grid=(seq // tb,),
        in_specs=[blk(RW_VECS * GW), blk(GW), blk(LANES)],
        out_specs=blk(GW),
        out_shape=jax.ShapeDtypeStruct((nb, seq, GW), F32),
        scratch_shapes=[pltpu.VMEM((vr, N_HEADS, HEAD_DIM, LANES), F32),
                        pltpu.VMEM((tb, nb, RW_VECS * GW), F32),
                        pltpu.VMEM((RW_STEP_VECS * GW, LANES), F32),
                        pltpu.VMEM((RW_STEP_VECS * GW, LANES), F32),
                        pltpu.VMEM((tb, pw, LANES), F32), pltpu.VMEM((tb, 2 * N_HEADS, LANES), F32),
                        pltpu.VMEM((tb, pw, LANES), F32)],
        compiler_params=_params("arbitrary"),
        name="rwkv_seq",
    )(kw, vp, c12)


def _bias_kernel(rb_ref, bk_ref, o_ref):
    h = pl.program_id(0)
    bk = bk_ref[0]
    out = jnp.full(bk.shape, -jnp.inf, F32)
    for b in range(REL_BUCKETS):
        out = jnp.where(bk == b, rb_ref[b, h], out)
    o_ref[0, 0] = out * LOG2E


def _bias_tiles(rel_bias, buckets):
    bs = MOBA_BLOCK
    return pl.pallas_call(
        _bias_kernel,
        grid=(N_HEADS, 2),
        in_specs=[pl.BlockSpec(memory_space=pltpu.SMEM), pl.BlockSpec((1, bs, bs), lambda h, k: (k, 0, 0))],
        out_specs=pl.BlockSpec((1, 1, bs, bs), lambda h, k: (h, k, 0, 0)),
        out_shape=jax.ShapeDtypeStruct((N_HEADS, 2, bs, bs), F32),
        name="moba_bias",
    )(rel_bias, buckets)


def _moba_kernel(far_ref, k_ref, qt_ref, vt_ref, bias_ref, o_ref, s_ref, p_ref, *, seq, nsel):
    bs = MOBA_BLOCK
    nb = seq // bs
    h = pl.program_id(1)
    k = k_ref[...]
    qt = qt_ref[...] * (HEAD_DIM ** -0.5)
    kmean = jnp.mean(k.reshape(nb, bs, HEAD_DIM), axis=1)
    gate = _hdot(kmean, qt)
    jj = lax.broadcasted_iota(jnp.int32, (nb, seq), 0)
    qblk = lax.broadcasted_iota(jnp.int32, (nb, seq), 1) // bs
    g = jnp.where(jj < qblk, gate, -jnp.inf)
    selb = jnp.full((nb, seq), -jnp.inf, F32)
    for r in range(nsel):
        m = jnp.max(g, axis=0, keepdims=True)
        idx = jnp.min(jnp.where(g == m, jj, nb), axis=0, keepdims=True)
        hit = jj == idx
        selb = jnp.where(jnp.logical_and(hit, qblk > r), 0.0, selb)
        g = jnp.where(hit, -jnp.inf, g)
    selfar = selb + far_ref[h] * LOG2E
    kb = k.astype(BF16)
    qtb = (qt * LOG2E).astype(BF16)
    vtb = jnp.concatenate([vt_ref[...].astype(BF16), jnp.ones((SUBLANES, seq), BF16)], axis=0)
    for i in range(nb):
        qs = slice(i * bs, (i + 1) * bs)
        q_i = qtb[:, qs]
        mx = None
        for j in range(i + 1):
            ks = slice(j * bs, (j + 1) * bs)
            s = jnp.dot(kb[ks, :], q_i, preferred_element_type=F32)
            if j == i:
                s = s + bias_ref[0, 0]
            elif j == i - 1:
                s = s + bias_ref[0, 1] + selb[j:j + 1, qs]
            else:
                s = s + selfar[j:j + 1, qs]
            s_ref[ks, :] = s
            mx = s if mx is None else jnp.maximum(mx, s)
        m = jnp.max(mx, axis=0, keepdims=True)
        for j in range(i + 1):
            ks = slice(j * bs, (j + 1) * bs)
            p_ref[ks, :] = jnp.exp2(s_ref[ks, :] - m).astype(BF16)
        acc = jnp.dot(vtb[:, 0:(i + 1) * bs], p_ref[0:(i + 1) * bs, :], preferred_element_type=F32)
        o_ref[:, qs] = acc[0:HEAD_DIM] / acc[HEAD_DIM:HEAD_DIM + 1]


def _moba(k4, qv_t, bias_tiles, far):
    bsz, _, seq, _ = k4.shape
    nsel = max(1, min(MOBA_TOPK, seq // MOBA_BLOCK - 1))
    return pl.pallas_call(
        functools.partial(_moba_kernel, seq=seq, nsel=nsel),
        grid=(bsz, N_HEADS),
        in_specs=[pl.BlockSpec(memory_space=pltpu.SMEM),
                  pl.BlockSpec((None, None, seq, HEAD_DIM), lambda b, h: (b, h, 0, 0)),
                  pl.BlockSpec((None, HEAD_DIM, seq), lambda b, h: (b, h, 0)),
                  pl.BlockSpec((None, HEAD_DIM, seq), lambda b, h: (b, N_HEADS + h, 0)),
                  pl.BlockSpec((1, 2, MOBA_BLOCK, MOBA_BLOCK), lambda b, h: (h, 0, 0, 0))],
        out_specs=pl.BlockSpec((None, HEAD_DIM, seq), lambda b, h: (b, h, 0)),
        out_shape=jax.ShapeDtypeStruct((bsz, GW, seq), F32),
        scratch_shapes=[pltpu.VMEM((seq, MOBA_BLOCK), F32), pltpu.VMEM((seq, MOBA_BLOCK), BF16)],
        compiler_params=_params("parallel", "parallel"),
        name="moba",
    )(far, k4, qv_t, qv_t, bias_tiles)


def _t5_bucket(rel):
    n = jnp.maximum(rel, 0)
    max_exact = REL_BUCKETS // 2
    nf = jnp.maximum(n, 1).astype(F32)
    large = max_exact + (jnp.log(nf / max_exact) / math.log(REL_MAX_DIST / max_exact)
                         * (REL_BUCKETS - max_exact)).astype(jnp.int32)
    return jnp.where(n < max_exact, n, jnp.minimum(large, REL_BUCKETS - 1))


def _bucket_tiles():
    kpos = jnp.arange(MOBA_BLOCK)[:, None]
    qpos = jnp.arange(MOBA_BLOCK)[None, :]
    rel = qpos - kpos
    own = jnp.where(rel >= 0, _t5_bucket(rel), -1)
    prev = _t5_bucket(rel + MOBA_BLOCK)
    return jnp.stack([own, prev]).astype(jnp.int32)


def _mixout_kernel(ys5_ref, yml_ref, yrw_ref, bonus_ref, g_ref, ymbt_ref, lnw_ref, lnb_ref, bd_ref,
                   gain_ref, w_ref, gpost_ref, x_ref, o_ref):
    bd = bd_ref[...]
    y = yrw_ref[...]
    mean = _sum2(y, bd) * (1.0 / HEAD_DIM)
    d = y - mean
    var = _sum2(d * d, bd) * (1.0 / HEAD_DIM)
    yrw = (d * lax.rsqrt(var + RW_GN_EPS) * lnw_ref[...] + lnb_ref[...] + bonus_ref[...]) * g_ref[...]
    acc = None
    for i, yg in enumerate((ys5_ref[...], yml_ref[...], yrw, ymbt_ref[...].T)):
        cols = slice(i * GW, (i + 1) * GW)
        part = jnp.dot(_rms(yg, gain_ref[:, cols]).astype(BF16), w_ref[cols, :], preferred_element_type=F32)
        acc = part if acc is None else acc + part
    o_ref[...] = x_ref[...] + _rms(acc, gpost_ref[...])


def _mixout(ys5, yml, yrw, bonus, g, ymb_t, ln_w, ln_b, bd, gain, w, gpost, x2d, seq):
    t = x2d.shape[0]
    tm = min(ROW_TILE, seq)
    per = seq // tm
    vec = lambda c: _const_spec((1, c))
    return pl.pallas_call(
        _mixout_kernel,
        grid=(t // tm,),
        in_specs=[_row_spec(tm, GW)] * 5
        + [pl.BlockSpec((None, GW, tm), lambda i: (i // per, 0, i % per)),
           vec(GW), vec(GW), _const_spec((GW, GW)), vec(D_MODEL),
           _const_spec((D_MODEL, D_MODEL)), vec(D_MODEL), _row_spec(tm, D_MODEL)],
        out_specs=_row_spec(tm, D_MODEL),
        out_shape=jax.ShapeDtypeStruct((t, D_MODEL), F32),
        compiler_params=_params("parallel"),
        name="mixout",
    )(ys5, yml, yrw, bonus, g, ymb_t, ln_w, ln_b, bd, gain, w, gpost, x2d)


def _kv_kernel(m_ref, g_ref, w_ref, o_ref):
    h = _rms(m_ref[0], g_ref[...]).astype(BF16)
    o_ref[0] = jnp.dot(h, w_ref[...], preferred_element_type=F32).astype(BF16)


def _kv(mem, g, w):
    bsz, m, _ = mem.shape
    return pl.pallas_call(
        _kv_kernel,
        grid=(bsz,),
        in_specs=[pl.BlockSpec((1, m, D_MODEL), lambda b: (b, 0, 0)), _const_spec((1, D_MODEL)),
                  _const_spec((D_MODEL, 2 * D_MODEL))],
        out_specs=pl.BlockSpec((1, m, 2 * D_MODEL), lambda b: (b, 0, 0)),
        out_shape=jax.ShapeDtypeStruct((bsz, m, 2 * D_MODEL), BF16),
        compiler_params=_params("parallel"),
        name="xa_kv",
    )(mem, g, w)


def _xattn_kernel(x_ref, kv_ref, gpre_ref, wq_ref, wo_ref, gpost_ref, o_ref):
    x = x_ref[...]
    q = jnp.dot(_rms(x, gpre_ref[...]).astype(BF16), wq_ref[...], preferred_element_type=F32)
    acc = None
    for hd in range(XA_HEADS):
        cols = slice(hd * XA_HEAD_DIM, (hd + 1) * XA_HEAD_DIM)
        vcols = slice(D_MODEL + hd * XA_HEAD_DIM, D_MODEL + (hd + 1) * XA_HEAD_DIM)
        s = lax.dot_general(q[:, cols].astype(BF16), kv_ref[0, :, cols], NT_DIMS,
                            preferred_element_type=F32) * (XA_HEAD_DIM ** -0.5)
        p = jnp.exp(s - jnp.max(s, axis=-1, keepdims=True))
        o = jnp.dot(p.astype(BF16), kv_ref[0, :, vcols], preferred_element_type=F32)
        o = o / jnp.sum(p, axis=-1, keepdims=True)
        part = jnp.dot(o.astype(BF16), wo_ref[cols, :], preferred_element_type=F32)
        acc = part if acc is None else acc + part
    o_ref[...] = x + _rms(acc, gpost_ref[...])


def _xattn(x2d, kv, gpre, wq, wo, gpost, seq):
    t = x2d.shape[0]
    m = kv.shape[1]
    tm = min(ROW_TILE, seq)
    per = seq // tm
    vec = _const_spec((1, D_MODEL))
    sq = _const_spec((D_MODEL, D_MODEL))
    return pl.pallas_call(
        _xattn_kernel,
        grid=(t // tm,),
        in_specs=[_row_spec(tm, D_MODEL), pl.BlockSpec((1, m, 2 * D_MODEL), lambda i: (i // per, 0, 0)),
                  vec, sq, sq, vec],
        out_specs=_row_spec(tm, D_MODEL),
        out_shape=jax.ShapeDtypeStruct((t, D_MODEL), F32),
        compiler_params=_params("parallel"),
        name="xattn",
    )(x2d, kv, gpre, wq, wo, gpost)


def _ffn_kernel(x_ref, gpre_ref, wi_ref, wo_ref, gpost_ref, o_ref):
    x = x_ref[...]
    h = _rms(x, gpre_ref[...]).astype(BF16)
    step = FFN_CHUNK
    acc = None
    for c in range(D_FF // step):
        gate = jnp.dot(h, wi_ref[:, c * step:(c + 1) * step], preferred_element_type=F32)
        up = jnp.dot(h, wi_ref[:, D_FF + c * step:D_FF + (c + 1) * step], preferred_element_type=F32)
        act = (gate * _sigmoid(gate) * up).astype(BF16)
        part = jnp.dot(act, wo_ref[c * step:(c + 1) * step, :], preferred_element_type=F32)
        acc = part if acc is None else acc + part
    o_ref[...] = x + _rms(acc, gpost_ref[...])


def _ffn(x2d, gpre, wi, wo, gpost):
    t = x2d.shape[0]
    tm = min(FFN_ROW_TILE, t)
    vec = _const_spec((1, D_MODEL))
    once = pl.Buffered(1)
    return pl.pallas_call(
        _ffn_kernel,
        grid=(t // tm,),
        in_specs=[_row_spec(tm, D_MODEL), vec,
                  pl.BlockSpec((D_MODEL, 2 * D_FF), lambda i: (0, 0), pipeline_mode=once),
                  pl.BlockSpec((D_FF, D_MODEL), lambda i: (0, 0), pipeline_mode=once), vec],
        out_specs=_row_spec(tm, D_MODEL),
        out_shape=jax.ShapeDtypeStruct((t, D_MODEL), F32),
        compiler_params=_params("parallel"),
        name="ffn",
    )(x2d, gpre, wi, wo, gpost)


def _block_diag(blocks):
    g, a, b = blocks.shape
    eye = jnp.eye(g, dtype=blocks.dtype)
    return jnp.einsum('gab,gk->gakb', blocks, eye).reshape(g * a, g * b)


def kernel(x, mem, rel_bias, norm_pre_mix, norm_post_mix, norm_pre_xa, norm_post_xa, norm_pre_ffn,
           norm_post_ffn, norm_mem, w_in, mix_out_gain, w_out, s5_lam_re, s5_lam_im, s5_log_step,
           s5_b_re, s5_b_im, s5_c_re, s5_c_im, s5_d, s5_w_glu, s5_b_glu, ml_conv, ml_i_bias, ml_f_bias,
           rw_mu, rw_w0, rw_w2, rw_a0, rw_a2, rw_g2, rw_k_k, rw_k_a, rw_r_k, rw_ln_w, rw_ln_b,
           xa_wq, xa_wkv, xa_wo, ffn_w_in, ffn_w_out):
    bsz, seq, _ = x.shape
    depth = w_in.shape[0]
    t = bsz * seq
    row = lambda a: a.reshape(1, -1)

    ml_end = GW + 4 * GW
    rw_lo = ml_end + 2 * N_HEADS
    mb_lo = rw_lo + 4 * GW
    perm = _rwkv_value_perm(bsz)
    w_proj = jnp.concatenate(
        [w_in[:, :, :ml_end], w_in[:, :, rw_lo:mb_lo], w_in[:, :, rw_lo + 2 * GW:rw_lo + 3 * GW][:, :, perm],
         w_in[:, :, mb_lo + GW:mb_lo + 2 * GW], w_in[:, :, ml_end:rw_lo],
         jnp.zeros((depth, D_MODEL, LANES - 2 * N_HEADS), w_in.dtype)],
        axis=2).astype(BF16)
    w_qv_t = jnp.concatenate([w_in[:, :, mb_lo:mb_lo + GW], w_in[:, :, mb_lo + 2 * GW:mb_lo + 3 * GW]],
                             axis=2).transpose(0, 2, 1).astype(BF16)
    rw_rows = 2 * GW + perm
    w_out_b = jnp.concatenate([w_out[:, :2 * GW], w_out[:, rw_rows], w_out[:, 3 * GW:]], axis=1).astype(BF16)
    gain_p = jnp.concatenate([mix_out_gain[:, :2 * GW], mix_out_gain[:, rw_rows], mix_out_gain[:, 3 * GW:]],
                             axis=1)
    mu_p = jnp.concatenate([rw_mu, rw_mu[:, 2 * GW:3 * GW][:, perm]], axis=1)
    wq_b, wkv_b, wo_b = xa_wq.astype(BF16), xa_wkv.astype(BF16), xa_wo.astype(BF16)
    ffn_wi_b, ffn_wo_b = ffn_w_in.astype(BF16), ffn_w_out.astype(BF16)
    w_glu_b = s5_w_glu.astype(BF16)

    ng = depth * S5_GROUPS
    ab_re, ab_im, bb_re, bb_im = _s5_params(
        s5_lam_re.reshape(ng, S5_STATE), s5_lam_im.reshape(ng, S5_STATE), s5_log_step.reshape(ng, 1),
        s5_b_re.transpose(0, 1, 3, 2).reshape(ng, S5_GROUP, S5_STATE),
        s5_b_im.transpose(0, 1, 3, 2).reshape(ng, S5_GROUP, S5_STATE))
    ab_re = ab_re.reshape(depth, 1, S5_P)
    ab_im = ab_im.reshape(depth, 1, S5_P)
    bb_re = bb_re.reshape(depth, S5_GROUPS, S5_GROUP, S5_STATE)
    bb_im = bb_im.reshape(depth, S5_GROUPS, S5_GROUP, S5_STATE)

    bd = _block_diag(jnp.ones((N_HEADS, HEAD_DIM, HEAD_DIM), BF16))
    head_of_lane = jnp.arange(GW) // HEAD_DIM
    sela = (head_of_lane[:, None] == jnp.arange(LANES)[None, :]).astype(BF16)
    selb = (head_of_lane[:, None] + N_HEADS == jnp.arange(LANES)[None, :]).astype(BF16)
    zeros64 = jnp.zeros((HEAD_DIM, GW), F32)
    head_of_perm = jnp.asarray(perm // HEAD_DIM)
    bd_np = (head_of_lane[:, None] == head_of_perm[None, :]).astype(BF16)
    bd_pp = (head_of_perm[:, None] == head_of_perm[None, :]).astype(BF16)

    bias_tiles = _bias_tiles(rel_bias, _bucket_tiles())
    far = rel_bias[REL_BUCKETS - 1, :]

    x2d = x.reshape(t, D_MODEL)
    for l in range(depth):
        u_s5, p_ml, p_rw, gates, k4, qv_t = _proj(x2d, row(norm_pre_mix[l]), w_proj[l], w_qv_t[l], bsz, seq)

        gate_bias = jnp.concatenate([ml_i_bias[l], ml_f_bias[l], jnp.zeros((LANES - 2 * N_HEADS,), F32)])
        y_ml = _mlstm(p_ml.reshape(bsz, seq, 4 * GW), gates.reshape(bsz, seq, LANES), ml_conv[l],
                      row(gate_bias)).reshape(t, GW)

        w2p = jnp.concatenate([rw_w2[l], zeros64], axis=0)
        a2p = jnp.concatenate([zeros64, rw_a2[l]], axis=0)
        kw, v_rw, c12, bonus, g_rw = _rwkv_pre(
            p_rw.reshape(bsz, seq, RW_COLS), row(mu_p[l]), row(rw_w0[l]), w2p, row(rw_a0[l]), a2p,
            rw_g2[l][:, perm], row(rw_k_k[l]), row(rw_k_a[l]), row(rw_r_k[l]), bd, bd_np, sela, selb)
        y_rw = _rwkv_seq(kw, v_rw, c12).reshape(t, GW)

        y_mb_t = _moba(k4, qv_t, bias_tiles, far)

        wb = jnp.concatenate([_block_diag(bb_re[l]), _block_diag(bb_im[l])], axis=1).astype(BF16)
        wc = jnp.concatenate([_block_diag(s5_c_re[l].transpose(0, 2, 1)),
                              -_block_diag(s5_c_im[l].transpose(0, 2, 1))], axis=0).astype(BF16)
        y_s5 = _s5(u_s5.reshape(bsz, seq, GW), wb, ab_re[l], ab_im[l], wc, row(s5_d[l]), w_glu_b[l],
                   row(s5_b_glu[l])).reshape(t, GW)

        x2d = _mixout(y_s5, y_ml, y_rw, bonus.reshape(t, GW), g_rw.reshape(t, GW), y_mb_t,
                      row(rw_ln_w[l][perm]), row(rw_ln_b[l][perm]), bd_pp, row(gain_p[l]), w_out_b[l],
                      row(norm_post_mix[l]), x2d, seq)

        kv = _kv(mem, row(norm_mem[l]), wkv_b[l])
        x2d = _xattn(x2d, kv, row(norm_pre_xa[l]), wq_b[l], wo_b[l], row(norm_post_xa[l]), seq)
        x2d = _ffn(x2d, row(norm_pre_ffn[l]), ffn_wi_b[l], ffn_wo_b[l], row(norm_post_ffn[l]))
    return x2d.reshape(bsz, seq, D_MODEL)
```
